```python
import jax, jax.numpy as jnp
from jax import lax
import numpy as np

D_MODEL = 4096
BATCH = 8
SEQ = 4096
DEPTH = 1

MIX_WIDTH = D_MODEL
POOL_WIDTH = MIX_WIDTH // 2
CONV_WIDTH = MIX_WIDTH - POOL_WIDTH
POOL_WINDOWS = (2, 4, 8, 16)
N_POOL_GROUPS = len(POOL_WINDOWS)
POOL_GROUP_DIM = POOL_WIDTH // N_POOL_GROUPS
CONV_HEAD_DIM = 128
CONV_HEADS = CONV_WIDTH // CONV_HEAD_DIM
CONV_WIDTH_K = 3
IN_PROJ_WIDTH = POOL_WIDTH + 3 * CONV_WIDTH
D_FF = 4 * D_MODEL
N_MOD = 6
EPS = 1e-6

kernel_name = "hybrid_pool_shortconv_adaln_block"


def rmsnorm(x, g):
    xf = x.astype(jnp.float32)
    xn = xf * lax.rsqrt(jnp.mean(xf * xf, axis=-1, keepdims=True) + EPS)
    return xn.astype(x.dtype) * g


def group_rmsnorm(x, g, n_groups):
    b, s, w = x.shape
    xg = x.reshape(b, s, n_groups, w // n_groups).astype(jnp.float32)
    xn = xg * lax.rsqrt(jnp.mean(xg * xg, axis=-1, keepdims=True) + EPS)
    return xn.reshape(b, s, w).astype(x.dtype) * g


def modulate(h, shift, scale):
    return h * (1 + scale[:, None, :]) + shift[:, None, :]


def multiscale_pool(v):
    b, s, _ = v.shape
    vg = v.reshape(b, s, N_POOL_GROUPS, POOL_GROUP_DIM)
    cs = jnp.cumsum(vg.astype(jnp.float32), axis=1)
    cs = jnp.pad(cs, ((0, 0), (1, 0), (0, 0), (0, 0)))
    half = jnp.array(POOL_WINDOWS, dtype=jnp.int32) // 2
    t = jnp.arange(s, dtype=jnp.int32)[:, None]
    lo = jnp.clip(t - half[None, :], 0, s)
    hi = jnp.clip(t + half[None, :], 0, s)
    gidx = jnp.arange(N_POOL_GROUPS, dtype=jnp.int32)[None, :]
    win_sum = cs[:, hi, gidx, :] - cs[:, lo, gidx, :]
    count = (hi - lo).astype(jnp.float32)[None, :, :, None]
    out = win_sum / count - vg.astype(jnp.float32)
    return out.astype(v.dtype)


def depthwise_conv3_centred(u, w, bias):
    up = jnp.pad(u, ((0, 0), (1, 1), (0, 0)))
    return w[0] * up[:, :-2] + w[1] * up[:, 1:-1] + w[2] * up[:, 2:] + bias


def _fwd_setup_inputs(seed: int = 0) -> dict:
    key = jax.random.key(seed)
    ks = jax.random.split(key, 20)
    f32 = jnp.float32
    L, D = DEPTH, D_MODEL

    def nrm(k, shape, fan_in):
        return jax.random.normal(k, shape, f32) * (fan_in ** -0.5)

    def gain(k, shape):
        return 1.0 + 0.1 * jax.random.normal(k, shape, f32)

    return {
        "x": jax.random.normal(ks[0], (BATCH, SEQ, D), f32),
        "c": jax.random.normal(ks[1], (BATCH, D), f32),
        "w_ada": nrm(ks[2], (L, D, N_MOD * D), D) * 0.5,
        "b_ada": 0.02 * jax.random.normal(ks[3], (L, N_MOD * D), f32),
        "norm1_g": gain(ks[4], (L, D)),
        "w_in": nrm(ks[5], (L, D, IN_PROJ_WIDTH), D),
        "pool_mix_w": nrm(ks[6], (L, N_POOL_GROUPS, POOL_GROUP_DIM, POOL_GROUP_DIM), POOL_GROUP_DIM),
        "pool_scale": gain(ks[7], (L, POOL_WIDTH)),
        "conv_w": nrm(ks[8], (L, CONV_WIDTH_K, CONV_WIDTH), CONV_WIDTH_K),
        "conv_b": 0.02 * jax.random.normal(ks[9], (L, CONV_WIDTH), f32),
        "gnorm_pool_g": gain(ks[10], (L, POOL_WIDTH)),
        "gnorm_conv_g": gain(ks[11], (L, CONV_WIDTH)),
        "w_out": nrm(ks[12], (L, MIX_WIDTH, D), MIX_WIDTH),
        "norm2_g": gain(ks[13], (L, D)),
        "w_mlp_in": nrm(ks[14], (L, D, D_FF), D),
        "w_mlp_out": nrm(ks[15], (L, D_FF, D), D_FF),
        "final_g": gain(ks[16], (D,)),
    }


def _fwd_reference(x, c, w_ada, b_ada, norm1_g, w_in, pool_mix_w, pool_scale, conv_w, conv_b,
              gnorm_pool_g, gnorm_conv_g, w_out, norm2_g, w_mlp_in, w_mlp_out, final_g):
    c_act = jax.nn.silu(c)
    for l in range(DEPTH):
        mod = c_act @ w_ada[l] + b_ada[l]
        shift1, scale1, gate1, shift2, scale2, gate2 = jnp.split(mod, N_MOD, axis=-1)

        h = modulate(rmsnorm(x, norm1_g[l]), shift1, scale1)
        proj = jnp.einsum("bsd,de->bse", h, w_in[l])
        v_pool = proj[..., :POOL_WIDTH]
        b_gate, c_gate, u = jnp.split(proj[..., POOL_WIDTH:], 3, axis=-1)

        pooled = multiscale_pool(v_pool)
        a_out = jnp.einsum("bsgd,gde->bsge", pooled, pool_mix_w[l])
        a_out = a_out.reshape(x.shape[0], x.shape[1], POOL_WIDTH) * pool_scale[l]

        b_out = b_gate * depthwise_conv3_centred(c_gate * u, conv_w[l], conv_b[l])

        mixed = jnp.concatenate(
            [group_rmsnorm(a_out, gnorm_pool_g[l], N_POOL_GROUPS),
             group_rmsnorm(b_out, gnorm_conv_g[l], CONV_HEADS)], axis=-1)
        x = x + gate1[:, None, :] * jnp.einsum("bse,ed->bsd", mixed, w_out[l])

        h = modulate(rmsnorm(x, norm2_g[l]), shift2, scale2)
        hid = jnp.square(jax.nn.relu(jnp.einsum("bsd,df->bsf", h, w_mlp_in[l])))
        x = x + gate2[:, None, :] * jnp.einsum("bsf,fd->bsd", hid, w_mlp_out[l])

    return rmsnorm(x, final_g)


import jax as _jax
import jax.numpy as _jnp

TWIN_FORMAT = 'train_step'
FWD_PARAMS = ['x', 'c', 'w_ada', 'b_ada', 'norm1_g', 'w_in', 'pool_mix_w', 'pool_scale', 'conv_w', 'conv_b', 'gnorm_pool_g', 'gnorm_conv_g', 'w_out', 'norm2_g', 'w_mlp_in', 'w_mlp_out', 'final_g']
TWIN_WEIGHTS = ['w_ada', 'b_ada', 'norm1_g', 'w_in', 'pool_mix_w', 'pool_scale', 'conv_w', 'conv_b', 'gnorm_pool_g', 'gnorm_conv_g', 'w_out', 'norm2_g', 'w_mlp_in', 'w_mlp_out', 'final_g']
TWIN_DIFF_INPUT = 'x'
TWIN_INPUTS = ['x', 'c', 'w_ada', 'b_ada', 'norm1_g', 'w_in', 'pool_mix_w', 'pool_scale', 'conv_w', 'conv_b', 'gnorm_pool_g', 'gnorm_conv_g', 'w_out', 'norm2_g', 'w_mlp_in', 'w_mlp_out', 'final_g', 'loss_target', 'm_w_ada', 'm_b_ada', 'm_norm1_g', 'm_w_in', 'm_pool_mix_w', 'm_pool_scale', 'm_conv_w', 'm_conv_b', 'm_gnorm_pool_g', 'm_gnorm_conv_g', 'm_w_out', 'm_norm2_g', 'm_w_mlp_in', 'm_w_mlp_out', 'm_final_g', 'v_w_ada', 'v_b_ada', 'v_norm1_g', 'v_w_in', 'v_pool_mix_w', 'v_pool_scale', 'v_conv_w', 'v_conv_b', 'v_gnorm_pool_g', 'v_gnorm_conv_g', 'v_w_out', 'v_norm2_g', 'v_w_mlp_in', 'v_w_mlp_out', 'v_final_g']
TWIN_OUTPUTS = ['loss', 'grad_x', 'grad_w_ada', 'grad_b_ada', 'grad_norm1_g', 'grad_w_in', 'grad_pool_mix_w', 'grad_pool_scale', 'grad_conv_w', 'grad_conv_b', 'grad_gnorm_pool_g', 'grad_gnorm_conv_g', 'grad_w_out', 'grad_norm2_g', 'grad_w_mlp_in', 'grad_w_mlp_out', 'grad_final_g', 'delta_w_ada', 'delta_b_ada', 'delta_norm1_g', 'delta_w_in', 'delta_pool_mix_w', 'delta_pool_scale', 'delta_conv_w', 'delta_conv_b', 'delta_gnorm_pool_g', 'delta_gnorm_conv_g', 'delta_w_out', 'delta_norm2_g', 'delta_w_mlp_in', 'delta_w_mlp_out', 'delta_final_g', 'new_m_w_ada', 'new_m_b_ada', 'new_m_norm1_g', 'new_m_w_in', 'new_m_pool_mix_w', 'new_m_pool_scale', 'new_m_conv_w', 'new_m_conv_b', 'new_m_gnorm_pool_g', 'new_m_gnorm_conv_g', 'new_m_w_out', 'new_m_norm2_g', 'new_m_w_mlp_in', 'new_m_w_mlp_out', 'new_m_final_g', 'new_v_w_ada', 'new_v_b_ada', 'new_v_norm1_g', 'new_v_w_in', 'new_v_pool_mix_w', 'new_v_pool_scale', 'new_v_conv_w', 'new_v_conv_b', 'new_v_gnorm_pool_g', 'new_v_gnorm_conv_g', 'new_v_w_out', 'new_v_norm2_g', 'new_v_w_mlp_in', 'new_v_w_mlp_out', 'new_v_final_g']
TWIN_LEAF_KINDS = {'loss': 'loss', 'grad_x': 'grad_x', 'grad_w_ada': 'grad_w', 'grad_b_ada': 'grad_w', 'grad_norm1_g': 'grad_w', 'grad_w_in': 'grad_w', 'grad_pool_mix_w': 'grad_w', 'grad_pool_scale': 'grad_w', 'grad_conv_w': 'grad_w', 'grad_conv_b': 'grad_w', 'grad_gnorm_pool_g': 'grad_w', 'grad_gnorm_conv_g': 'grad_w', 'grad_w_out': 'grad_w', 'grad_norm2_g': 'grad_w', 'grad_w_mlp_in': 'grad_w', 'grad_w_mlp_out': 'grad_w', 'grad_final_g': 'grad_w', 'delta_w_ada': 'delta_w', 'delta_b_ada': 'delta_w', 'delta_norm1_g': 'delta_w', 'delta_w_in': 'delta_w', 'delta_pool_mix_w': 'delta_w', 'delta_pool_scale': 'delta_w', 'delta_conv_w': 'delta_w', 'delta_conv_b': 'delta_w', 'delta_gnorm_pool_g': 'delta_w', 'delta_gnorm_conv_g': 'delta_w', 'delta_w_out': 'delta_w', 'delta_norm2_g': 'delta_w', 'delta_w_mlp_in': 'delta_w', 'delta_w_mlp_out': 'delta_w', 'delta_final_g': 'delta_w', 'new_m_w_ada': 'new_m', 'new_m_b_ada': 'new_m', 'new_m_norm1_g': 'new_m', 'new_m_w_in': 'new_m', 'new_m_pool_mix_w': 'new_m', 'new_m_pool_scale': 'new_m', 'new_m_conv_w': 'new_m', 'new_m_conv_b': 'new_m', 'new_m_gnorm_pool_g': 'new_m', 'new_m_gnorm_conv_g': 'new_m', 'new_m_w_out': 'new_m', 'new_m_norm2_g': 'new_m', 'new_m_w_mlp_in': 'new_m', 'new_m_w_mlp_out': 'new_m', 'new_m_final_g': 'new_m', 'new_v_w_ada': 'new_v', 'new_v_b_ada': 'new_v', 'new_v_norm1_g': 'new_v', 'new_v_w_in': 'new_v', 'new_v_pool_mix_w': 'new_v', 'new_v_pool_scale': 'new_v', 'new_v_conv_w': 'new_v', 'new_v_conv_b': 'new_v', 'new_v_gnorm_pool_g': 'new_v', 'new_v_gnorm_conv_g': 'new_v', 'new_v_w_out': 'new_v', 'new_v_norm2_g': 'new_v', 'new_v_w_mlp_in': 'new_v', 'new_v_w_mlp_out': 'new_v', 'new_v_final_g': 'new_v'}


def _forward(args):
    return _fwd_reference(*[args[k] for k in FWD_PARAMS])


def _output_shape():
    out = _jax.eval_shape(lambda: _forward(_fwd_setup_inputs(0)))
    return out.shape, out.dtype

N_MICROBATCH = 1
ADAM_LR = 0.001
ADAM_B1 = 0.9
ADAM_B2 = 0.999
ADAM_EPS = 1e-08
ADAM_WD = 0.01
ADAM_STEP = 10
PER_EXAMPLE_BATCH_AXIS = {'x': 0, 'c': 0, 'loss_target': 0}
SHARED_INPUTS = []
_WEIGHT_DTYPES = {'w_ada': _jnp.float32, 'b_ada': _jnp.float32, 'norm1_g': _jnp.float32, 'w_in': _jnp.float32, 'pool_mix_w': _jnp.float32, 'pool_scale': _jnp.float32, 'conv_w': _jnp.float32, 'conv_b': _jnp.float32, 'gnorm_pool_g': _jnp.float32, 'gnorm_conv_g': _jnp.float32, 'w_out': _jnp.float32, 'norm2_g': _jnp.float32, 'w_mlp_in': _jnp.float32, 'w_mlp_out': _jnp.float32, 'final_g': _jnp.float32}
MOMENT_SCALE = {'w_ada': 9.094243e-02, 'b_ada': 1.759078e-01, 'norm1_g': 1.908836e-02, 'w_in': 1.388861e-02, 'pool_mix_w': 1.369620e-02, 'pool_scale': 1.389999e-02, 'conv_w': 1.508898e-02, 'conv_b': 1.288899e-02, 'gnorm_pool_g': 1.397615e-02, 'gnorm_conv_g': 1.345252e-02, 'w_out': 1.365402e-02, 'norm2_g': 1.989210e-02, 'w_mlp_in': 1.131458e-02, 'w_mlp_out': 3.860760e-02, 'final_g': 8.088010e+00}


def _to_microbatches(a, axis):
    t = _jnp.moveaxis(a, axis, 0)
    t = t.reshape((N_MICROBATCH, t.shape[0] // N_MICROBATCH) + t.shape[1:])
    return _jnp.moveaxis(t, 1, axis + 1)


def setup_inputs(seed: int = 0) -> dict:
    inp = _fwd_setup_inputs(seed)
    key = _jax.random.fold_in(_jax.random.key(seed), 7919)
    shape, _ = _output_shape()
    out = dict(inp)
    out["loss_target"] = _jax.random.normal(_jax.random.fold_in(key, 0), shape, _jnp.float32)
    for i, name in enumerate(TWIN_WEIGHTS):
        w = inp[name].astype(_jnp.float32)
        if MOMENT_SCALE is None:
            s = _jnp.sqrt(_jnp.mean(_jnp.square(w)) + 1e-30)
        else:
            s = MOMENT_SCALE[name]
        km, kv = _jax.random.split(_jax.random.fold_in(key, i + 1))
        out[name] = w
        out["m_" + name] = s * _jax.random.normal(km, w.shape, _jnp.float32)
        out["v_" + name] = (s * s) * _jax.random.uniform(kv, w.shape, _jnp.float32, 0.5, 1.5)
    if N_MICROBATCH > 1:
        for name, axis in PER_EXAMPLE_BATCH_AXIS.items():
            out[name] = _to_microbatches(out[name], axis)
    return {'x': out['x'], 'c': out['c'], 'w_ada': out['w_ada'], 'b_ada': out['b_ada'], 'norm1_g': out['norm1_g'], 'w_in': out['w_in'], 'pool_mix_w': out['pool_mix_w'], 'pool_scale': out['pool_scale'], 'conv_w': out['conv_w'], 'conv_b': out['conv_b'], 'gnorm_pool_g': out['gnorm_pool_g'], 'gnorm_conv_g': out['gnorm_conv_g'], 'w_out': out['w_out'], 'norm2_g': out['norm2_g'], 'w_mlp_in': out['w_mlp_in'], 'w_mlp_out': out['w_mlp_out'], 'final_g': out['final_g'], 'loss_target': out['loss_target'], 'm_w_ada': out['m_w_ada'], 'm_b_ada': out['m_b_ada'], 'm_norm1_g': out['m_norm1_g'], 'm_w_in': out['m_w_in'], 'm_pool_mix_w': out['m_pool_mix_w'], 'm_pool_scale': out['m_pool_scale'], 'm_conv_w': out['m_conv_w'], 'm_conv_b': out['m_conv_b'], 'm_gnorm_pool_g': out['m_gnorm_pool_g'], 'm_gnorm_conv_g': out['m_gnorm_conv_g'], 'm_w_out': out['m_w_out'], 'm_norm2_g': out['m_norm2_g'], 'm_w_mlp_in': out['m_w_mlp_in'], 'm_w_mlp_out': out['m_w_mlp_out'], 'm_final_g': out['m_final_g'], 'v_w_ada': out['v_w_ada'], 'v_b_ada': out['v_b_ada'], 'v_norm1_g': out['v_norm1_g'], 'v_w_in': out['v_w_in'], 'v_pool_mix_w': out['v_pool_mix_w'], 'v_pool_scale': out['v_pool_scale'], 'v_conv_w': out['v_conv_w'], 'v_conv_b': out['v_conv_b'], 'v_gnorm_pool_g': out['v_gnorm_pool_g'], 'v_gnorm_conv_g': out['v_gnorm_conv_g'], 'v_w_out': out['v_w_out'], 'v_norm2_g': out['v_norm2_g'], 'v_w_mlp_in': out['v_w_mlp_in'], 'v_w_mlp_out': out['v_w_mlp_out'], 'v_final_g': out['v_final_g']}


def _loss(weights, diff, rest, loss_target):
    with _jax.named_scope("forward"):
        args = {**rest, TWIN_DIFF_INPUT: diff, **{k: w.astype(_WEIGHT_DTYPES[k]) for k, w in weights.items()}}
        y = _forward(args)
    with _jax.named_scope("loss_head"):
        err = _jnp.square(y.astype(_jnp.float32) - loss_target)
        return 0.5 * _jnp.sum(_jnp.mean(err, axis=-1)) if err.ndim else 0.5 * err


def _adamw(w, g, m, v):
    m = ADAM_B1 * m + (1.0 - ADAM_B1) * g
    v = ADAM_B2 * v + (1.0 - ADAM_B2) * _jnp.square(g)
    m_hat = m / (1.0 - ADAM_B1 ** ADAM_STEP)
    v_hat = v / (1.0 - ADAM_B2 ** ADAM_STEP)
    delta = -ADAM_LR * (m_hat / (_jnp.sqrt(v_hat) + ADAM_EPS) + ADAM_WD * w)
    return delta, m, v


def reference(x, c, w_ada, b_ada, norm1_g, w_in, pool_mix_w, pool_scale, conv_w, conv_b, gnorm_pool_g, gnorm_conv_g, w_out, norm2_g, w_mlp_in, w_mlp_out, final_g, loss_target, m_w_ada, m_b_ada, m_norm1_g, m_w_in, m_pool_mix_w, m_pool_scale, m_conv_w, m_conv_b, m_gnorm_pool_g, m_gnorm_conv_g, m_w_out, m_norm2_g, m_w_mlp_in, m_w_mlp_out, m_final_g, v_w_ada, v_b_ada, v_norm1_g, v_w_in, v_pool_mix_w, v_pool_scale, v_conv_w, v_conv_b, v_gnorm_pool_g, v_gnorm_conv_g, v_w_out, v_norm2_g, v_w_mlp_in, v_w_mlp_out, v_final_g):
    given = dict(x=x, c=c, w_ada=w_ada, b_ada=b_ada, norm1_g=norm1_g, w_in=w_in, pool_mix_w=pool_mix_w, pool_scale=pool_scale, conv_w=conv_w, conv_b=conv_b, gnorm_pool_g=gnorm_pool_g, gnorm_conv_g=gnorm_conv_g, w_out=w_out, norm2_g=norm2_g, w_mlp_in=w_mlp_in, w_mlp_out=w_mlp_out, final_g=final_g, loss_target=loss_target, m_w_ada=m_w_ada, m_b_ada=m_b_ada, m_norm1_g=m_norm1_g, m_w_in=m_w_in, m_pool_mix_w=m_pool_mix_w, m_pool_scale=m_pool_scale, m_conv_w=m_conv_w, m_conv_b=m_conv_b, m_gnorm_pool_g=m_gnorm_pool_g, m_gnorm_conv_g=m_gnorm_conv_g, m_w_out=m_w_out, m_norm2_g=m_norm2_g, m_w_mlp_in=m_w_mlp_in, m_w_mlp_out=m_w_mlp_out, m_final_g=m_final_g, v_w_ada=v_w_ada, v_b_ada=v_b_ada, v_norm1_g=v_norm1_g, v_w_in=v_w_in, v_pool_mix_w=v_pool_mix_w, v_pool_scale=v_pool_scale, v_conv_w=v_conv_w, v_conv_b=v_conv_b, v_gnorm_pool_g=v_gnorm_pool_g, v_gnorm_conv_g=v_gnorm_conv_g, v_w_out=v_w_out, v_norm2_g=v_norm2_g, v_w_mlp_in=v_w_mlp_in, v_w_mlp_out=v_w_mlp_out, v_final_g=v_final_g)
    weights = {n: given[n] for n in TWIN_WEIGHTS}
    shared = {n: given[n] for n in SHARED_INPUTS}
    per_example = {n: given[n] for n in ['x', 'c']}
    grad_fn = _jax.value_and_grad(_loss, argnums=(0, 1))

    def one_microbatch(ex, loss_target):
        ex = dict(ex)
        diff = ex.pop(TWIN_DIFF_INPUT)
        return grad_fn(weights, diff, {**shared, **ex}, loss_target)

    if N_MICROBATCH == 1:
        loss, (grad_w, grad_x) = one_microbatch(per_example, given["loss_target"])
    else:
        def body(carry, xs):
            loss_sum, grad_sum = carry
            l_k, (gw_k, gx_k) = one_microbatch(xs[0], xs[1])
            with _jax.named_scope("update"):
                return (loss_sum + l_k, _jax.tree.map(_jnp.add, grad_sum, gw_k)), gx_k

        init = (_jnp.zeros((), _jnp.float32), _jax.tree.map(_jnp.zeros_like, weights))
        (loss, grad_w), grad_x = _jax.lax.scan(body, init, (per_example, given["loss_target"]))
    with _jax.named_scope("update"):
        delta_w, new_m, new_v = {}, {}, {}
        for n in TWIN_WEIGHTS:
            delta_w[n], new_m[n], new_v[n] = _adamw(weights[n], grad_w[n], given["m_" + n], given["v_" + n])
    return (loss, grad_x, *[grad_w[n] for n in TWIN_WEIGHTS], *[delta_w[n] for n in TWIN_WEIGHTS],
            *[new_m[n] for n in TWIN_WEIGHTS], *[new_v[n] for n in TWIN_WEIGHTS])
```

```python
import jax
import jax.numpy as jnp
from jax import lax
from jax.experimental import pallas as pl
from jax.experimental.pallas import tpu as pltpu

F32 = jnp.float32
BF16 = jnp.bfloat16
MESH = pl.DeviceIdType.MESH
ANY = pl.BlockSpec(memory_space=pl.ANY)

NORM_EPS = 1e-6
POOL_WINDOWS = (2, 4, 8, 16)
CONV_HEAD_DIM = 128
N_MOD = 6
N_CHIPS = 4
N_DEV = 8

ADAM_LR = 0.001
ADAM_B1 = 0.9
ADAM_B2 = 0.999
ADAM_EPS = 1e-08
ADAM_WD = 0.01
ADAM_STEP = 10

VMEM_LIMIT_BYTES = 56 * 1024 * 1024

NN = (((1,), (0,)), ((), ()))
NT = (((1,), (1,)), ((), ()))
TN = (((0,), (0,)), ((), ()))


def _tile(n, pref):
    t = min(n, pref)
    while n % t:
        t //= 2
    return t


def _params(*sem):
    return pltpu.CompilerParams(dimension_semantics=sem, vmem_limit_bytes=VMEM_LIMIT_BYTES)


def _position():
    return lax.axis_index("x"), lax.axis_index("y"), lax.axis_index("c")


def _flip(ix, iy, ic, mask):
    return (1 - ix if mask & 4 else ix, 1 - iy if mask & 2 else iy, 1 - ic if mask & 1 else ic)


def _allgather8(name, blk):
    rows, cols = blk.shape

    def body(x_ref, out_ref, send_sems, recv_sems, local_sem):
        ix, iy, ic = _position()
        me = 4 * ix + 2 * iy + ic
        mine = pltpu.make_async_copy(x_ref, out_ref.at[me], local_sem)
        mine.start()
        sends = []
        for mask in range(1, N_DEV):
            cp = pltpu.make_async_remote_copy(
                src_ref=x_ref, dst_ref=out_ref.at[me],
                send_sem=send_sems.at[mask - 1], recv_sem=recv_sems.at[mask - 1],
                device_id=_flip(ix, iy, ic, mask), device_id_type=MESH)
            cp.start()
            sends.append(cp)
        for mask in range(1, N_DEV):
            px, py, pc = _flip(ix, iy, ic, mask)
            pltpu.make_async_remote_copy(
                src_ref=x_ref, dst_ref=out_ref.at[4 * px + 2 * py + pc],
                send_sem=send_sems.at[mask - 1], recv_sem=recv_sems.at[mask - 1],
                device_id=(px, py, pc), device_id_type=MESH).wait_recv()
        for cp in sends:
            cp.wait_send()
        mine.wait()

    return pl.pallas_call(
        body, name=name,
        out_shape=jax.ShapeDtypeStruct((N_DEV, rows, cols), F32),
        in_specs=[pl.BlockSpec(memory_space=pltpu.VMEM)],
        out_specs=pl.BlockSpec(memory_space=pltpu.VMEM),
        scratch_shapes=[pltpu.SemaphoreType.DMA((N_DEV - 1,)), pltpu.SemaphoreType.DMA((N_DEV - 1,)),
                        pltpu.SemaphoreType.DMA],
    )(blk)


def _gather_flat(name, vec):
    n = vec.shape[0]
    npad = -(-n // 1024) * 1024
    blk = jnp.pad(vec, (0, npad - n)).reshape(8, npad // 8)
    return _allgather8(name, blk).reshape(N_DEV, npad)[:, :n]


def _chip_relations(ix, iy):
    return [(1 - ix, iy), (ix, 1 - iy), (1 - ix, 1 - iy)]


def _gather_weights(shards):
    n = len(shards)

    def body(*refs):
        src, out = refs[:n], refs[n:2 * n]
        send_sems, recv_sems, local_sems = refs[2 * n:]
        ix, iy, ic = _position()
        chip = 2 * ix + iy
        sibling = (ix, iy, 1 - ic)
        rels = _chip_relations(ix, iy)

        def half_rows(a, h):
            half = shards[a].shape[0] // 2
            return pl.ds(h * half, half)

        local = []
        for a in range(n):
            cp = pltpu.make_async_copy(src[a], out[a].at[chip], local_sems.at[a])
            cp.start()
            local.append(cp)
        started = []
        for a in range(n):
            for r, (px, py) in enumerate(rels):
                cp = pltpu.make_async_remote_copy(
                    src_ref=src[a].at[half_rows(a, ic)], dst_ref=out[a].at[chip, half_rows(a, ic)],
                    send_sem=send_sems.at[6 * a + r], recv_sem=recv_sems.at[6 * a + r],
                    device_id=(px, py, ic), device_id_type=MESH)
                cp.start()
                started.append(cp)
        for a in range(n):
            for r, (px, py) in enumerate(rels):
                landed = out[a].at[2 * px + py, half_rows(a, ic)]
                pltpu.make_async_remote_copy(
                    src_ref=landed, dst_ref=landed,
                    send_sem=send_sems.at[6 * a + r], recv_sem=recv_sems.at[6 * a + r],
                    device_id=(px, py, ic), device_id_type=MESH).wait_recv()
                cp = pltpu.make_async_remote_copy(
                    src_ref=landed, dst_ref=landed,
                    send_sem=send_sems.at[6 * a + 3 + r], recv_sem=recv_sems.at[6 * a + 3 + r],
                    device_id=sibling, device_id_type=MESH)
                cp.start()
                started.append(cp)
        for a in range(n):
            for r, (px, py) in enumerate(rels):
                passed = out[a].at[2 * px + py, half_rows(a, 1 - ic)]
                pltpu.make_async_remote_copy(
                    src_ref=passed, dst_ref=passed,
                    send_sem=send_sems.at[6 * a + 3 + r], recv_sem=recv_sems.at[6 * a + 3 + r],
                    device_id=sibling, device_id_type=MESH).wait_recv()
        for cp in started:
            cp.wait_send()
        for cp in local:
            cp.wait()

    return pl.pallas_call(
        body, name="gather_weights",
        out_shape=[jax.ShapeDtypeStruct((N_CHIPS,) + s.shape, s.dtype) for s in shards],
        in_specs=[ANY] * n, out_specs=[ANY] * n,
        scratch_shapes=[pltpu.SemaphoreType.DMA((6 * n,)), pltpu.SemaphoreType.DMA((6 * n,)),
                        pltpu.SemaphoreType.DMA((n,))],
    )(*shards)


def _exchange_halves(grads):
    n = len(grads)

    def body(*refs):
        src, out = refs[:n], refs[n:2 * n]
        send_sems, recv_sems = refs[2 * n:]
        ix, iy, ic = _position()
        copies = []
        for a in range(n):
            half = grads[a].shape[1] // 2
            cp = pltpu.make_async_remote_copy(
                src_ref=src[a].at[pl.ds(0, N_CHIPS), pl.ds((1 - ic) * half, half)], dst_ref=out[a],
                send_sem=send_sems.at[a], recv_sem=recv_sems.at[a],
                device_id=(ix, iy, 1 - ic), device_id_type=MESH)
            cp.start()
            copies.append(cp)
        for cp in copies:
            cp.wait()

    return pl.pallas_call(
        body, name="exchange_halves",
        out_shape=[jax.ShapeDtypeStruct((N_CHIPS, g.shape[1] // 2, g.shape[2]), g.dtype) for g in grads],
        in_specs=[ANY] * n, out_specs=[ANY] * n,
        scratch_shapes=[pltpu.SemaphoreType.DMA((n,)), pltpu.SemaphoreType.DMA((n,))],
    )(*grads)


def _scatter_partials(parts):
    n = len(parts)

    def body(*refs):
        src, out = refs[:n], refs[n:2 * n]
        send_sems, recv_sems = refs[2 * n:]
        ix, iy, ic = _position()
        copies = []
        for a in range(n):
            for r, (px, py) in enumerate(_chip_relations(ix, iy)):
                cp = pltpu.make_async_remote_copy(
                    src_ref=src[a].at[2 * px + py], dst_ref=out[a].at[r],
                    send_sem=send_sems.at[3 * a + r], recv_sem=recv_sems.at[3 * a + r],
                    device_id=(px, py, ic), device_id_type=MESH)
                cp.start()
                copies.append(cp)
        for cp in copies:
            cp.wait()

    return pl.pallas_call(
        body, name="scatter_partials",
        out_shape=[jax.ShapeDtypeStruct((3,) + p.shape[1:], p.dtype) for p in parts],
        in_specs=[ANY] * n, out_specs=[ANY] * n,
        scratch_shapes=[pltpu.SemaphoreType.DMA((3 * n,)), pltpu.SemaphoreType.DMA((3 * n,))],
    )(*parts)


def _share_reduced(fulls):
    n = len(fulls)

    def body(*refs):
        src, out = refs[:n], refs[n:2 * n]
        send_sems, recv_sems = refs[2 * n:]
        ix, iy, ic = _position()
        copies = []
        for a in range(n):
            half = fulls[a].shape[0] // 2
            mine = pl.ds(ic * half, half)
            cp = pltpu.make_async_remote_copy(
                src_ref=src[a].at[mine], dst_ref=out[a].at[mine],
                send_sem=send_sems.at[a], recv_sem=recv_sems.at[a],
                device_id=(ix, iy, 1 - ic), device_id_type=MESH)
            cp.start()
            copies.append(cp)
        for cp in copies:
            cp.wait()

    return pl.pallas_call(
        body, name="share_reduced",
        out_shape=[jax.ShapeDtypeStruct(f.shape, f.dtype) for f in fulls],
        in_specs=[ANY] * n, out_specs=[ANY] * n,
        input_output_aliases={a: a for a in range(n)},
        scratch_shapes=[pltpu.SemaphoreType.DMA((n,)), pltpu.SemaphoreType.DMA((n,))],
    )(*fulls)


def _add_half(name, grad, recv, core):
    _, rows, cols = grad.shape
    half = rows // 2
    tr, tc = _tile(half, 512), _tile(cols, 2048)
    nbr = half // tr

    def body(core_ref, g_ref, r_ref, o_ref):
        o_ref[...] = (g_ref[...].astype(F32) + r_ref[...].astype(F32)).astype(BF16)

    return pl.pallas_call(
        body, name=name,
        grid_spec=pltpu.PrefetchScalarGridSpec(
            num_scalar_prefetch=1, grid=(N_CHIPS, nbr, cols // tc),
            in_specs=[pl.BlockSpec((None, tr, tc), lambda s, i, j, core_ref: (s, core_ref[0] * nbr + i, j)),
                      pl.BlockSpec((None, tr, tc), lambda s, i, j, core_ref: (s, i, j))],
            out_specs=pl.BlockSpec((None, tr, tc), lambda s, i, j, core_ref: (s, i, j))),
        out_shape=jax.ShapeDtypeStruct((N_CHIPS, half, cols), BF16),
        compiler_params=_params("parallel", "parallel", "parallel"),
    )(core, grad, recv)


def _reduce_chips(name, part, recv, chip_core):
    _, half, cols = part.shape
    tr, tc = _tile(half, 512), _tile(cols, 2048)
    nbr = half // tr

    def body(pos_ref, p_ref, r_ref, o_ref):
        acc = p_ref[...].astype(F32)
        for r in range(3):
            acc = acc + r_ref[r].astype(F32)
        o_ref[...] = acc

    return pl.pallas_call(
        body, name=name,
        grid_spec=pltpu.PrefetchScalarGridSpec(
            num_scalar_prefetch=1, grid=(nbr, cols // tc),
            in_specs=[pl.BlockSpec((None, tr, tc), lambda i, j, pos_ref: (pos_ref[0], i, j)),
                      pl.BlockSpec((3, tr, tc), lambda i, j, pos_ref: (0, i, j))],
            out_specs=pl.BlockSpec((tr, tc), lambda i, j, pos_ref: (pos_ref[1] * nbr + i, j))),
        out_shape=jax.ShapeDtypeStruct((2 * half, cols), F32),
        compiler_params=_params("parallel", "parallel"),
    )(chip_core, part, recv)


def _adamw(name, g, w, m, v):
    rows, cols = g.shape
    tr, tc = _tile(rows, 256), _tile(cols, 2048)
    bc1 = 1.0 - ADAM_B1 ** ADAM_STEP
    bc2 = 1.0 - ADAM_B2 ** ADAM_STEP

    def body(g_ref, w_ref, m_ref, v_ref, d_ref, mo_ref, vo_ref):
        gv = g_ref[...]
        mn = ADAM_B1 * m_ref[...] + (1.0 - ADAM_B1) * gv
        vn = ADAM_B2 * v_ref[...] + (1.0 - ADAM_B2) * (gv * gv)
        d_ref[...] = -ADAM_LR * ((mn / bc1) / (jnp.sqrt(vn / bc2) + ADAM_EPS) + ADAM_WD * w_ref[...])
        mo_ref[...] = mn
        vo_ref[...] = vn

    spec = pl.BlockSpec((tr, tc), lambda i, j: (i, j))
    shape = jax.ShapeDtypeStruct((rows, cols), F32)
    return pl.pallas_call(
        body, name=name, grid=(rows // tr, cols // tc),
        in_specs=[spec] * 4, out_specs=[spec] * 3, out_shape=[shape] * 3,
        compiler_params=_params("parallel", "parallel"),
    )(g, w, m, v)


def _reduce_small(gathered, d_model):
    n = gathered.shape[1]
    loss_at = (N_MOD + 3) * d_model

    def body(g_ref, s_ref, loss_ref):
        acc = g_ref[0:1, :]
        for d in range(1, N_DEV):
            acc = acc + g_ref[d:d + 1, :]
        s_ref[...] = acc
        lanes = acc[:, loss_at:loss_at + d_model]
        loss_ref[...] = jnp.broadcast_to((0.5 / d_model) * jnp.sum(lanes, axis=1, keepdims=True), loss_ref.shape)

    return pl.pallas_call(
        body, name="reduce_small",
        out_shape=[jax.ShapeDtypeStruct((1, n), F32), jax.ShapeDtypeStruct((1, 128), F32)],
        compiler_params=pltpu.CompilerParams(vmem_limit_bytes=VMEM_LIMIT_BYTES),
    )(gathered)


def _ada_forward(c_all, w_ada, b_cols):
    d_model, width = w_ada.shape
    tn = _tile(width, 512)

    def body(c_ref, w_ref, b_ref, o_ref):
        cv = c_ref[...]
        act = cv * jax.nn.sigmoid(cv)
        o_ref[...] = lax.dot_general(act, w_ref[...], NN, precision=lax.Precision.HIGHEST,
                                     preferred_element_type=F32) + b_ref[...]

    return pl.pallas_call(
        body, name="ada_forward", grid=(width // tn,),
        in_specs=[pl.BlockSpec((N_DEV, d_model), lambda j: (0, 0)),
                  pl.BlockSpec((d_model, tn), lambda j: (0, j)),
                  pl.BlockSpec((1, tn), lambda j: (0, j))],
        out_specs=pl.BlockSpec((N_DEV, tn), lambda j: (0, j)),
        out_shape=jax.ShapeDtypeStruct((N_DEV, width), F32),
        compiler_params=_params("parallel"),
    )(c_all, w_ada, b_cols)


def _ada_backward(c_all_t, dmod_cols, w, m, v):
    d_model, width = w.shape
    tr, tc = _tile(d_model, 256), _tile(width, 1536)
    bc1 = 1.0 - ADAM_B1 ** ADAM_STEP
    bc2 = 1.0 - ADAM_B2 ** ADAM_STEP

    def body(c_ref, dm_ref, w_ref, m_ref, v_ref, g_ref, d_ref, mo_ref, vo_ref):
        cv = c_ref[...]
        act = cv * jax.nn.sigmoid(cv)
        gv = lax.dot_general(act, dm_ref[...], NN, precision=lax.Precision.HIGHEST, preferred_element_type=F32)
        mn = ADAM_B1 * m_ref[...] + (1.0 - ADAM_B1) * gv
        vn = ADAM_B2 * v_ref[...] + (1.0 - ADAM_B2) * (gv * gv)
        g_ref[...] = gv
        d_ref[...] = -ADAM_LR * ((mn / bc1) / (jnp.sqrt(vn / bc2) + ADAM_EPS) + ADAM_WD * w_ref[...])
        mo_ref[...] = mn
        vo_ref[...] = vn

    spec = pl.BlockSpec((tr, tc), lambda i, j: (i, j))
    shape = jax.ShapeDtypeStruct((d_model, width), F32)
    return pl.pallas_call(
        body, name="ada_backward", grid=(d_model // tr, width // tc),
        in_specs=[pl.BlockSpec((tr, N_DEV), lambda i, j: (i, 0)),
                  pl.BlockSpec((N_DEV, tc), lambda i, j: (0, j)), spec, spec, spec],
        out_specs=[spec] * 4, out_shape=[shape] * 4,
        compiler_params=_params("parallel", "parallel"),
    )(c_all_t, dmod_cols, w, m, v)


def _matmul(name, a, b, extras, *, grid, tiles, dims, a_spec, b_spec, extra_specs, out_shape, out_specs,
            epilogue, prologue=None):
    tm, tn, _ = tiles
    gm, gn, gk = grid
    n_extra, n_out = len(extras), len(out_shape)

    def body(*refs):
        a_ref, b_ref = refs[0], refs[1]
        extra_refs = refs[2:2 + n_extra]
        out_refs = refs[2 + n_extra:2 + n_extra + n_out]
        acc_ref = refs[-1]
        k = pl.program_id(2)

        @pl.when(k == 0)
        def _():
            acc_ref[...] = jnp.zeros_like(acc_ref)

        av = a_ref[...]
        if prologue is not None:
            av = prologue(av)
        acc_ref[...] += lax.dot_general(av, b_ref[...], dims, preferred_element_type=F32)

        @pl.when(k == gk - 1)
        def _():
            epilogue(acc_ref[...], extra_refs, out_refs)

    return pl.pallas_call(
        body, name=name, grid=(gm, gn, gk),
        in_specs=[a_spec, b_spec, *extra_specs], out_specs=out_specs, out_shape=out_shape,
        scratch_shapes=[pltpu.VMEM((tm, tn), F32)],
        compiler_params=_params("parallel", "parallel", "arbitrary"),
    )(a, b, *extras)


def _store(dtype):
    def epilogue(acc, extra_refs, out_refs):
        out_refs[0][...] = acc.astype(dtype)
    return epilogue


def _residual_epilogue(acc, extra_refs, out_refs):
    res_ref, gate_ref = extra_refs
    out_refs[0][...] = res_ref[...] + gate_ref[...] * acc
    out_refs[1][...] = acc.astype(BF16)


def _square(av):
    af = av.astype(F32)
    return (af * af).astype(BF16)


MM_TILE_M = 1024
MM_TILE_N = 1024
MM_TILE_K = 1024


def _in_projection(h, w_slabs):
    seq, d_model = h.shape
    _, _, cols = w_slabs.shape
    tm, tn, tk = _tile(seq, MM_TILE_M), _tile(cols, MM_TILE_N), _tile(d_model, MM_TILE_K)
    nbj = cols // tn
    return _matmul(
        "in_projection", h, w_slabs, (), grid=(seq // tm, N_CHIPS * nbj, d_model // tk), tiles=(tm, tn, tk), dims=NN,
        a_spec=pl.BlockSpec((tm, tk), lambda i, j, k: (i, k)),
        b_spec=pl.BlockSpec((None, tk, tn), lambda i, j, k: (j // nbj, k, j % nbj)),
        extra_specs=(),
        out_shape=[jax.ShapeDtypeStruct((N_CHIPS, seq, cols), BF16)],
        out_specs=[pl.BlockSpec((None, tm, tn), lambda i, j, k: (j // nbj, i, j % nbj))],
        epilogue=_store(BF16))[0]


def _residual_projection(name, a, w, res, gate, prologue=None):
    seq, kdim = a.shape
    d_model = w.shape[1]
    tm, tn, tk = _tile(seq, MM_TILE_M), _tile(d_model, MM_TILE_N), _tile(kdim, MM_TILE_K)
    tile = pl.BlockSpec((tm, tn), lambda i, j, k: (i, j))
    return _matmul(
        name, a, w, (res, gate), grid=(seq // tm, d_model // tn, kdim // tk), tiles=(tm, tn, tk), dims=NN,
        a_spec=pl.BlockSpec((tm, tk), lambda i, j, k: (i, k)),
        b_spec=pl.BlockSpec((tk, tn), lambda i, j, k: (k, j)),
        extra_specs=(tile, pl.BlockSpec((1, tn), lambda i, j, k: (0, j))),
        out_shape=[jax.ShapeDtypeStruct((seq, d_model), F32), jax.ShapeDtypeStruct((seq, d_model), BF16)],
        out_specs=[tile, tile],
        epilogue=_residual_epilogue, prologue=prologue)


def _mlp_in(h, w_slabs):
    seq, d_model = h.shape
    _, _, cols = w_slabs.shape
    tm, tn, tk = _tile(seq, MM_TILE_M), _tile(cols, MM_TILE_N), _tile(d_model, MM_TILE_K)
    nbj = cols // tn

    def epilogue(acc, extra_refs, out_refs):
        out_refs[0][...] = jnp.maximum(acc, 0.0).astype(BF16)

    return _matmul(
        "mlp_in", h, w_slabs, (), grid=(seq // tm, N_CHIPS * nbj, d_model // tk), tiles=(tm, tn, tk), dims=NN,
        a_spec=pl.BlockSpec((tm, tk), lambda i, j, k: (i, k)),
        b_spec=pl.BlockSpec((None, tk, tn), lambda i, j, k: (j // nbj, k, j % nbj)),
        extra_specs=(),
        out_shape=[jax.ShapeDtypeStruct((seq, N_CHIPS * cols), BF16)],
        out_specs=[pl.BlockSpec((tm, tn), lambda i, j, k: (i, j))],
        epilogue=epilogue)[0]


def _grad_hidden(dmlp, w2, act):
    seq, d_model = dmlp.shape
    ff = w2.shape[0]
    tm, tn, tk = _tile(seq, MM_TILE_M), _tile(ff, MM_TILE_N), _tile(d_model, MM_TILE_K)

    def epilogue(acc, extra_refs, out_refs):
        out_refs[0][...] = (acc * (2.0 * extra_refs[0][...].astype(F32))).astype(BF16)

    tile = pl.BlockSpec((tm, tn), lambda i, j, k: (i, j))
    return _matmul(
        "grad_hidden", dmlp, w2, (act,), grid=(seq // tm, ff // tn, d_model // tk), tiles=(tm, tn, tk), dims=NT,
        a_spec=pl.BlockSpec((tm, tk), lambda i, j, k: (i, k)),
        b_spec=pl.BlockSpec((tn, tk), lambda i, j, k: (j, k)),
        extra_specs=(tile,),
        out_shape=[jax.ShapeDtypeStruct((seq, ff), BF16)], out_specs=[tile],
        epilogue=epilogue)[0]


def _weight_grad(name, a, b, prologue=None):
    seq, m = a.shape
    n = b.shape[1]
    tm, tn, tk = _tile(m, MM_TILE_M), _tile(n, MM_TILE_N), _tile(seq, MM_TILE_K)
    return _matmul(
        name, a, b, (), grid=(m // tm, n // tn, seq // tk), tiles=(tm, tn, tk), dims=TN,
        a_spec=pl.BlockSpec((tk, tm), lambda i, j, k: (k, i)),
        b_spec=pl.BlockSpec((tk, tn), lambda i, j, k: (k, j)),
        extra_specs=(),
        out_shape=[jax.ShapeDtypeStruct((m, n), BF16)],
        out_specs=[pl.BlockSpec((tm, tn), lambda i, j, k: (i, j))],
        epilogue=_store(BF16), prologue=prologue)[0]


def _weight_grad_slabs(name, a, b, slab_cols):
    seq, m = a.shape
    cols = b.shape[2] if slab_cols is None else slab_cols
    tm, tn, tk = _tile(m, MM_TILE_M), _tile(cols, MM_TILE_N), _tile(seq, MM_TILE_K)
    nbj = cols // tn
    if slab_cols is None:
        b_spec = pl.BlockSpec((None, tk, tn), lambda i, j, k: (j // nbj, k, j % nbj))
    else:
        b_spec = pl.BlockSpec((tk, tn), lambda i, j, k: (k, j))
    return _matmul(
        name, a, b, (), grid=(m // tm, N_CHIPS * nbj, seq // tk), tiles=(tm, tn, tk), dims=TN,
        a_spec=pl.BlockSpec((tk, tm), lambda i, j, k: (k, i)),
        b_spec=b_spec, extra_specs=(),
        out_shape=[jax.ShapeDtypeStruct((N_CHIPS, m, cols), BF16)],
        out_specs=[pl.BlockSpec((None, tm, tn), lambda i, j, k: (j // nbj, i, j % nbj))],
        epilogue=_store(BF16))[0]


def _grad_input_slabs(name, dy, w_slabs):
    _, d_model, cols = w_slabs.shape
    seq = dy.shape[1] if dy.ndim == 3 else dy.shape[0]
    tm, tn, tk = _tile(seq, MM_TILE_M), _tile(d_model, MM_TILE_N), _tile(cols, MM_TILE_K)
    nbk = cols // tk
    if dy.ndim == 3:
        a_spec = pl.BlockSpec((None, tm, tk), lambda i, j, k: (k // nbk, i, k % nbk))
    else:
        a_spec = pl.BlockSpec((tm, tk), lambda i, j, k: (i, k))
    return _matmul(
        name, dy, w_slabs, (), grid=(seq // tm, d_model // tn, N_CHIPS * nbk), tiles=(tm, tn, tk), dims=NT,
        a_spec=a_spec,
        b_spec=pl.BlockSpec((None, tn, tk), lambda i, j, k: (k // nbk, j, k % nbk)),
        extra_specs=(),
        out_shape=[jax.ShapeDtypeStruct((seq, d_model), F32)],
        out_specs=[pl.BlockSpec((tm, tn), lambda i, j, k: (i, j))],
        epilogue=_store(F32))[0]


def _grad_input(name, dy, w):
    seq, n = dy.shape
    kdim = w.shape[0]
    tm, tn, tk = _tile(seq, MM_TILE_M), _tile(kdim, MM_TILE_N), _tile(n, MM_TILE_K)
    return _matmul(
        name, dy, w, (), grid=(seq // tm, kdim // tn, n // tk), tiles=(tm, tn, tk), dims=NT,
        a_spec=pl.BlockSpec((tm, tk), lambda i, j, k: (i, k)),
        b_spec=pl.BlockSpec((tn, tk), lambda i, j, k: (j, k)),
        extra_specs=(),
        out_shape=[jax.ShapeDtypeStruct((seq, kdim), F32)],
        out_specs=[pl.BlockSpec((tm, tn), lambda i, j, k: (i, j))],
        epilogue=_store(F32))[0]


ROW_TILE = 128


def _norm_modulate(name, xin, g, scale, shift):
    seq, d_model = xin.shape
    tr = _tile(seq, ROW_TILE)

    def body(x_ref, g_ref, sc_ref, sh_ref, h_ref):
        xv = x_ref[...]
        r = lax.rsqrt(jnp.mean(xv * xv, axis=-1, keepdims=True) + NORM_EPS)
        h_ref[...] = (((xv * r) * g_ref[...]) * (1.0 + sc_ref[...]) + sh_ref[...]).astype(BF16)

    row = pl.BlockSpec((tr, d_model), lambda i: (i, 0))
    vec = pl.BlockSpec((1, d_model), lambda i: (0, 0))
    return pl.pallas_call(
        body, name=name, grid=(seq // tr,),
        in_specs=[row, vec, vec, vec], out_specs=row,
        out_shape=jax.ShapeDtypeStruct((seq, d_model), BF16),
        compiler_params=_params("parallel"),
    )(xin, g, scale, shift)


def _loss_head(x2, target, final_g, mlp, gate2):
    seq, d_model = x2.shape
    tr = _tile(seq, ROW_TILE)

    def body(x_ref, t_ref, fg_ref, mlp_ref, gate_ref, dx_ref, dmlp_ref, gfg_ref, dgate_ref, sq_ref):
        @pl.when(pl.program_id(0) == 0)
        def _():
            gfg_ref[...] = jnp.zeros_like(gfg_ref)
            dgate_ref[...] = jnp.zeros_like(dgate_ref)
            sq_ref[...] = jnp.zeros_like(sq_ref)

        xv = x_ref[...]
        fg = fg_ref[...]
        r = lax.rsqrt(jnp.mean(xv * xv, axis=-1, keepdims=True) + NORM_EPS)
        n = xv * r
        err = n * fg - t_ref[...]
        sq_ref[...] += jnp.sum(err * err, axis=0, keepdims=True)
        dy = err * (1.0 / d_model)
        gfg_ref[...] += jnp.sum(dy * n, axis=0, keepdims=True)
        dn = dy * fg
        dx = r * (dn - n * jnp.mean(dn * n, axis=-1, keepdims=True))
        dx_ref[...] = dx
        dgate_ref[...] += jnp.sum(dx * mlp_ref[...].astype(F32), axis=0, keepdims=True)
        dmlp_ref[...] = (dx * gate_ref[...]).astype(BF16)

    row = pl.BlockSpec((tr, d_model), lambda i: (i, 0))
    vec = pl.BlockSpec((1, d_model), lambda i: (0, 0))
    vshape = jax.ShapeDtypeStruct((1, d_model), F32)
    return pl.pallas_call(
        body, name="loss_head", grid=(seq // tr,),
        in_specs=[row, row, vec, row, vec], out_specs=[row, row, vec, vec, vec],
        out_shape=[jax.ShapeDtypeStruct((seq, d_model), F32), jax.ShapeDtypeStruct((seq, d_model), BF16),
                   vshape, vshape, vshape],
        compiler_params=_params("arbitrary"),
    )(x2, target, final_g, mlp, gate2)


def _norm_modulate_backward(name, dh, xin, g, scale, dres, branch=None, gate=None):
    seq, d_model = xin.shape
    tr = _tile(seq, ROW_TILE)
    with_branch = branch is not None
    n_in = 7 if with_branch else 5

    def body(*refs):
        dh_ref, x_ref, g_ref, sc_ref, dres_ref = refs[:5]
        outs = refs[n_in:]
        dx_ref, dsc_ref, dsh_ref, dg_ref = outs[:4]

        @pl.when(pl.program_id(0) == 0)
        def _():
            for ref in outs[1:5] if with_branch else outs[1:4]:
                ref[...] = jnp.zeros_like(ref)

        xv = x_ref[...]
        gv = g_ref[...]
        dhv = dh_ref[...]
        r = lax.rsqrt(jnp.mean(xv * xv, axis=-1, keepdims=True) + NORM_EPS)
        xn = xv * r
        dsh_ref[...] += jnp.sum(dhv, axis=0, keepdims=True)
        dsc_ref[...] += jnp.sum(dhv * (xn * gv), axis=0, keepdims=True)
        t = dhv * (1.0 + sc_ref[...])
        dg_ref[...] += jnp.sum(t * xn, axis=0, keepdims=True)
        dxn = t * gv
        dx = dres_ref[...] + r * (dxn - xn * jnp.mean(dxn * xn, axis=-1, keepdims=True))
        dx_ref[...] = dx
        if with_branch:
            br_ref, gate_ref = refs[5:7]
            dgate_ref, dbr_ref = outs[4:6]
            dgate_ref[...] += jnp.sum(dx * br_ref[...].astype(F32), axis=0, keepdims=True)
            dbr_ref[...] = (dx * gate_ref[...]).astype(BF16)

    row = pl.BlockSpec((tr, d_model), lambda i: (i, 0))
    vec = pl.BlockSpec((1, d_model), lambda i: (0, 0))
    vshape = jax.ShapeDtypeStruct((1, d_model), F32)
    in_specs = [row, row, vec, vec, row]
    out_specs = [row, vec, vec, vec]
    out_shape = [jax.ShapeDtypeStruct((seq, d_model), F32), vshape, vshape, vshape]
    args = [dh, xin, g, scale, dres]
    if with_branch:
        in_specs += [row, vec]
        out_specs += [vec, row]
        out_shape += [vshape, jax.ShapeDtypeStruct((seq, d_model), BF16)]
        args += [branch, gate]
    return pl.pallas_call(
        body, name=name, grid=(seq // tr,),
        in_specs=in_specs, out_specs=out_specs, out_shape=out_shape,
        compiler_params=_params("arbitrary"),
    )(*args)


def _shifted(v, k, t):
    seq = v.shape[0]
    if k == 0:
        return v
    moved = pltpu.roll(v, (-k) % seq, 0)
    return jnp.where((t + k >= 0) & (t + k < seq), moved, 0.0)


def _window_sum(v, offsets, t):
    acc = None
    for k in offsets:
        term = _shifted(v, k, t)
        acc = term if acc is None else acc + term
    return acc


def _window_count(seq, half):
    t = lax.broadcasted_iota(jnp.int32, (seq, 1), 0)
    return (jnp.minimum(t + half, seq) - jnp.maximum(t - half, 0)).astype(F32)


def _pool_forward(proj, group_dim):
    _, seq, cols = proj.shape
    tl = _tile(group_dim, 256)
    nbl = group_dim // tl
    n_groups = cols // group_dim

    def body(v_ref, o_ref):
        g = pl.program_id(0)
        for gi, window in enumerate(POOL_WINDOWS[:n_groups]):
            @pl.when(g == gi)
            def _(window=window):
                half = window // 2
                v = v_ref[...].astype(F32)
                t = lax.broadcasted_iota(jnp.int32, v.shape, 0)
                total = _window_sum(v, range(-half, half), t)
                o_ref[...] = (total / _window_count(seq, half) - v).astype(BF16)

    return pl.pallas_call(
        body, name="pool_forward", grid=(n_groups, nbl),
        in_specs=[pl.BlockSpec((None, seq, tl), lambda g, j: (0, 0, g * nbl + j))],
        out_specs=pl.BlockSpec((seq, tl), lambda g, j: (0, g * nbl + j)),
        out_shape=jax.ShapeDtypeStruct((seq, cols), BF16),
        compiler_params=_params("parallel", "parallel"),
    )(proj)


def _group_matrix(w_ref):
    return jnp.concatenate([w_ref[r] for r in range(N_CHIPS)], axis=0)


def _pool_mix_forward(pooled, w_pm, pool_scale, gnorm_g, d_model):
    seq, cols = pooled.shape
    _, n_groups, shard_rows, group_dim = w_pm.shape
    tm = _tile(seq, 512)

    def body(p_ref, w_ref, ps_ref, g_ref, o_ref):
        a = jnp.dot(p_ref[...], _group_matrix(w_ref), preferred_element_type=F32) * ps_ref[...]
        ra = lax.rsqrt(jnp.mean(a * a, axis=-1, keepdims=True) + NORM_EPS)
        o_ref[...] = ((a * ra) * g_ref[...]).astype(BF16)

    tile = pl.BlockSpec((tm, group_dim), lambda g, i: (i, g))
    vec = pl.BlockSpec((1, group_dim), lambda g, i: (0, g))
    return pl.pallas_call(
        body, name="pool_mix_forward", grid=(n_groups, seq // tm),
        in_specs=[tile, pl.BlockSpec((N_CHIPS, None, shard_rows, group_dim), lambda g, i: (0, g, 0, 0)), vec, vec],
        out_specs=tile,
        out_shape=jax.ShapeDtypeStruct((seq, d_model), BF16),
        compiler_params=_params("parallel", "parallel"),
    )(pooled, w_pm, pool_scale, gnorm_g)


def _conv_parts(b_ref, c_ref, u_ref, w_ref, bias_ref):
    bv = b_ref[...].astype(F32)
    cu = c_ref[...].astype(F32) * u_ref[...].astype(F32)
    t = lax.broadcasted_iota(jnp.int32, cu.shape, 0)
    prev, nxt = _shifted(cu, -1, t), _shifted(cu, 1, t)
    w = w_ref[...]
    conv = w[0:1] * prev + w[1:2] * cu + w[2:3] * nxt + bias_ref[...]
    return bv, cu, prev, nxt, conv, w, t


def _conv_forward(proj, conv_w, conv_b, gnorm_g, mixed):
    _, seq, cols = proj.shape
    tl = CONV_HEAD_DIM
    first = cols // tl

    def body(b_ref, c_ref, u_ref, w_ref, bias_ref, g_ref, mixed_ref, o_ref):
        bv, _, _, _, conv, _, _ = _conv_parts(b_ref, c_ref, u_ref, w_ref, bias_ref)
        bo = bv * conv
        rb = lax.rsqrt(jnp.mean(bo * bo, axis=-1, keepdims=True) + NORM_EPS)
        o_ref[...] = ((bo * rb) * g_ref[...]).astype(BF16)

    def slab(s):
        return pl.BlockSpec((None, seq, tl), lambda j, s=s: (s, 0, j))

    vec = pl.BlockSpec((1, tl), lambda j: (0, j))
    return pl.pallas_call(
        body, name="conv_forward", grid=(cols // tl,),
        in_specs=[slab(1), slab(2), slab(3), pl.BlockSpec((3, tl), lambda j: (0, j)), vec, vec, ANY],
        out_specs=pl.BlockSpec((seq, tl), lambda j: (0, first + j)),
        out_shape=jax.ShapeDtypeStruct(mixed.shape, mixed.dtype),
        input_output_aliases={6: 0},
        compiler_params=_params("parallel"),
    )(proj, proj, proj, conv_w, conv_b, gnorm_g, mixed)


def _pool_mix_backward(dmixed, pooled, w_pm, pool_scale, gnorm_g):
    seq, cols = pooled.shape
    _, n_groups, shard_rows, group_dim = w_pm.shape
    tm = _tile(seq, 512)

    def body(dm_ref, p_ref, w_ref, ps_ref, g_ref, dp_ref, dpm_ref, gg_ref, gps_ref):
        @pl.when(pl.program_id(1) == 0)
        def _():
            gg_ref[...] = jnp.zeros_like(gg_ref)
            gps_ref[...] = jnp.zeros_like(gps_ref)

        w = _group_matrix(w_ref)
        ps = ps_ref[...]
        a_pre = jnp.dot(p_ref[...], w, preferred_element_type=F32)
        a = a_pre * ps
        ra = lax.rsqrt(jnp.mean(a * a, axis=-1, keepdims=True) + NORM_EPS)
        an = a * ra
        dm = dm_ref[...]
        gg_ref[...] += jnp.sum(dm * an, axis=0, keepdims=True)
        dan = dm * g_ref[...]
        da = ra * (dan - an * jnp.mean(dan * an, axis=-1, keepdims=True))
        gps_ref[...] += jnp.sum(da * a_pre, axis=0, keepdims=True)
        dpm = (da * ps).astype(BF16)
        dpm_ref[...] = dpm
        dp_ref[...] = lax.dot_general(dpm, w, NT, preferred_element_type=F32)

    tile = pl.BlockSpec((tm, group_dim), lambda g, i: (i, g))
    vec = pl.BlockSpec((1, group_dim), lambda g, i: (0, g))
    vshape = jax.ShapeDtypeStruct((1, cols), F32)
    return pl.pallas_call(
        body, name="pool_mix_backward", grid=(n_groups, seq // tm),
        in_specs=[tile, tile, pl.BlockSpec((N_CHIPS, None, shard_rows, group_dim), lambda g, i: (0, g, 0, 0)),
                  vec, vec],
        out_specs=[tile, tile, vec, vec],
        out_shape=[jax.ShapeDtypeStruct((seq, cols), F32), jax.ShapeDtypeStruct((seq, cols), BF16), vshape, vshape],
        compiler_params=_params("parallel", "arbitrary"),
    )(dmixed, pooled, w_pm, pool_scale, gnorm_g)


def _pool_mix_weight_grad(pooled, dpm, n_groups):
    seq, cols = pooled.shape
    group_dim = cols // n_groups
    shard_rows = group_dim // N_CHIPS
    tk = _tile(seq, 1024)
    gk = seq // tk

    def body(p_ref, d_ref, o_ref, acc_ref):
        k = pl.program_id(1)

        @pl.when(k == 0)
        def _():
            acc_ref[...] = jnp.zeros_like(acc_ref)

        acc_ref[...] += lax.dot_general(p_ref[...], d_ref[...], TN, preferred_element_type=F32)

        @pl.when(k == gk - 1)
        def _():
            for r in range(N_CHIPS):
                o_ref[r] = acc_ref[r * shard_rows:(r + 1) * shard_rows, :].astype(BF16)

    tile = pl.BlockSpec((tk, group_dim), lambda g, k: (k, g))
    return pl.pallas_call(
        body, name="pool_mix_weight_grad", grid=(n_groups, gk),
        in_specs=[tile, tile],
        out_specs=pl.BlockSpec((N_CHIPS, None, shard_rows, group_dim), lambda g, k: (0, g, 0, 0)),
        out_shape=jax.ShapeDtypeStruct((N_CHIPS, n_groups, shard_rows, group_dim), BF16),
        scratch_shapes=[pltpu.VMEM((group_dim, group_dim), F32)],
        compiler_params=_params("parallel", "arbitrary"),
    )(pooled, dpm)


def _mixers_backward(dpooled, dmixed, proj, conv_w, conv_b, gnorm_g, group_dim):
    _, seq, cols = proj.shape
    tl = CONV_HEAD_DIM
    first = cols // tl
    per_group = group_dim // tl
    n_groups = cols // group_dim

    def body(dp_ref, dm_ref, b_ref, c_ref, u_ref, w_ref, bias_ref, g_ref, o_ref, gg_ref, gb_ref, gw_ref):
        j = pl.program_id(0)
        for gi, window in enumerate(POOL_WINDOWS[:n_groups]):
            @pl.when(j // per_group == gi)
            def _(window=window):
                half = window // 2
                dp = dp_ref[...]
                t = lax.broadcasted_iota(jnp.int32, dp.shape, 0)
                dq = dp / _window_count(seq, half)
                o_ref[0] = (_window_sum(dq, range(-half + 1, half + 1), t) - dp).astype(BF16)

        bv, cu, prev, nxt, conv, w, t = _conv_parts(b_ref, c_ref, u_ref, w_ref, bias_ref)
        bo = bv * conv
        rb = lax.rsqrt(jnp.mean(bo * bo, axis=-1, keepdims=True) + NORM_EPS)
        bn = bo * rb
        dm = dm_ref[...]
        gg_ref[...] = jnp.sum(dm * bn, axis=0, keepdims=True)
        dbn = dm * g_ref[...]
        dbo = rb * (dbn - bn * jnp.mean(dbn * bn, axis=-1, keepdims=True))
        o_ref[1] = (dbo * conv).astype(BF16)
        dconv = dbo * bv
        gb_ref[...] = jnp.sum(dconv, axis=0, keepdims=True)
        gw_ref[0:1, :] = jnp.sum(dconv * prev, axis=0, keepdims=True)
        gw_ref[1:2, :] = jnp.sum(dconv * cu, axis=0, keepdims=True)
        gw_ref[2:3, :] = jnp.sum(dconv * nxt, axis=0, keepdims=True)
        dcu = w[0:1] * _shifted(dconv, 1, t) + w[1:2] * dconv + w[2:3] * _shifted(dconv, -1, t)
        o_ref[2] = (dcu * u_ref[...].astype(F32)).astype(BF16)
        o_ref[3] = (dcu * c_ref[...].astype(F32)).astype(BF16)

    def slab(s):
        return pl.BlockSpec((None, seq, tl), lambda j, s=s: (s, 0, j))

    vec = pl.BlockSpec((1, tl), lambda j: (0, j))
    rows3 = pl.BlockSpec((3, tl), lambda j: (0, j))
    vshape = jax.ShapeDtypeStruct((1, cols), F32)
    return pl.pallas_call(
        body, name="mixers_backward", grid=(cols // tl,),
        in_specs=[pl.BlockSpec((seq, tl), lambda j: (0, j)), pl.BlockSpec((seq, tl), lambda j: (0, first + j)),
                  slab(1), slab(2), slab(3), rows3, vec, vec],
        out_specs=[pl.BlockSpec((N_CHIPS, seq, tl), lambda j: (0, 0, j)), vec, vec, rows3],
        out_shape=[jax.ShapeDtypeStruct((N_CHIPS, seq, cols), BF16), vshape, vshape,
                   jax.ShapeDtypeStruct((3, cols), F32)],
        compiler_params=_params("parallel"),
    )(dpooled, dmixed, proj, proj, proj, conv_w, conv_b, gnorm_g)


def _reduce_scatter(grads, core, chip_core):
    received = _exchange_halves(grads)
    parts = [_add_half(f"add_half_{a}", g, r, core) for a, (g, r) in enumerate(zip(grads, received))]
    landed = _scatter_partials(parts)
    fulls = [_reduce_chips(f"reduce_chips_{a}", p, l, chip_core) for a, (p, l) in enumerate(zip(parts, landed))]
    return _share_reduced(fulls)


def kernel(x, c, w_ada, b_ada, norm1_g, w_in, pool_mix_w, pool_scale, conv_w, conv_b, gnorm_pool_g, gnorm_conv_g, w_out, norm2_g, w_mlp_in, w_mlp_out, final_g, loss_target, m_w_ada, m_b_ada, m_norm1_g, m_w_in, m_pool_mix_w, m_pool_scale, m_conv_w, m_conv_b, m_gnorm_pool_g, m_gnorm_conv_g, m_w_out, m_norm2_g, m_w_mlp_in, m_w_mlp_out, m_final_g, v_w_ada, v_b_ada, v_norm1_g, v_w_in, v_pool_mix_w, v_pool_scale, v_conv_w, v_conv_b, v_gnorm_pool_g, v_gnorm_conv_g, v_w_out, v_norm2_g, v_w_mlp_in, v_w_mlp_out, v_final_g):
    seq, d_model = x.shape[1], x.shape[2]
    cols = w_in.shape[2]
    n_groups, group_dim = pool_mix_w.shape[1], pool_mix_w.shape[3]
    shard_rows = pool_mix_w.shape[2]
    ff_cols = w_mlp_in.shape[2]
    ada_cols = w_ada.shape[2]
    conv_shard = conv_w.shape[2]
    assert pool_scale.shape[1] == cols and conv_b.shape[1] == cols and n_groups * group_dim == cols
    assert cols % CONV_HEAD_DIM == 0 and group_dim % CONV_HEAD_DIM == 0 and shard_rows * N_CHIPS == group_dim

    ix, iy, ic = _position()
    chip = 2 * ix + iy
    me = 4 * ix + 2 * iy + ic
    core = jnp.reshape(ic, (1,)).astype(jnp.int32)
    chip_core = jnp.stack([chip, ic]).astype(jnp.int32)

    xs, target = x[0], loss_target[0]
    final_row = final_g.reshape(1, d_model)

    small = _gather_flat("gather_cond", jnp.concatenate([c[0], conv_w[0].reshape(-1)]))
    c_all = small[:, :d_model]
    conv_w_full = jnp.concatenate(
        [small[2 * j, d_model:].reshape(3, conv_shard) for j in range(N_CHIPS)], axis=1)
    b_cols = lax.dynamic_slice_in_dim(b_ada, chip * ada_cols, ada_cols, axis=1)
    mod_part = _ada_forward(c_all, w_ada[0], b_cols)
    mod_all = _gather_flat("gather_mod", mod_part.reshape(-1)).reshape(N_DEV, N_DEV, ada_cols)
    mod = jnp.concatenate(
        [lax.dynamic_slice_in_dim(mod_all[2 * j], me, 1, axis=0) for j in range(N_CHIPS)], axis=1)
    shift1, scale1, gate1, shift2, scale2, gate2 = [mod[:, i * d_model:(i + 1) * d_model] for i in range(N_MOD)]

    shards = [w_in[0].astype(BF16), pool_mix_w[0].reshape(n_groups * shard_rows, group_dim).astype(BF16),
              w_out[0].astype(BF16), w_mlp_in[0].astype(BF16), w_mlp_out[0].astype(BF16)]
    wg_in, wg_pm, wg_out, wg_1, wg_2 = _gather_weights(shards)
    wg_pm = wg_pm.reshape(N_CHIPS, n_groups, shard_rows, group_dim)
    wg_out = wg_out.reshape(d_model, d_model)
    wg_2 = wg_2.reshape(N_CHIPS * ff_cols, d_model)

    h1 = _norm_modulate("norm_modulate_1", xs, norm1_g, scale1, shift1)
    proj = _in_projection(h1, wg_in)
    pooled = _pool_forward(proj, group_dim)
    mixed = _pool_mix_forward(pooled, wg_pm, pool_scale, gnorm_pool_g, d_model)
    mixed = _conv_forward(proj, conv_w_full, conv_b, gnorm_conv_g, mixed)
    x1, attn = _residual_projection("out_projection", mixed, wg_out, xs, gate1)
    h2 = _norm_modulate("norm_modulate_2", x1, norm2_g, scale2, shift2)
    act = _mlp_in(h2, wg_1)
    x2, mlp = _residual_projection("mlp_out", act, wg_2, x1, gate2, prologue=_square)

    dx2, dmlp, g_final, dgate2, sq_err = _loss_head(x2, target, final_row, mlp, gate2)
    dhid = _grad_hidden(dmlp, wg_2, act)
    gw_2 = _weight_grad("grad_w_mlp_out", act, dmlp, prologue=_square)
    gw_1 = _weight_grad_slabs("grad_w_mlp_in", h2, dhid, ff_cols)
    dh2 = _grad_input_slabs("grad_h2", dhid, wg_1)
    dx1, dscale2, dshift2, g_norm2, dgate1, dattn = _norm_modulate_backward(
        "norm_modulate_backward_2", dh2, x1, norm2_g, scale2, dx2, attn, gate1)
    dmixed = _grad_input("grad_mixed", dattn, wg_out)
    gw_out = _weight_grad("grad_w_out", mixed, dattn)
    dpooled, dpm, g_gpool, g_pscale = _pool_mix_backward(dmixed, pooled, wg_pm, pool_scale, gnorm_pool_g)
    gw_pm = _pool_mix_weight_grad(pooled, dpm, n_groups)
    dproj, g_gconv, g_convb, g_convw = _mixers_backward(
        dpooled, dmixed, proj, conv_w_full, conv_b, gnorm_conv_g, group_dim)
    gw_in = _weight_grad_slabs("grad_w_in", h1, dproj, None)
    dh1 = _grad_input_slabs("grad_h1", dproj, wg_in)
    grad_x, dscale1, dshift1, g_norm1 = _norm_modulate_backward(
        "norm_modulate_backward_1", dh1, xs, norm1_g, scale1, dx1)

    big = [gw_in, gw_pm.reshape(N_CHIPS, n_groups * shard_rows, group_dim),
           gw_out.reshape(N_CHIPS, d_model // N_CHIPS, d_model), gw_1,
           gw_2.reshape(N_CHIPS, ff_cols, d_model)]
    g_w_in, g_pm, g_w_out, g_w_1, g_w_2 = _reduce_scatter(big, core, chip_core)

    mine = jnp.concatenate(
        [dshift1, dscale1, dgate1, dshift2, dscale2, dgate2, g_norm1, g_norm2, g_final, sq_err,
         g_pscale, g_convb, g_gpool, g_gconv, g_convw.reshape(1, 3 * cols)], axis=1)
    gathered = _gather_flat("gather_small", mine.reshape(-1))
    sums, loss = _reduce_small(gathered, d_model)
    n_rep = (N_MOD + 3) * d_model
    g_rep = jnp.concatenate([sums[:, :n_rep], sums[:, n_rep + d_model:n_rep + d_model + 4 * cols]], axis=1)
    n_small = g_rep.shape[1]

    def pack(b, n1, n2, fg, ps, cb, gp, gc):
        return jnp.concatenate([b, n1, n2, fg.reshape(1, d_model), ps, cb, gp, gc], axis=1).reshape(8, n_small // 8)

    d_rep, m_rep, v_rep = _adamw(
        "adamw_small", g_rep.reshape(8, n_small // 8),
        pack(b_ada, norm1_g, norm2_g, final_g, pool_scale, conv_b, gnorm_pool_g, gnorm_conv_g),
        pack(m_b_ada, m_norm1_g, m_norm2_g, m_final_g, m_pool_scale, m_conv_b, m_gnorm_pool_g, m_gnorm_conv_g),
        pack(v_b_ada, v_norm1_g, v_norm2_g, v_final_g, v_pool_scale, v_conv_b, v_gnorm_pool_g, v_gnorm_conv_g))

    def unpack(flat):
        flat = flat.reshape(1, n_small)
        sizes = [N_MOD * d_model, d_model, d_model, d_model, cols, cols, cols, cols]
        parts, at = [], 0
        for size in sizes:
            parts.append(flat[:, at:at + size])
            at += size
        parts[3] = parts[3].reshape(d_model)
        return parts

    g_convw_full = sums[:, n_rep + d_model + 4 * cols:].reshape(3, cols)
    g_convw_mine = lax.dynamic_slice_in_dim(g_convw_full, chip * conv_shard, conv_shard, axis=1)
    d_convw, m_convw, v_convw = _adamw("adamw_conv_w", g_convw_mine, conv_w[0], m_conv_w[0], v_conv_w[0])

    dmod_cols = lax.dynamic_slice_in_dim(gathered[:, :N_MOD * d_model], chip * ada_cols, ada_cols, axis=1)
    g_ada, d_ada, mn_ada, vn_ada = _ada_backward(c_all.T, dmod_cols, w_ada[0], m_w_ada[0], v_w_ada[0])

    pm2d = (n_groups * shard_rows, group_dim)
    upd_in = _adamw("adamw_w_in", g_w_in, w_in[0], m_w_in[0], v_w_in[0])
    upd_pm = _adamw("adamw_pool_mix", g_pm, pool_mix_w[0].reshape(pm2d), m_pool_mix_w[0].reshape(pm2d),
                    v_pool_mix_w[0].reshape(pm2d))
    upd_out = _adamw("adamw_w_out", g_w_out, w_out[0], m_w_out[0], v_w_out[0])
    upd_1 = _adamw("adamw_w_mlp_in", g_w_1, w_mlp_in[0], m_w_mlp_in[0], v_w_mlp_in[0])
    upd_2 = _adamw("adamw_w_mlp_out", g_w_2, w_mlp_out[0], m_w_mlp_out[0], v_w_mlp_out[0])

    g_parts, d_parts, m_parts, v_parts = unpack(g_rep), unpack(d_rep), unpack(m_rep), unpack(v_rep)

    def ordered(ada, rep, w_in_, pm, convw, w_out_, w_1, w_2):
        b, n1, n2, fg, ps, cb, gp, gc = rep
        return [ada[None], b, n1, w_in_[None], pm.reshape(pool_mix_w.shape), ps, convw[None], cb, gp, gc,
                w_out_[None], n2, w_1[None], w_2[None], fg]

    grads = ordered(g_ada, g_parts, g_w_in, g_pm, g_convw_mine, g_w_out, g_w_1, g_w_2)
    deltas = ordered(d_ada, d_parts, upd_in[0], upd_pm[0], d_convw, upd_out[0], upd_1[0], upd_2[0])
    new_m = ordered(mn_ada, m_parts, upd_in[1], upd_pm[1], m_convw, upd_out[1], upd_1[1], upd_2[1])
    new_v = ordered(vn_ada, v_parts, upd_in[2], upd_pm[2], v_convw, upd_out[2], upd_1[2], upd_2[2])
    return (loss[0, 0], grad_x[None], *grads, *deltas, *new_m, *new_v)
```

```python
import jax
import jax.numpy as jnp
from jax import lax
from jax.experimental import pallas as pl
from jax.experimental.pallas import tpu as pltpu
from jax.experimental.pallas import tpu_sc as plsc

F32 = jnp.float32
BF16 = jnp.bfloat16
MESH = pl.DeviceIdType.MESH
ANY = pl.BlockSpec(memory_space=pl.ANY)

NORM_EPS = 1e-6
POOL_WINDOWS = (2, 4, 8, 16)
CONV_HEAD_DIM = 128
N_MOD = 6
N_CHIPS = 4
N_DEV = 8

ADAM_LR = 0.001
ADAM_B1 = 0.9
ADAM_B2 = 0.999
ADAM_EPS = 1e-08
ADAM_WD = 0.01
ADAM_STEP = 10

VMEM_LIMIT_BYTES = 56 * 1024 * 1024

NN = (((1,), (0,)), ((), ()))
NT = (((1,), (1,)), ((), ()))
TN = (((0,), (0,)), ((), ()))


def _tile(n, pref):
    t = min(n, pref)
    while n % t:
        t //= 2
    return t


def _params(*sem):
    return pltpu.CompilerParams(dimension_semantics=sem, vmem_limit_bytes=VMEM_LIMIT_BYTES)


def _position():
    return lax.axis_index("x"), lax.axis_index("y"), lax.axis_index("c")


def _flip(ix, iy, ic, mask):
    return (1 - ix if mask & 4 else ix, 1 - iy if mask & 2 else iy, 1 - ic if mask & 1 else ic)


def _allgather8(name, blk):
    rows, cols = blk.shape

    def body(x_ref, out_ref, send_sems, recv_sems, local_sem):
        ix, iy, ic = _position()
        me = 4 * ix + 2 * iy + ic
        mine = pltpu.make_async_copy(x_ref, out_ref.at[me], local_sem)
        mine.start()
        sends = []
        for mask in range(1, N_DEV):
            cp = pltpu.make_async_remote_copy(
                src_ref=x_ref, dst_ref=out_ref.at[me],
                send_sem=send_sems.at[mask - 1], recv_sem=recv_sems.at[mask - 1],
                device_id=_flip(ix, iy, ic, mask), device_id_type=MESH)
            cp.start()
            sends.append(cp)
        for mask in range(1, N_DEV):
            px, py, pc = _flip(ix, iy, ic, mask)
            pltpu.make_async_remote_copy(
                src_ref=x_ref, dst_ref=out_ref.at[4 * px + 2 * py + pc],
                send_sem=send_sems.at[mask - 1], recv_sem=recv_sems.at[mask - 1],
                device_id=(px, py, pc), device_id_type=MESH).wait_recv()
        for cp in sends:
            cp.wait_send()
        mine.wait()

    return pl.pallas_call(
        body, name=name,
        out_shape=jax.ShapeDtypeStruct((N_DEV, rows, cols), F32),
        in_specs=[pl.BlockSpec(memory_space=pltpu.VMEM)],
        out_specs=pl.BlockSpec(memory_space=pltpu.VMEM),
        scratch_shapes=[pltpu.SemaphoreType.DMA((N_DEV - 1,)), pltpu.SemaphoreType.DMA((N_DEV - 1,)),
                        pltpu.SemaphoreType.DMA],
    )(blk)


def _gather_flat(name, vec):
    n = vec.shape[0]
    npad = -(-n // 1024) * 1024
    blk = jnp.pad(vec, (0, npad - n)).reshape(8, npad // 8)
    return _allgather8(name, blk).reshape(N_DEV, npad)[:, :n]


def _chip_relations(ix, iy):
    return [(1 - ix, iy), (ix, 1 - iy), (1 - ix, 1 - iy)]


def _gather_weights(name, shards, collective_id):
    n = len(shards)

    def body(*refs):
        src, out = refs[:n], refs[n:2 * n]
        send_sems, recv_sems, local_sems = refs[2 * n:]
        ix, iy, ic = _position()
        chip = 2 * ix + iy
        sibling = (ix, iy, 1 - ic)
        rels = _chip_relations(ix, iy)

        barrier = pltpu.get_barrier_semaphore()
        for peer in [(px, py, ic) for px, py in rels] + [sibling]:
            pl.semaphore_signal(barrier, inc=1, device_id=peer, device_id_type=MESH)
        pl.semaphore_wait(barrier, 4)

        def half_rows(a, h):
            half = shards[a].shape[0] // 2
            return pl.ds(h * half, half)

        local = []
        for a in range(n):
            cp = pltpu.make_async_copy(src[a], out[a].at[chip], local_sems.at[a])
            cp.start()
            local.append(cp)
        started = []
        for a in range(n):
            for r, (px, py) in enumerate(rels):
                cp = pltpu.make_async_remote_copy(
                    src_ref=src[a].at[half_rows(a, ic)], dst_ref=out[a].at[chip, half_rows(a, ic)],
                    send_sem=send_sems.at[6 * a + r], recv_sem=recv_sems.at[6 * a + r],
                    device_id=(px, py, ic), device_id_type=MESH)
                cp.start()
                started.append(cp)
        for a in range(n):
            for r, (px, py) in enumerate(rels):
                landed = out[a].at[2 * px + py, half_rows(a, ic)]
                pltpu.make_async_remote_copy(
                    src_ref=landed, dst_ref=landed,
                    send_sem=send_sems.at[6 * a + r], recv_sem=recv_sems.at[6 * a + r],
                    device_id=(px, py, ic), device_id_type=MESH).wait_recv()
                cp = pltpu.make_async_remote_copy(
                    src_ref=landed, dst_ref=landed,
                    send_sem=send_sems.at[6 * a + 3 + r], recv_sem=recv_sems.at[6 * a + 3 + r],
                    device_id=sibling, device_id_type=MESH)
                cp.start()
                started.append(cp)
        for a in range(n):
            for r, (px, py) in enumerate(rels):
                passed = out[a].at[2 * px + py, half_rows(a, 1 - ic)]
                pltpu.make_async_remote_copy(
                    src_ref=passed, dst_ref=passed,
                    send_sem=send_sems.at[6 * a + 3 + r], recv_sem=recv_sems.at[6 * a + 3 + r],
                    device_id=sibling, device_id_type=MESH).wait_recv()
        for cp in started:
            cp.wait_send()
        for cp in local:
            cp.wait()

    return pl.kernel(
        body, name=name,
        out_type=[jax.ShapeDtypeStruct((N_CHIPS,) + s.shape, s.dtype) for s in shards],
        mesh=plsc.ScalarSubcoreMesh(axis_name="sequencer", num_cores=1),
        scratch_types=[pltpu.SemaphoreType.DMA((6 * n,)), pltpu.SemaphoreType.DMA((6 * n,)),
                       pltpu.SemaphoreType.DMA((n,))],
        compiler_params=pltpu.CompilerParams(collective_id=collective_id),
    )(*shards)


def _exchange_halves(grads):
    n = len(grads)

    def body(*refs):
        src, out = refs[:n], refs[n:2 * n]
        send_sems, recv_sems = refs[2 * n:]
        ix, iy, ic = _position()
        copies = []
        for a in range(n):
            half = grads[a].shape[1] // 2
            cp = pltpu.make_async_remote_copy(
                src_ref=src[a].at[pl.ds(0, N_CHIPS), pl.ds((1 - ic) * half, half)], dst_ref=out[a],
                send_sem=send_sems.at[a], recv_sem=recv_sems.at[a],
                device_id=(ix, iy, 1 - ic), device_id_type=MESH)
            cp.start()
            copies.append(cp)
        for cp in copies:
            cp.wait()

    return pl.pallas_call(
        body, name="exchange_halves",
        out_shape=[jax.ShapeDtypeStruct((N_CHIPS, g.shape[1] // 2, g.shape[2]), g.dtype) for g in grads],
        in_specs=[ANY] * n, out_specs=[ANY] * n,
        scratch_shapes=[pltpu.SemaphoreType.DMA((n,)), pltpu.SemaphoreType.DMA((n,))],
    )(*grads)


def _scatter_partials(parts):
    n = len(parts)

    def body(*refs):
        src, out = refs[:n], refs[n:2 * n]
        send_sems, recv_sems = refs[2 * n:]
        ix, iy, ic = _position()
        copies = []
        for a in range(n):
            for r, (px, py) in enumerate(_chip_relations(ix, iy)):
                cp = pltpu.make_async_remote_copy(
                    src_ref=src[a].at[2 * px + py], dst_ref=out[a].at[r],
                    send_sem=send_sems.at[3 * a + r], recv_sem=recv_sems.at[3 * a + r],
                    device_id=(px, py, ic), device_id_type=MESH)
                cp.start()
                copies.append(cp)
        for cp in copies:
            cp.wait()

    return pl.pallas_call(
        body, name="scatter_partials",
        out_shape=[jax.ShapeDtypeStruct((3,) + p.shape[1:], p.dtype) for p in parts],
        in_specs=[ANY] * n, out_specs=[ANY] * n,
        scratch_shapes=[pltpu.SemaphoreType.DMA((3 * n,)), pltpu.SemaphoreType.DMA((3 * n,))],
    )(*parts)


def _share_reduced(fulls):
    n = len(fulls)

    def body(*refs):
        src, out = refs[:n], refs[n:2 * n]
        send_sems, recv_sems = refs[2 * n:]
        ix, iy, ic = _position()
        copies = []
        for a in range(n):
            half = fulls[a].shape[0] // 2
            mine = pl.ds(ic * half, half)
            cp = pltpu.make_async_remote_copy(
                src_ref=src[a].at[mine], dst_ref=out[a].at[mine],
                send_sem=send_sems.at[a], recv_sem=recv_sems.at[a],
                device_id=(ix, iy, 1 - ic), device_id_type=MESH)
            cp.start()
            copies.append(cp)
        for cp in copies:
            cp.wait()

    return pl.pallas_call(
        body, name="share_reduced",
        out_shape=[jax.ShapeDtypeStruct(f.shape, f.dtype) for f in fulls],
        in_specs=[ANY] * n, out_specs=[ANY] * n,
        input_output_aliases={a: a for a in range(n)},
        scratch_shapes=[pltpu.SemaphoreType.DMA((n,)), pltpu.SemaphoreType.DMA((n,))],
    )(*fulls)


def _add_half(name, grad, recv, core):
    _, rows, cols = grad.shape
    half = rows // 2
    tr, tc = _tile(half, 512), _tile(cols, 2048)
    nbr = half // tr

    def body(core_ref, g_ref, r_ref, o_ref):
        o_ref[...] = (g_ref[...].astype(F32) + r_ref[...].astype(F32)).astype(BF16)

    return pl.pallas_call(
        body, name=name,
        grid_spec=pltpu.PrefetchScalarGridSpec(
            num_scalar_prefetch=1, grid=(N_CHIPS, nbr, cols // tc),
            in_specs=[pl.BlockSpec((None, tr, tc), lambda s, i, j, core_ref: (s, core_ref[0] * nbr + i, j)),
                      pl.BlockSpec((None, tr, tc), lambda s, i, j, core_ref: (s, i, j))],
            out_specs=pl.BlockSpec((None, tr, tc), lambda s, i, j, core_ref: (s, i, j))),
        out_shape=jax.ShapeDtypeStruct((N_CHIPS, half, cols), BF16),
        compiler_params=_params("parallel", "parallel", "parallel"),
    )(core, grad, recv)


def _reduce_chips(name, part, recv, chip_core):
    _, half, cols = part.shape
    tr, tc = _tile(half, 512), _tile(cols, 2048)
    nbr = half // tr

    def body(pos_ref, p_ref, r_ref, o_ref):
        acc = p_ref[...].astype(F32)
        for r in range(3):
            acc = acc + r_ref[r].astype(F32)
        o_ref[...] = acc

    return pl.pallas_call(
        body, name=name,
        grid_spec=pltpu.PrefetchScalarGridSpec(
            num_scalar_prefetch=1, grid=(nbr, cols // tc),
            in_specs=[pl.BlockSpec((None, tr, tc), lambda i, j, pos_ref: (pos_ref[0], i, j)),
                      pl.BlockSpec((3, tr, tc), lambda i, j, pos_ref: (0, i, j))],
            out_specs=pl.BlockSpec((tr, tc), lambda i, j, pos_ref: (pos_ref[1] * nbr + i, j))),
        out_shape=jax.ShapeDtypeStruct((2 * half, cols), F32),
        compiler_params=_params("parallel", "parallel"),
    )(chip_core, part, recv)


def _adamw(name, g, w, m, v):
    rows, cols = g.shape
    tr, tc = _tile(rows, 256), _tile(cols, 2048)
    bc1 = 1.0 - ADAM_B1 ** ADAM_STEP
    bc2 = 1.0 - ADAM_B2 ** ADAM_STEP

    def body(g_ref, w_ref, m_ref, v_ref, d_ref, mo_ref, vo_ref):
        gv = g_ref[...]
        mn = ADAM_B1 * m_ref[...] + (1.0 - ADAM_B1) * gv
        vn = ADAM_B2 * v_ref[...] + (1.0 - ADAM_B2) * (gv * gv)
        d_ref[...] = -ADAM_LR * ((mn / bc1) / (jnp.sqrt(vn / bc2) + ADAM_EPS) + ADAM_WD * w_ref[...])
        mo_ref[...] = mn
        vo_ref[...] = vn

    spec = pl.BlockSpec((tr, tc), lambda i, j: (i, j))
    shape = jax.ShapeDtypeStruct((rows, cols), F32)
    return pl.pallas_call(
        body, name=name, grid=(rows // tr, cols // tc),
        in_specs=[spec] * 4, out_specs=[spec] * 3, out_shape=[shape] * 3,
        compiler_params=_params("parallel", "parallel"),
    )(g, w, m, v)


def _reduce_small(gathered, d_model):
    n = gathered.shape[1]
    loss_at = (N_MOD + 3) * d_model

    def body(g_ref, s_ref, loss_ref):
        acc = g_ref[0:1, :]
        for d in range(1, N_DEV):
            acc = acc + g_ref[d:d + 1, :]
        s_ref[...] = acc
        lanes = acc[:, loss_at:loss_at + d_model]
        loss_ref[...] = jnp.broadcast_to((0.5 / d_model) * jnp.sum(lanes, axis=1, keepdims=True), loss_ref.shape)

    return pl.pallas_call(
        body, name="reduce_small",
        out_shape=[jax.ShapeDtypeStruct((1, n), F32), jax.ShapeDtypeStruct((1, 128), F32)],
        compiler_params=pltpu.CompilerParams(vmem_limit_bytes=VMEM_LIMIT_BYTES),
    )(gathered)


def _ada_forward(c_all, w_ada, b_cols):
    d_model, width = w_ada.shape
    tn = _tile(width, 512)

    def body(c_ref, w_ref, b_ref, o_ref):
        cv = c_ref[...]
        act = cv * jax.nn.sigmoid(cv)
        o_ref[...] = lax.dot_general(act, w_ref[...], NN, precision=lax.Precision.HIGHEST,
                                     preferred_element_type=F32) + b_ref[...]

    return pl.pallas_call(
        body, name="ada_forward", grid=(width // tn,),
        in_specs=[pl.BlockSpec((N_DEV, d_model), lambda j: (0, 0)),
                  pl.BlockSpec((d_model, tn), lambda j: (0, j)),
                  pl.BlockSpec((1, tn), lambda j: (0, j))],
        out_specs=pl.BlockSpec((N_DEV, tn), lambda j: (0, j)),
        out_shape=jax.ShapeDtypeStruct((N_DEV, width), F32),
        compiler_params=_params("parallel"),
    )(c_all, w_ada, b_cols)


def _ada_backward(c_all_t, dmod_cols, w, m, v):
    d_model, width = w.shape
    tr, tc = _tile(d_model, 256), _tile(width, 1536)
    bc1 = 1.0 - ADAM_B1 ** ADAM_STEP
    bc2 = 1.0 - ADAM_B2 ** ADAM_STEP

    def body(c_ref, dm_ref, w_ref, m_ref, v_ref, g_ref, d_ref, mo_ref, vo_ref):
        cv = c_ref[...]
        act = cv * jax.nn.sigmoid(cv)
        gv = lax.dot_general(act, dm_ref[...], NN, precision=lax.Precision.HIGHEST, preferred_element_type=F32)
        mn = ADAM_B1 * m_ref[...] + (1.0 - ADAM_B1) * gv
        vn = ADAM_B2 * v_ref[...] + (1.0 - ADAM_B2) * (gv * gv)
        g_ref[...] = gv
        d_ref[...] = -ADAM_LR * ((mn / bc1) / (jnp.sqrt(vn / bc2) + ADAM_EPS) + ADAM_WD * w_ref[...])
        mo_ref[...] = mn
        vo_ref[...] = vn

    spec = pl.BlockSpec((tr, tc), lambda i, j: (i, j))
    shape = jax.ShapeDtypeStruct((d_model, width), F32)
    return pl.pallas_call(
        body, name="ada_backward", grid=(d_model // tr, width // tc),
        in_specs=[pl.BlockSpec((tr, N_DEV), lambda i, j: (i, 0)),
                  pl.BlockSpec((N_DEV, tc), lambda i, j: (0, j)), spec, spec, spec],
        out_specs=[spec] * 4, out_shape=[shape] * 4,
        compiler_params=_params("parallel", "parallel"),
    )(c_all_t, dmod_cols, w, m, v)


def _matmul(name, a, b, extras, *, grid, tiles, dims, a_spec, b_spec, extra_specs, out_shape, out_specs,
            epilogue, prologue=None):
    tm, tn, _ = tiles
    gm, gn, gk = grid
    n_extra, n_out = len(extras), len(out_shape)

    def body(*refs):
        a_ref, b_ref = refs[0], refs[1]
        extra_refs = refs[2:2 + n_extra]
        out_refs = refs[2 + n_extra:2 + n_extra + n_out]
        acc_ref = refs[-1]
        k = pl.program_id(2)

        @pl.when(k == 0)
        def _():
            acc_ref[...] = jnp.zeros_like(acc_ref)

        av = a_ref[...]
        if prologue is not None:
            av = prologue(av)
        acc_ref[...] += lax.dot_general(av, b_ref[...], dims, preferred_element_type=F32)

        @pl.when(k == gk - 1)
        def _():
            epilogue(acc_ref[...], extra_refs, out_refs)

    return pl.pallas_call(
        body, name=name, grid=(gm, gn, gk),
        in_specs=[a_spec, b_spec, *extra_specs], out_specs=out_specs, out_shape=out_shape,
        scratch_shapes=[pltpu.VMEM((tm, tn), F32)],
        compiler_params=_params("parallel", "parallel", "arbitrary"),
    )(a, b, *extras)


def _store(dtype):
    def epilogue(acc, extra_refs, out_refs):
        out_refs[0][...] = acc.astype(dtype)
    return epilogue


def _residual_epilogue(acc, extra_refs, out_refs):
    res_ref, gate_ref = extra_refs
    out_refs[0][...] = res_ref[...] + gate_ref[...] * acc
    out_refs[1][...] = acc.astype(BF16)


def _square(av):
    af = av.astype(F32)
    return (af * af).astype(BF16)


MM_TILE_M = 1024
MM_TILE_N = 1024
MM_TILE_K = 1024


def _in_projection(h, w_slabs):
    seq, d_model = h.shape
    _, _, cols = w_slabs.shape
    tm, tn, tk = _tile(seq, MM_TILE_M), _tile(cols, MM_TILE_N), _tile(d_model, MM_TILE_K)
    nbj = cols // tn
    return _matmul(
        "in_projection", h, w_slabs, (), grid=(seq // tm, N_CHIPS * nbj, d_model // tk), tiles=(tm, tn, tk), dims=NN,
        a_spec=pl.BlockSpec((tm, tk), lambda i, j, k: (i, k)),
        b_spec=pl.BlockSpec((None, tk, tn), lambda i, j, k: (j // nbj, k, j % nbj)),
        extra_specs=(),
        out_shape=[jax.ShapeDtypeStruct((N_CHIPS, seq, cols), BF16)],
        out_specs=[pl.BlockSpec((None, tm, tn), lambda i, j, k: (j // nbj, i, j % nbj))],
        epilogue=_store(BF16))[0]


def _residual_projection(name, a, w, res, gate, prologue=None):
    seq, kdim = a.shape
    d_model = w.shape[1]
    tm, tn, tk = _tile(seq, MM_TILE_M), _tile(d_model, MM_TILE_N), _tile(kdim, MM_TILE_K)
    tile = pl.BlockSpec((tm, tn), lambda i, j, k: (i, j))
    return _matmul(
        name, a, w, (res, gate), grid=(seq // tm, d_model // tn, kdim // tk), tiles=(tm, tn, tk), dims=NN,
        a_spec=pl.BlockSpec((tm, tk), lambda i, j, k: (i, k)),
        b_spec=pl.BlockSpec((tk, tn), lambda i, j, k: (k, j)),
        extra_specs=(tile, pl.BlockSpec((1, tn), lambda i, j, k: (0, j))),
        out_shape=[jax.ShapeDtypeStruct((seq, d_model), F32), jax.ShapeDtypeStruct((seq, d_model), BF16)],
        out_specs=[tile, tile],
        epilogue=_residual_epilogue, prologue=prologue)


def _mlp_in(h, w_slabs):
    seq, d_model = h.shape
    _, _, cols = w_slabs.shape
    tm, tn, tk = _tile(seq, MM_TILE_M), _tile(cols, MM_TILE_N), _tile(d_model, MM_TILE_K)
    nbj = cols // tn

    def epilogue(acc, extra_refs, out_refs):
        out_refs[0][...] = jnp.maximum(acc, 0.0).astype(BF16)

    return _matmul(
        "mlp_in", h, w_slabs, (), grid=(seq // tm, N_CHIPS * nbj, d_model // tk), tiles=(tm, tn, tk), dims=NN,
        a_spec=pl.BlockSpec((tm, tk), lambda i, j, k: (i, k)),
        b_spec=pl.BlockSpec((None, tk, tn), lambda i, j, k: (j // nbj, k, j % nbj)),
        extra_specs=(),
        out_shape=[jax.ShapeDtypeStruct((seq, N_CHIPS * cols), BF16)],
        out_specs=[pl.BlockSpec((tm, tn), lambda i, j, k: (i, j))],
        epilogue=epilogue)[0]


def _grad_hidden(dmlp, w2, act):
    seq, d_model = dmlp.shape
    ff = w2.shape[0]
    tm, tn, tk = _tile(seq, MM_TILE_M), _tile(ff, MM_TILE_N), _tile(d_model, MM_TILE_K)

    def epilogue(acc, extra_refs, out_refs):
        out_refs[0][...] = (acc * (2.0 * extra_refs[0][...].astype(F32))).astype(BF16)

    tile = pl.BlockSpec((tm, tn), lambda i, j, k: (i, j))
    return _matmul(
        "grad_hidden", dmlp, w2, (act,), grid=(seq // tm, ff // tn, d_model // tk), tiles=(tm, tn, tk), dims=NT,
        a_spec=pl.BlockSpec((tm, tk), lambda i, j, k: (i, k)),
        b_spec=pl.BlockSpec((tn, tk), lambda i, j, k: (j, k)),
        extra_specs=(tile,),
        out_shape=[jax.ShapeDtypeStruct((seq, ff), BF16)], out_specs=[tile],
        epilogue=epilogue)[0]


def _weight_grad(name, a, b, prologue=None):
    seq, m = a.shape
    n = b.shape[1]
    tm, tn, tk = _tile(m, MM_TILE_M), _tile(n, MM_TILE_N), _tile(seq, MM_TILE_K)
    return _matmul(
        name, a, b, (), grid=(m // tm, n // tn, seq // tk), tiles=(tm, tn, tk), dims=TN,
        a_spec=pl.BlockSpec((tk, tm), lambda i, j, k: (k, i)),
        b_spec=pl.BlockSpec((tk, tn), lambda i, j, k: (k, j)),
        extra_specs=(),
        out_shape=[jax.ShapeDtypeStruct((m, n), BF16)],
        out_specs=[pl.BlockSpec((tm, tn), lambda i, j, k: (i, j))],
        epilogue=_store(BF16), prologue=prologue)[0]


def _weight_grad_slabs(name, a, b, slab_cols):
    seq, m = a.shape
    cols = b.shape[2] if slab_cols is None else slab_cols
    tm, tn, tk = _tile(m, MM_TILE_M), _tile(cols, MM_TILE_N), _tile(seq, MM_TILE_K)
    nbj = cols // tn
    if slab_cols is None:
        b_spec = pl.BlockSpec((None, tk, tn), lambda i, j, k: (j // nbj, k, j % nbj))
    else:
        b_spec = pl.BlockSpec((tk, tn), lambda i, j, k: (k, j))
    return _matmul(
        name, a, b, (), grid=(m // tm, N_CHIPS * nbj, seq // tk), tiles=(tm, tn, tk), dims=TN,
        a_spec=pl.BlockSpec((tk, tm), lambda i, j, k: (k, i)),
        b_spec=b_spec, extra_specs=(),
        out_shape=[jax.ShapeDtypeStruct((N_CHIPS, m, cols), BF16)],
        out_specs=[pl.BlockSpec((None, tm, tn), lambda i, j, k: (j // nbj, i, j % nbj))],
        epilogue=_store(BF16))[0]


def _grad_input_slabs(name, dy, w_slabs):
    _, d_model, cols = w_slabs.shape
    seq = dy.shape[1] if dy.ndim == 3 else dy.shape[0]
    tm, tn, tk = _tile(seq, MM_TILE_M), _tile(d_model, MM_TILE_N), _tile(cols, MM_TILE_K)
    nbk = cols // tk
    if dy.ndim == 3:
        a_spec = pl.BlockSpec((None, tm, tk), lambda i, j, k: (k // nbk, i, k % nbk))
    else:
        a_spec = pl.BlockSpec((tm, tk), lambda i, j, k: (i, k))
    return _matmul(
        name, dy, w_slabs, (), grid=(seq // tm, d_model // tn, N_CHIPS * nbk), tiles=(tm, tn, tk), dims=NT,
        a_spec=a_spec,
        b_spec=pl.BlockSpec((None, tn, tk), lambda i, j, k: (k // nbk, j, k % nbk)),
        extra_specs=(),
        out_shape=[jax.ShapeDtypeStruct((seq, d_model), F32)],
        out_specs=[pl.BlockSpec((tm, tn), lambda i, j, k: (i, j))],
        epilogue=_store(F32))[0]


def _grad_input(name, dy, w):
    seq, n = dy.shape
    kdim = w.shape[0]
    tm, tn, tk = _tile(seq, MM_TILE_M), _tile(kdim, MM_TILE_N), _tile(n, MM_TILE_K)
    return _matmul(
        name, dy, w, (), grid=(seq // tm, kdim // tn, n // tk), tiles=(tm, tn, tk), dims=NT,
        a_spec=pl.BlockSpec((tm, tk), lambda i, j, k: (i, k)),
        b_spec=pl.BlockSpec((tn, tk), lambda i, j, k: (j, k)),
        extra_specs=(),
        out_shape=[jax.ShapeDtypeStruct((seq, kdim), F32)],
        out_specs=[pl.BlockSpec((tm, tn), lambda i, j, k: (i, j))],
        epilogue=_store(F32))[0]


ROW_TILE = 128


def _norm_modulate(name, xin, g, scale, shift):
    seq, d_model = xin.shape
    tr = _tile(seq, ROW_TILE)

    def body(x_ref, g_ref, sc_ref, sh_ref, h_ref):
        xv = x_ref[...]
        r = lax.rsqrt(jnp.mean(xv * xv, axis=-1, keepdims=True) + NORM_EPS)
        h_ref[...] = (((xv * r) * g_ref[...]) * (1.0 + sc_ref[...]) + sh_ref[...]).astype(BF16)

    row = pl.BlockSpec((tr, d_model), lambda i: (i, 0))
    vec = pl.BlockSpec((1, d_model), lambda i: (0, 0))
    return pl.pallas_call(
        body, name=name, grid=(seq // tr,),
        in_specs=[row, vec, vec, vec], out_specs=row,
        out_shape=jax.ShapeDtypeStruct((seq, d_model), BF16),
        compiler_params=_params("parallel"),
    )(xin, g, scale, shift)


def _loss_head(x2, target, final_g, mlp, gate2):
    seq, d_model = x2.shape
    tr = _tile(seq, ROW_TILE)

    def body(x_ref, t_ref, fg_ref, mlp_ref, gate_ref, dx_ref, dmlp_ref, gfg_ref, dgate_ref, sq_ref):
        @pl.when(pl.program_id(0) == 0)
        def _():
            gfg_ref[...] = jnp.zeros_like(gfg_ref)
            dgate_ref[...] = jnp.zeros_like(dgate_ref)
            sq_ref[...] = jnp.zeros_like(sq_ref)

        xv = x_ref[...]
        fg = fg_ref[...]
        r = lax.rsqrt(jnp.mean(xv * xv, axis=-1, keepdims=True) + NORM_EPS)
        n = xv * r
        err = n * fg - t_ref[...]
        sq_ref[...] += jnp.sum(err * err, axis=0, keepdims=True)
        dy = err * (1.0 / d_model)
        gfg_ref[...] += jnp.sum(dy * n, axis=0, keepdims=True)
        dn = dy * fg
        dx = r * (dn - n * jnp.mean(dn * n, axis=-1, keepdims=True))
        dx_ref[...] = dx
        dgate_ref[...] += jnp.sum(dx * mlp_ref[...].astype(F32), axis=0, keepdims=True)
        dmlp_ref[...] = (dx * gate_ref[...]).astype(BF16)

    row = pl.BlockSpec((tr, d_model), lambda i: (i, 0))
    vec = pl.BlockSpec((1, d_model), lambda i: (0, 0))
    vshape = jax.ShapeDtypeStruct((1, d_model), F32)
    return pl.pallas_call(
        body, name="loss_head", grid=(seq // tr,),
        in_specs=[row, row, vec, row, vec], out_specs=[row, row, vec, vec, vec],
        out_shape=[jax.ShapeDtypeStruct((seq, d_model), F32), jax.ShapeDtypeStruct((seq, d_model), BF16),
                   vshape, vshape, vshape],
        compiler_params=_params("arbitrary"),
    )(x2, target, final_g, mlp, gate2)


def _norm_modulate_backward(name, dh, xin, g, scale, dres, branch=None, gate=None):
    seq, d_model = xin.shape
    tr = _tile(seq, ROW_TILE)
    with_branch = branch is not None
    n_in = 7 if with_branch else 5

    def body(*refs):
        dh_ref, x_ref, g_ref, sc_ref, dres_ref = refs[:5]
        outs = refs[n_in:]
        dx_ref, dsc_ref, dsh_ref, dg_ref = outs[:4]

        @pl.when(pl.program_id(0) == 0)
        def _():
            for ref in outs[1:5] if with_branch else outs[1:4]:
                ref[...] = jnp.zeros_like(ref)

        xv = x_ref[...]
        gv = g_ref[...]
        dhv = dh_ref[...]
        r = lax.rsqrt(jnp.mean(xv * xv, axis=-1, keepdims=True) + NORM_EPS)
        xn = xv * r
        dsh_ref[...] += jnp.sum(dhv, axis=0, keepdims=True)
        dsc_ref[...] += jnp.sum(dhv * (xn * gv), axis=0, keepdims=True)
        t = dhv * (1.0 + sc_ref[...])
        dg_ref[...] += jnp.sum(t * xn, axis=0, keepdims=True)
        dxn = t * gv
        dx = dres_ref[...] + r * (dxn - xn * jnp.mean(dxn * xn, axis=-1, keepdims=True))
        dx_ref[...] = dx
        if with_branch:
            br_ref, gate_ref = refs[5:7]
            dgate_ref, dbr_ref = outs[4:6]
            dgate_ref[...] += jnp.sum(dx * br_ref[...].astype(F32), axis=0, keepdims=True)
            dbr_ref[...] = (dx * gate_ref[...]).astype(BF16)

    row = pl.BlockSpec((tr, d_model), lambda i: (i, 0))
    vec = pl.BlockSpec((1, d_model), lambda i: (0, 0))
    vshape = jax.ShapeDtypeStruct((1, d_model), F32)
    in_specs = [row, row, vec, vec, row]
    out_specs = [row, vec, vec, vec]
    out_shape = [jax.ShapeDtypeStruct((seq, d_model), F32), vshape, vshape, vshape]
    args = [dh, xin, g, scale, dres]
    if with_branch:
        in_specs += [row, vec]
        out_specs += [vec, row]
        out_shape += [vshape, jax.ShapeDtypeStruct((seq, d_model), BF16)]
        args += [branch, gate]
    return pl.pallas_call(
        body, name=name, grid=(seq // tr,),
        in_specs=in_specs, out_specs=out_specs, out_shape=out_shape,
        compiler_params=_params("arbitrary"),
    )(*args)


def _shifted(v, k, t):
    seq = v.shape[0]
    if k == 0:
        return v
    moved = pltpu.roll(v, (-k) % seq, 0)
    return jnp.where((t + k >= 0) & (t + k < seq), moved, 0.0)


def _window_sum(v, offsets, t):
    acc = None
    for k in offsets:
        term = _shifted(v, k, t)
        acc = term if acc is None else acc + term
    return acc


def _window_count(seq, half):
    t = lax.broadcasted_iota(jnp.int32, (seq, 1), 0)
    return (jnp.minimum(t + half, seq) - jnp.maximum(t - half, 0)).astype(F32)


def _pool_forward(proj, group_dim):
    _, seq, cols = proj.shape
    tl = _tile(group_dim, 256)
    nbl = group_dim // tl
    n_groups = cols // group_dim

    def body(v_ref, o_ref):
        g = pl.program_id(0)
        for gi, window in enumerate(POOL_WINDOWS[:n_groups]):
            @pl.when(g == gi)
            def _(window=window):
                half = window // 2
                v = v_ref[...].astype(F32)
                t = lax.broadcasted_iota(jnp.int32, v.shape, 0)
                total = _window_sum(v, range(-half, half), t)
                o_ref[...] = (total / _window_count(seq, half) - v).astype(BF16)

    return pl.pallas_call(
        body, name="pool_forward", grid=(n_groups, nbl),
        in_specs=[pl.BlockSpec((None, seq, tl), lambda g, j: (0, 0, g * nbl + j))],
        out_specs=pl.BlockSpec((seq, tl), lambda g, j: (0, g * nbl + j)),
        out_shape=jax.ShapeDtypeStruct((seq, cols), BF16),
        compiler_params=_params("parallel", "parallel"),
    )(proj)


def _group_matrix(w_ref):
    return jnp.concatenate([w_ref[r] for r in range(N_CHIPS)], axis=0)


def _pool_mix_forward(pooled, w_pm, pool_scale, gnorm_g, d_model):
    seq, cols = pooled.shape
    _, n_groups, shard_rows, group_dim = w_pm.shape
    tm = _tile(seq, 512)

    def body(p_ref, w_ref, ps_ref, g_ref, o_ref):
        a = jnp.dot(p_ref[...], _group_matrix(w_ref), preferred_element_type=F32) * ps_ref[...]
        ra = lax.rsqrt(jnp.mean(a * a, axis=-1, keepdims=True) + NORM_EPS)
        o_ref[...] = ((a * ra) * g_ref[...]).astype(BF16)

    tile = pl.BlockSpec((tm, group_dim), lambda g, i: (i, g))
    vec = pl.BlockSpec((1, group_dim), lambda g, i: (0, g))
    return pl.pallas_call(
        body, name="pool_mix_forward", grid=(n_groups, seq // tm),
        in_specs=[tile, pl.BlockSpec((N_CHIPS, None, shard_rows, group_dim), lambda g, i: (0, g, 0, 0)), vec, vec],
        out_specs=tile,
        out_shape=jax.ShapeDtypeStruct((seq, d_model), BF16),
        compiler_params=_params("parallel", "parallel"),
    )(pooled, w_pm, pool_scale, gnorm_g)


def _conv_parts(b_ref, c_ref, u_ref, w_ref, bias_ref):
    bv = b_ref[...].astype(F32)
    cu = c_ref[...].astype(F32) * u_ref[...].astype(F32)
    t = lax.broadcasted_iota(jnp.int32, cu.shape, 0)
    prev, nxt = _shifted(cu, -1, t), _shifted(cu, 1, t)
    w = w_ref[...]
    conv = w[0:1] * prev + w[1:2] * cu + w[2:3] * nxt + bias_ref[...]
    return bv, cu, prev, nxt, conv, w, t


def _conv_forward(proj, conv_w, conv_b, gnorm_g, mixed):
    _, seq, cols = proj.shape
    tl = CONV_HEAD_DIM
    first = cols // tl

    def body(b_ref, c_ref, u_ref, w_ref, bias_ref, g_ref, mixed_ref, o_ref):
        bv, _, _, _, conv, _, _ = _conv_parts(b_ref, c_ref, u_ref, w_ref, bias_ref)
        bo = bv * conv
        rb = lax.rsqrt(jnp.mean(bo * bo, axis=-1, keepdims=True) + NORM_EPS)
        o_ref[...] = ((bo * rb) * g_ref[...]).astype(BF16)

    def slab(s):
        return pl.BlockSpec((None, seq, tl), lambda j, s=s: (s, 0, j))

    vec = pl.BlockSpec((1, tl), lambda j: (0, j))
    return pl.pallas_call(
        body, name="conv_forward", grid=(cols // tl,),
        in_specs=[slab(1), slab(2), slab(3), pl.BlockSpec((3, tl), lambda j: (0, j)), vec, vec, ANY],
        out_specs=pl.BlockSpec((seq, tl), lambda j: (0, first + j)),
        out_shape=jax.ShapeDtypeStruct(mixed.shape, mixed.dtype),
        input_output_aliases={6: 0},
        compiler_params=_params("parallel"),
    )(proj, proj, proj, conv_w, conv_b, gnorm_g, mixed)


def _pool_mix_backward(dmixed, pooled, w_pm, pool_scale, gnorm_g):
    seq, cols = pooled.shape
    _, n_groups, shard_rows, group_dim = w_pm.shape
    tm = _tile(seq, 512)

    def body(dm_ref, p_ref, w_ref, ps_ref, g_ref, dp_ref, dpm_ref, gg_ref, gps_ref):
        @pl.when(pl.program_id(1) == 0)
        def _():
            gg_ref[...] = jnp.zeros_like(gg_ref)
            gps_ref[...] = jnp.zeros_like(gps_ref)

        w = _group_matrix(w_ref)
        ps = ps_ref[...]
        a_pre = jnp.dot(p_ref[...], w, preferred_element_type=F32)
        a = a_pre * ps
        ra = lax.rsqrt(jnp.mean(a * a, axis=-1, keepdims=True) + NORM_EPS)
        an = a * ra
        dm = dm_ref[...]
        gg_ref[...] += jnp.sum(dm * an, axis=0, keepdims=True)
        dan = dm * g_ref[...]
        da = ra * (dan - an * jnp.mean(dan * an, axis=-1, keepdims=True))
        gps_ref[...] += jnp.sum(da * a_pre, axis=0, keepdims=True)
        dpm = (da * ps).astype(BF16)
        dpm_ref[...] = dpm
        dp_ref[...] = lax.dot_general(dpm, w, NT, preferred_element_type=F32)

    tile = pl.BlockSpec((tm, group_dim), lambda g, i: (i, g))
    vec = pl.BlockSpec((1, group_dim), lambda g, i: (0, g))
    vshape = jax.ShapeDtypeStruct((1, cols), F32)
    return pl.pallas_call(
        body, name="pool_mix_backward", grid=(n_groups, seq // tm),
        in_specs=[tile, tile, pl.BlockSpec((N_CHIPS, None, shard_rows, group_dim), lambda g, i: (0, g, 0, 0)),
                  vec, vec],
        out_specs=[tile, tile, vec, vec],
        out_shape=[jax.ShapeDtypeStruct((seq, cols), F32), jax.ShapeDtypeStruct((seq, cols), BF16), vshape, vshape],
        compiler_params=_params("parallel", "arbitrary"),
    )(dmixed, pooled, w_pm, pool_scale, gnorm_g)


def _pool_mix_weight_grad(pooled, dpm, n_groups):
    seq, cols = pooled.shape
    group_dim = cols // n_groups
    shard_rows = group_dim // N_CHIPS
    tk = _tile(seq, 1024)
    gk = seq // tk

    def body(p_ref, d_ref, o_ref, acc_ref):
        k = pl.program_id(1)

        @pl.when(k == 0)
        def _():
            acc_ref[...] = jnp.zeros_like(acc_ref)

        acc_ref[...] += lax.dot_general(p_ref[...], d_ref[...], TN, preferred_element_type=F32)

        @pl.when(k == gk - 1)
        def _():
            for r in range(N_CHIPS):
                o_ref[r] = acc_ref[r * shard_rows:(r + 1) * shard_rows, :].astype(BF16)

    tile = pl.BlockSpec((tk, group_dim), lambda g, k: (k, g))
    return pl.pallas_call(
        body, name="pool_mix_weight_grad", grid=(n_groups, gk),
        in_specs=[tile, tile],
        out_specs=pl.BlockSpec((N_CHIPS, None, shard_rows, group_dim), lambda g, k: (0, g, 0, 0)),
        out_shape=jax.ShapeDtypeStruct((N_CHIPS, n_groups, shard_rows, group_dim), BF16),
        scratch_shapes=[pltpu.VMEM((group_dim, group_dim), F32)],
        compiler_params=_params("parallel", "arbitrary"),
    )(pooled, dpm)


def _mixers_backward(dpooled, dmixed, proj, conv_w, conv_b, gnorm_g, group_dim):
    _, seq, cols = proj.shape
    tl = CONV_HEAD_DIM
    first = cols // tl
    per_group = group_dim // tl
    n_groups = cols // group_dim

    def body(dp_ref, dm_ref, b_ref, c_ref, u_ref, w_ref, bias_ref, g_ref, o_ref, gg_ref, gb_ref, gw_ref):
        j = pl.program_id(0)
        for gi, window in enumerate(POOL_WINDOWS[:n_groups]):
            @pl.when(j // per_group == gi)
            def _(window=window):
                half = window // 2
                dp = dp_ref[...]
                t = lax.broadcasted_iota(jnp.int32, dp.shape, 0)
                dq = dp / _window_count(seq, half)
                o_ref[0] = (_window_sum(dq, range(-half + 1, half + 1), t) - dp).astype(BF16)

        bv, cu, prev, nxt, conv, w, t = _conv_parts(b_ref, c_ref, u_ref, w_ref, bias_ref)
        bo = bv * conv
        rb = lax.rsqrt(jnp.mean(bo * bo, axis=-1, keepdims=True) + NORM_EPS)
        bn = bo * rb
        dm = dm_ref[...]
        gg_ref[...] = jnp.sum(dm * bn, axis=0, keepdims=True)
        dbn = dm * g_ref[...]
        dbo = rb * (dbn - bn * jnp.mean(dbn * bn, axis=-1, keepdims=True))
        o_ref[1] = (dbo * conv).astype(BF16)
        dconv = dbo * bv
        gb_ref[...] = jnp.sum(dconv, axis=0, keepdims=True)
        gw_ref[0:1, :] = jnp.sum(dconv * prev, axis=0, keepdims=True)
        gw_ref[1:2, :] = jnp.sum(dconv * cu, axis=0, keepdims=True)
        gw_ref[2:3, :] = jnp.sum(dconv * nxt, axis=0, keepdims=True)
        dcu = w[0:1] * _shifted(dconv, 1, t) + w[1:2] * dconv + w[2:3] * _shifted(dconv, -1, t)
        o_ref[2] = (dcu * u_ref[...].astype(F32)).astype(BF16)
        o_ref[3] = (dcu * c_ref[...].astype(F32)).astype(BF16)

    def slab(s):
        return pl.BlockSpec((None, seq, tl), lambda j, s=s: (s, 0, j))

    vec = pl.BlockSpec((1, tl), lambda j: (0, j))
    rows3 = pl.BlockSpec((3, tl), lambda j: (0, j))
    vshape = jax.ShapeDtypeStruct((1, cols), F32)
    return pl.pallas_call(
        body, name="mixers_backward", grid=(cols // tl,),
        in_specs=[pl.BlockSpec((seq, tl), lambda j: (0, j)), pl.BlockSpec((seq, tl), lambda j: (0, first + j)),
                  slab(1), slab(2), slab(3), rows3, vec, vec],
        out_specs=[pl.BlockSpec((N_CHIPS, seq, tl), lambda j: (0, 0, j)), vec, vec, rows3],
        out_shape=[jax.ShapeDtypeStruct((N_CHIPS, seq, cols), BF16), vshape, vshape,
                   jax.ShapeDtypeStruct((3, cols), F32)],
        compiler_params=_params("parallel"),
    )(dpooled, dmixed, proj, proj, proj, conv_w, conv_b, gnorm_g)


def _reduce_scatter(grads, core, chip_core):
    received = _exchange_halves(grads)
    parts = [_add_half(f"add_half_{a}", g, r, core) for a, (g, r) in enumerate(zip(grads, received))]
    landed = _scatter_partials(parts)
    fulls = [_reduce_chips(f"reduce_chips_{a}", p, l, chip_core) for a, (p, l) in enumerate(zip(parts, landed))]
    return _share_reduced(fulls)


def kernel(x, c, w_ada, b_ada, norm1_g, w_in, pool_mix_w, pool_scale, conv_w, conv_b, gnorm_pool_g, gnorm_conv_g, w_out, norm2_g, w_mlp_in, w_mlp_out, final_g, loss_target, m_w_ada, m_b_ada, m_norm1_g, m_w_in, m_pool_mix_w, m_pool_scale, m_conv_w, m_conv_b, m_gnorm_pool_g, m_gnorm_conv_g, m_w_out, m_norm2_g, m_w_mlp_in, m_w_mlp_out, m_final_g, v_w_ada, v_b_ada, v_norm1_g, v_w_in, v_pool_mix_w, v_pool_scale, v_conv_w, v_conv_b, v_gnorm_pool_g, v_gnorm_conv_g, v_w_out, v_norm2_g, v_w_mlp_in, v_w_mlp_out, v_final_g):
    seq, d_model = x.shape[1], x.shape[2]
    cols = w_in.shape[2]
    n_groups, group_dim = pool_mix_w.shape[1], pool_mix_w.shape[3]
    shard_rows = pool_mix_w.shape[2]
    ff_cols = w_mlp_in.shape[2]
    ada_cols = w_ada.shape[2]
    conv_shard = conv_w.shape[2]
    assert pool_scale.shape[1] == cols and conv_b.shape[1] == cols and n_groups * group_dim == cols
    assert cols % CONV_HEAD_DIM == 0 and group_dim % CONV_HEAD_DIM == 0 and shard_rows * N_CHIPS == group_dim

    ix, iy, ic = _position()
    chip = 2 * ix + iy
    me = 4 * ix + 2 * iy + ic
    core = jnp.reshape(ic, (1,)).astype(jnp.int32)
    chip_core = jnp.stack([chip, ic]).astype(jnp.int32)

    xs, target = x[0], loss_target[0]
    final_row = final_g.reshape(1, d_model)

    small = _gather_flat("gather_cond", jnp.concatenate([c[0], conv_w[0].reshape(-1)]))
    c_all = small[:, :d_model]
    conv_w_full = jnp.concatenate(
        [small[2 * j, d_model:].reshape(3, conv_shard) for j in range(N_CHIPS)], axis=1)
    b_cols = lax.dynamic_slice_in_dim(b_ada, chip * ada_cols, ada_cols, axis=1)
    mod_part = _ada_forward(c_all, w_ada[0], b_cols)
    mod_all = _gather_flat("gather_mod", mod_part.reshape(-1)).reshape(N_DEV, N_DEV, ada_cols)
    mod = jnp.concatenate(
        [lax.dynamic_slice_in_dim(mod_all[2 * j], me, 1, axis=0) for j in range(N_CHIPS)], axis=1)
    shift1, scale1, gate1, shift2, scale2, gate2 = [mod[:, i * d_model:(i + 1) * d_model] for i in range(N_MOD)]

    shards = [w_in[0].astype(BF16), pool_mix_w[0].reshape(n_groups * shard_rows, group_dim).astype(BF16),
              w_out[0].astype(BF16), w_mlp_in[0].astype(BF16), w_mlp_out[0].astype(BF16)]
    wg_in, wg_pm = _gather_weights("gather_w_in", shards[0:2], 1)
    (wg_out,) = _gather_weights("gather_w_out", shards[2:3], 2)
    (wg_1,) = _gather_weights("gather_w_mlp_in", shards[3:4], 3)
    (wg_2,) = _gather_weights("gather_w_mlp_out", shards[4:5], 4)
    wg_pm = wg_pm.reshape(N_CHIPS, n_groups, shard_rows, group_dim)
    wg_out = wg_out.reshape(d_model, d_model)
    wg_2 = wg_2.reshape(N_CHIPS * ff_cols, d_model)

    h1 = _norm_modulate("norm_modulate_1", xs, norm1_g, scale1, shift1)
    proj = _in_projection(h1, wg_in)
    pooled = _pool_forward(proj, group_dim)
    mixed = _pool_mix_forward(pooled, wg_pm, pool_scale, gnorm_pool_g, d_model)
    mixed = _conv_forward(proj, conv_w_full, conv_b, gnorm_conv_g, mixed)
    x1, attn = _residual_projection("out_projection", mixed, wg_out, xs, gate1)
    h2 = _norm_modulate("norm_modulate_2", x1, norm2_g, scale2, shift2)
    act = _mlp_in(h2, wg_1)
    x2, mlp = _residual_projection("mlp_out", act, wg_2, x1, gate2, prologue=_square)

    dx2, dmlp, g_final, dgate2, sq_err = _loss_head(x2, target, final_row, mlp, gate2)
    dhid = _grad_hidden(dmlp, wg_2, act)
    gw_2 = _weight_grad("grad_w_mlp_out", act, dmlp, prologue=_square)
    gw_1 = _weight_grad_slabs("grad_w_mlp_in", h2, dhid, ff_cols)
    dh2 = _grad_input_slabs("grad_h2", dhid, wg_1)
    dx1, dscale2, dshift2, g_norm2, dgate1, dattn = _norm_modulate_backward(
        "norm_modulate_backward_2", dh2, x1, norm2_g, scale2, dx2, attn, gate1)
    dmixed = _grad_input("grad_mixed", dattn, wg_out)
    gw_out = _weight_grad("grad_w_out", mixed, dattn)
    dpooled, dpm, g_gpool, g_pscale = _pool_mix_backward(dmixed, pooled, wg_pm, pool_scale, gnorm_pool_g)
    gw_pm = _pool_mix_weight_grad(pooled, dpm, n_groups)
    dproj, g_gconv, g_convb, g_convw = _mixers_backward(
        dpooled, dmixed, proj, conv_w_full, conv_b, gnorm_conv_g, group_dim)
    gw_in = _weight_grad_slabs("grad_w_in", h1, dproj, None)
    dh1 = _grad_input_slabs("grad_h1", dproj, wg_in)
    grad_x, dscale1, dshift1, g_norm1 = _norm_modulate_backward(
        "norm_modulate_backward_1", dh1, xs, norm1_g, scale1, dx1)

    big = [gw_in, gw_pm.reshape(N_CHIPS, n_groups * shard_rows, group_dim),
           gw_out.reshape(N_CHIPS, d_model // N_CHIPS, d_model), gw_1,
           gw_2.reshape(N_CHIPS, ff_cols, d_model)]
    g_w_in, g_pm, g_w_out, g_w_1, g_w_2 = _reduce_scatter(big, core, chip_core)

    mine = jnp.concatenate(
        [dshift1, dscale1, dgate1, dshift2, dscale2, dgate2, g_norm1, g_norm2, g_final, sq_err,
         g_pscale, g_convb, g_gpool, g_gconv, g_convw.reshape(1, 3 * cols)], axis=1)
    gathered = _gather_flat("gather_small", mine.reshape(-1))
    sums, loss = _reduce_small(gathered, d_model)
    n_rep = (N_MOD + 3) * d_model
    g_rep = jnp.concatenate([sums[:, :n_rep], sums[:, n_rep + d_model:n_rep + d_model + 4 * cols]], axis=1)
    n_small = g_rep.shape[1]

    def pack(b, n1, n2, fg, ps, cb, gp, gc):
        return jnp.concatenate([b, n1, n2, fg.reshape(1, d_model), ps, cb, gp, gc], axis=1).reshape(8, n_small // 8)

    d_rep, m_rep, v_rep = _adamw(
        "adamw_small", g_rep.reshape(8, n_small // 8),
        pack(b_ada, norm1_g, norm2_g, final_g, pool_scale, conv_b, gnorm_pool_g, gnorm_conv_g),
        pack(m_b_ada, m_norm1_g, m_norm2_g, m_final_g, m_pool_scale, m_conv_b, m_gnorm_pool_g, m_gnorm_conv_g),
        pack(v_b_ada, v_norm1_g, v_norm2_g, v_final_g, v_pool_scale, v_conv_b, v_gnorm_pool_g, v_gnorm_conv_g))

    def unpack(flat):
        flat = flat.reshape(1, n_small)
        sizes = [N_MOD * d_model, d_model, d_model, d_model, cols, cols, cols, cols]
        parts, at = [], 0
        for size in sizes:
            parts.append(flat[:, at:at + size])
            at += size
        parts[3] = parts[3].reshape(d_model)
        return parts

    g_convw_full = sums[:, n_rep + d_model + 4 * cols:].reshape(3, cols)
    g_convw_mine = lax.dynamic_slice_in_dim(g_convw_full, chip * conv_shard, conv_shard, axis=1)
    d_convw, m_convw, v_convw = _adamw("adamw_conv_w", g_convw_mine, conv_w[0], m_conv_w[0], v_conv_w[0])

    dmod_cols = lax.dynamic_slice_in_dim(gathered[:, :N_MOD * d_model], chip * ada_cols, ada_cols, axis=1)
    g_ada, d_ada, mn_ada, vn_ada = _ada_backward(c_all.T, dmod_cols, w_ada[0], m_w_ada[0], v_w_ada[0])

    pm2d = (n_groups * shard_rows, group_dim)
    upd_in = _adamw("adamw_w_in", g_w_in, w_in[0], m_w_in[0], v_w_in[0])
    upd_pm = _adamw("adamw_pool_mix", g_pm, pool_mix_w[0].reshape(pm2d), m_pool_mix_w[0].reshape(pm2d),
                    v_pool_mix_w[0].reshape(pm2d))
    upd_out = _adamw("adamw_w_out", g_w_out, w_out[0], m_w_out[0], v_w_out[0])
    upd_1 = _adamw("adamw_w_mlp_in", g_w_1, w_mlp_in[0], m_w_mlp_in[0], v_w_mlp_in[0])
    upd_2 = _adamw("adamw_w_mlp_out", g_w_2, w_mlp_out[0], m_w_mlp_out[0], v_w_mlp_out[0])

    g_parts, d_parts, m_parts, v_parts = unpack(g_rep), unpack(d_rep), unpack(m_rep), unpack(v_rep)

    def ordered(ada, rep, w_in_, pm, convw, w_out_, w_1, w_2):
        b, n1, n2, fg, ps, cb, gp, gc = rep
        return [ada[None], b, n1, w_in_[None], pm.reshape(pool_mix_w.shape), ps, convw[None], cb, gp, gc,
                w_out_[None], n2, w_1[None], w_2[None], fg]

    grads = ordered(g_ada, g_parts, g_w_in, g_pm, g_convw_mine, g_w_out, g_w_1, g_w_2)
    deltas = ordered(d_ada, d_parts, upd_in[0], upd_pm[0], d_convw, upd_out[0], upd_1[0], upd_2[0])
    new_m = ordered(mn_ada, m_parts, upd_in[1], upd_pm[1], m_convw, upd_out[1], upd_1[1], upd_2[1])
    new_v = ordered(vn_ada, v_parts, upd_in[2], upd_pm[2], v_convw, upd_out[2], upd_1[2], upd_2[2])
    return (loss[0, 0], grad_x[None], *grads, *deltas, *new_m, *new_v)
```

```python
import jax
import jax.numpy as jnp
from jax import lax
from jax.experimental import pallas as pl
from jax.experimental.pallas import tpu as pltpu
from jax.experimental.pallas import tpu_sc as plsc

F32 = jnp.float32
BF16 = jnp.bfloat16
MESH = pl.DeviceIdType.MESH
ANY = pl.BlockSpec(memory_space=pl.ANY)

NORM_EPS = 1e-6
POOL_WINDOWS = (2, 4, 8, 16)
CONV_HEAD_DIM = 128
N_MOD = 6
N_CHIPS = 4
N_DEV = 8

ADAM_LR = 0.001
ADAM_B1 = 0.9
ADAM_B2 = 0.999
ADAM_EPS = 1e-08
ADAM_WD = 0.01
ADAM_STEP = 10

VMEM_LIMIT_BYTES = 56 * 1024 * 1024

NN = (((1,), (0,)), ((), ()))
NT = (((1,), (1,)), ((), ()))
TN = (((0,), (0,)), ((), ()))


def _tile(n, pref):
    t = min(n, pref)
    while n % t:
        t //= 2
    return t


def _params(*sem):
    return pltpu.CompilerParams(dimension_semantics=sem, vmem_limit_bytes=VMEM_LIMIT_BYTES)


def _position():
    return lax.axis_index("x"), lax.axis_index("y"), lax.axis_index("c")


def _flip(ix, iy, ic, mask):
    return (1 - ix if mask & 4 else ix, 1 - iy if mask & 2 else iy, 1 - ic if mask & 1 else ic)


def _allgather8(name, blk):
    rows, cols = blk.shape

    def body(x_ref, out_ref, send_sems, recv_sems, local_sem):
        ix, iy, ic = _position()
        me = 4 * ix + 2 * iy + ic
        mine = pltpu.make_async_copy(x_ref, out_ref.at[me], local_sem)
        mine.start()
        sends = []
        for mask in range(1, N_DEV):
            cp = pltpu.make_async_remote_copy(
                src_ref=x_ref, dst_ref=out_ref.at[me],
                send_sem=send_sems.at[mask - 1], recv_sem=recv_sems.at[mask - 1],
                device_id=_flip(ix, iy, ic, mask), device_id_type=MESH)
            cp.start()
            sends.append(cp)
        for mask in range(1, N_DEV):
            px, py, pc = _flip(ix, iy, ic, mask)
            pltpu.make_async_remote_copy(
                src_ref=x_ref, dst_ref=out_ref.at[4 * px + 2 * py + pc],
                send_sem=send_sems.at[mask - 1], recv_sem=recv_sems.at[mask - 1],
                device_id=(px, py, pc), device_id_type=MESH).wait_recv()
        for cp in sends:
            cp.wait_send()
        mine.wait()

    return pl.pallas_call(
        body, name=name,
        out_shape=jax.ShapeDtypeStruct((N_DEV, rows, cols), F32),
        in_specs=[pl.BlockSpec(memory_space=pltpu.VMEM)],
        out_specs=pl.BlockSpec(memory_space=pltpu.VMEM),
        scratch_shapes=[pltpu.SemaphoreType.DMA((N_DEV - 1,)), pltpu.SemaphoreType.DMA((N_DEV - 1,)),
                        pltpu.SemaphoreType.DMA],
    )(blk)


def _gather_flat(name, vec):
    n = vec.shape[0]
    npad = -(-n // 1024) * 1024
    blk = jnp.pad(vec, (0, npad - n)).reshape(8, npad // 8)
    return _allgather8(name, blk).reshape(N_DEV, npad)[:, :n]


def _chip_relations(ix, iy):
    return [(1 - ix, iy), (ix, 1 - iy), (1 - ix, 1 - iy)]


def _gather_weights(name, shards, collective_id):
    n = len(shards)

    def body(*refs):
        src, out = refs[:n], refs[n:2 * n]
        send_sems, recv_sems, local_sems = refs[2 * n:]
        ix, iy, ic = _position()
        chip = 2 * ix + iy
        sibling = (ix, iy, 1 - ic)
        rels = _chip_relations(ix, iy)

        _handshake([(px, py, ic) for px, py in rels] + [sibling])

        def half_rows(a, h):
            half = shards[a].shape[0] // 2
            return pl.ds(h * half, half)

        started = []
        for a in range(n):
            for r, (px, py) in enumerate(rels):
                cp = pltpu.make_async_remote_copy(
                    src_ref=src[a].at[half_rows(a, ic)], dst_ref=out[a].at[chip, half_rows(a, ic)],
                    send_sem=send_sems.at[6 * a + r], recv_sem=recv_sems.at[6 * a + r],
                    device_id=(px, py, ic), device_id_type=MESH)
                cp.start()
                started.append(cp)
        local = []
        for a in range(n):
            cp = pltpu.make_async_copy(src[a], out[a].at[chip], local_sems.at[a])
            cp.start()
            local.append(cp)
        for a in range(n):
            for r, (px, py) in enumerate(rels):
                landed = out[a].at[2 * px + py, half_rows(a, ic)]
                pltpu.make_async_remote_copy(
                    src_ref=landed, dst_ref=landed,
                    send_sem=send_sems.at[6 * a + r], recv_sem=recv_sems.at[6 * a + r],
                    device_id=(px, py, ic), device_id_type=MESH).wait_recv()
                cp = pltpu.make_async_remote_copy(
                    src_ref=landed, dst_ref=landed,
                    send_sem=send_sems.at[6 * a + 3 + r], recv_sem=recv_sems.at[6 * a + 3 + r],
                    device_id=sibling, device_id_type=MESH)
                cp.start()
                started.append(cp)
        for a in range(n):
            for r, (px, py) in enumerate(rels):
                passed = out[a].at[2 * px + py, half_rows(a, 1 - ic)]
                pltpu.make_async_remote_copy(
                    src_ref=passed, dst_ref=passed,
                    send_sem=send_sems.at[6 * a + 3 + r], recv_sem=recv_sems.at[6 * a + 3 + r],
                    device_id=sibling, device_id_type=MESH).wait_recv()
        for cp in started:
            cp.wait_send()
        for cp in local:
            cp.wait()

    out_type = [jax.ShapeDtypeStruct((N_CHIPS,) + s.shape, s.dtype) for s in shards]
    return _sequencer_call(name, body, shards, out_type, [6 * n, 6 * n, n], collective_id)


def _sequencer_call(name, body, operands, out_type, sem_counts, collective_id):
    return pl.kernel(
        body, name=name, out_type=out_type,
        mesh=plsc.ScalarSubcoreMesh(axis_name="sequencer", num_cores=1),
        scratch_types=[pltpu.SemaphoreType.DMA((n,)) for n in sem_counts],
        compiler_params=pltpu.CompilerParams(collective_id=collective_id),
    )(*operands)


def _handshake(peers):
    barrier = pltpu.get_barrier_semaphore()
    for peer in peers:
        pl.semaphore_signal(barrier, inc=1, device_id=peer, device_id_type=MESH)
    pl.semaphore_wait(barrier, len(peers))


def _exchange_halves(name, grads, collective_id):
    n = len(grads)

    def body(*refs):
        src, out = refs[:n], refs[n:2 * n]
        send_sems, recv_sems = refs[2 * n:]
        ix, iy, ic = _position()
        sibling = (ix, iy, 1 - ic)
        _handshake([sibling])
        copies = []
        for a in range(n):
            half = grads[a].shape[1] // 2
            cp = pltpu.make_async_remote_copy(
                src_ref=src[a].at[pl.ds(0, N_CHIPS), pl.ds((1 - ic) * half, half)], dst_ref=out[a],
                send_sem=send_sems.at[a], recv_sem=recv_sems.at[a],
                device_id=sibling, device_id_type=MESH)
            cp.start()
            copies.append(cp)
        for cp in copies:
            cp.wait()

    out_type = [jax.ShapeDtypeStruct((N_CHIPS, g.shape[1] // 2, g.shape[2]), g.dtype) for g in grads]
    return _sequencer_call(name, body, grads, out_type, [n, n], collective_id)


def _scatter_partials(name, parts, collective_id):
    n = len(parts)

    def body(*refs):
        src, out = refs[:n], refs[n:2 * n]
        send_sems, recv_sems = refs[2 * n:]
        ix, iy, ic = _position()
        rels = _chip_relations(ix, iy)
        _handshake([(px, py, ic) for px, py in rels])
        copies = []
        for a in range(n):
            for r, (px, py) in enumerate(rels):
                cp = pltpu.make_async_remote_copy(
                    src_ref=src[a].at[2 * px + py], dst_ref=out[a].at[r],
                    send_sem=send_sems.at[3 * a + r], recv_sem=recv_sems.at[3 * a + r],
                    device_id=(px, py, ic), device_id_type=MESH)
                cp.start()
                copies.append(cp)
        for cp in copies:
            cp.wait()

    out_type = [jax.ShapeDtypeStruct((3,) + p.shape[1:], p.dtype) for p in parts]
    return _sequencer_call(name, body, parts, out_type, [3 * n, 3 * n], collective_id)


def _swap_reduced(name, reduced, collective_id):
    n = len(reduced)

    def body(*refs):
        src, out = refs[:n], refs[n:2 * n]
        send_sems, recv_sems = refs[2 * n:]
        ix, iy, ic = _position()
        sibling = (ix, iy, 1 - ic)
        _handshake([sibling])
        copies = []
        for a in range(n):
            cp = pltpu.make_async_remote_copy(
                src_ref=src[a], dst_ref=out[a], send_sem=send_sems.at[a], recv_sem=recv_sems.at[a],
                device_id=sibling, device_id_type=MESH)
            cp.start()
            copies.append(cp)
        for cp in copies:
            cp.wait()

    out_type = [jax.ShapeDtypeStruct(r.shape, r.dtype) for r in reduced]
    return _sequencer_call(name, body, reduced, out_type, [n, n], collective_id)


def _add_half(name, grad, recv, core, after=()):
    _, rows, cols = grad.shape
    half = rows // 2
    tr, tc = _tile(half, 512), _tile(cols, 2048)
    nbr = half // tr

    def body(core_ref, g_ref, r_ref, *rest):
        rest[-1][...] = (g_ref[...].astype(F32) + r_ref[...].astype(F32)).astype(BF16)

    return pl.pallas_call(
        body, name=name,
        grid_spec=pltpu.PrefetchScalarGridSpec(
            num_scalar_prefetch=1, grid=(N_CHIPS, nbr, cols // tc),
            in_specs=[pl.BlockSpec((None, tr, tc), lambda s, i, j, core_ref: (s, core_ref[0] * nbr + i, j)),
                      pl.BlockSpec((None, tr, tc), lambda s, i, j, core_ref: (s, i, j))] + [ANY] * len(after),
            out_specs=pl.BlockSpec((None, tr, tc), lambda s, i, j, core_ref: (s, i, j))),
        out_shape=jax.ShapeDtypeStruct((N_CHIPS, half, cols), BF16),
        compiler_params=_params("parallel", "parallel", "parallel"),
    )(core, grad, recv, *after)


def _reduce_chips(name, part, recv, chip, after=()):
    _, half, cols = part.shape
    tr, tc = _tile(half, 512), _tile(cols, 2048)

    def body(chip_ref, p_ref, r_ref, *rest):
        o_ref = rest[-1]
        acc = p_ref[...].astype(F32)
        for r in range(3):
            acc = acc + r_ref[r].astype(F32)
        o_ref[...] = acc

    return pl.pallas_call(
        body, name=name,
        grid_spec=pltpu.PrefetchScalarGridSpec(
            num_scalar_prefetch=1, grid=(half // tr, cols // tc),
            in_specs=[pl.BlockSpec((None, tr, tc), lambda i, j, chip_ref: (chip_ref[0], i, j)),
                      pl.BlockSpec((3, tr, tc), lambda i, j, chip_ref: (0, i, j))] + [ANY] * len(after),
            out_specs=pl.BlockSpec((tr, tc), lambda i, j, chip_ref: (i, j))),
        out_shape=jax.ShapeDtypeStruct((half, cols), F32),
        compiler_params=_params("parallel", "parallel"),
    )(chip, part, recv, *after)


def _adamw_half(name, g_half, w, m, v, which, done=None, after=()):
    half, cols = g_half.shape
    tr, tc = _tile(half, 256), _tile(cols, 2048)
    nbr = half // tr
    bc1 = 1.0 - ADAM_B1 ** ADAM_STEP
    bc2 = 1.0 - ADAM_B2 ** ADAM_STEP

    def body(which_ref, g_ref, w_ref, m_ref, v_ref, *rest):
        go_ref, d_ref, mo_ref, vo_ref = rest[-4:]
        gv = g_ref[...]
        mn = ADAM_B1 * m_ref[...] + (1.0 - ADAM_B1) * gv
        vn = ADAM_B2 * v_ref[...] + (1.0 - ADAM_B2) * (gv * gv)
        go_ref[...] = gv
        d_ref[...] = -ADAM_LR * ((mn / bc1) / (jnp.sqrt(vn / bc2) + ADAM_EPS) + ADAM_WD * w_ref[...])
        mo_ref[...] = mn
        vo_ref[...] = vn

    mine = pl.BlockSpec((tr, tc), lambda i, j, which_ref: (which_ref[0] * nbr + i, j))
    kept = [] if done is None else list(done)
    shape = jax.ShapeDtypeStruct((2 * half, cols), F32)
    return pl.pallas_call(
        body, name=name,
        grid_spec=pltpu.PrefetchScalarGridSpec(
            num_scalar_prefetch=1, grid=(nbr, cols // tc),
            in_specs=([pl.BlockSpec((tr, tc), lambda i, j, which_ref: (i, j)), mine, mine, mine]
                      + [ANY] * (len(kept) + len(after))),
            out_specs=[mine] * 4),
        out_shape=[shape] * 4,
        input_output_aliases={5 + k: k for k in range(len(kept))},
        compiler_params=_params("parallel", "parallel"),
    )(which, g_half, w, m, v, *kept, *after)


def _adamw(name, g, w, m, v):
    rows, cols = g.shape
    tr, tc = _tile(rows, 256), _tile(cols, 2048)
    bc1 = 1.0 - ADAM_B1 ** ADAM_STEP
    bc2 = 1.0 - ADAM_B2 ** ADAM_STEP

    def body(g_ref, w_ref, m_ref, v_ref, d_ref, mo_ref, vo_ref):
        gv = g_ref[...]
        mn = ADAM_B1 * m_ref[...] + (1.0 - ADAM_B1) * gv
        vn = ADAM_B2 * v_ref[...] + (1.0 - ADAM_B2) * (gv * gv)
        d_ref[...] = -ADAM_LR * ((mn / bc1) / (jnp.sqrt(vn / bc2) + ADAM_EPS) + ADAM_WD * w_ref[...])
        mo_ref[...] = mn
        vo_ref[...] = vn

    spec = pl.BlockSpec((tr, tc), lambda i, j: (i, j))
    shape = jax.ShapeDtypeStruct((rows, cols), F32)
    return pl.pallas_call(
        body, name=name, grid=(rows // tr, cols // tc),
        in_specs=[spec] * 4, out_specs=[spec] * 3, out_shape=[shape] * 3,
        compiler_params=_params("parallel", "parallel"),
    )(g, w, m, v)


def _reduce_small(gathered, d_model):
    n = gathered.shape[1]
    loss_at = (N_MOD + 3) * d_model

    def body(g_ref, s_ref, loss_ref):
        acc = g_ref[0:1, :]
        for d in range(1, N_DEV):
            acc = acc + g_ref[d:d + 1, :]
        s_ref[...] = acc
        lanes = acc[:, loss_at:loss_at + d_model]
        loss_ref[...] = jnp.broadcast_to((0.5 / d_model) * jnp.sum(lanes, axis=1, keepdims=True), loss_ref.shape)

    return pl.pallas_call(
        body, name="reduce_small",
        out_shape=[jax.ShapeDtypeStruct((1, n), F32), jax.ShapeDtypeStruct((1, 128), F32)],
        compiler_params=pltpu.CompilerParams(vmem_limit_bytes=VMEM_LIMIT_BYTES),
    )(gathered)


def _ada_forward(c_all, w_ada, b_cols):
    d_model, width = w_ada.shape
    tn = _tile(width, 512)

    def body(c_ref, w_ref, b_ref, o_ref):
        cv = c_ref[...]
        act = cv * jax.nn.sigmoid(cv)
        o_ref[...] = lax.dot_general(act, w_ref[...], NN, precision=lax.Precision.HIGHEST,
                                     preferred_element_type=F32) + b_ref[...]

    return pl.pallas_call(
        body, name="ada_forward", grid=(width // tn,),
        in_specs=[pl.BlockSpec((N_DEV, d_model), lambda j: (0, 0)),
                  pl.BlockSpec((d_model, tn), lambda j: (0, j)),
                  pl.BlockSpec((1, tn), lambda j: (0, j))],
        out_specs=pl.BlockSpec((N_DEV, tn), lambda j: (0, j)),
        out_shape=jax.ShapeDtypeStruct((N_DEV, width), F32),
        compiler_params=_params("parallel"),
    )(c_all, w_ada, b_cols)


def _ada_backward(c_all_t, dmod_cols, w, m, v):
    d_model, width = w.shape
    tr, tc = _tile(d_model, 256), _tile(width, 1536)
    bc1 = 1.0 - ADAM_B1 ** ADAM_STEP
    bc2 = 1.0 - ADAM_B2 ** ADAM_STEP

    def body(c_ref, dm_ref, w_ref, m_ref, v_ref, g_ref, d_ref, mo_ref, vo_ref):
        cv = c_ref[...]
        act = cv * jax.nn.sigmoid(cv)
        gv = lax.dot_general(act, dm_ref[...], NN, precision=lax.Precision.HIGHEST, preferred_element_type=F32)
        mn = ADAM_B1 * m_ref[...] + (1.0 - ADAM_B1) * gv
        vn = ADAM_B2 * v_ref[...] + (1.0 - ADAM_B2) * (gv * gv)
        g_ref[...] = gv
        d_ref[...] = -ADAM_LR * ((mn / bc1) / (jnp.sqrt(vn / bc2) + ADAM_EPS) + ADAM_WD * w_ref[...])
        mo_ref[...] = mn
        vo_ref[...] = vn

    spec = pl.BlockSpec((tr, tc), lambda i, j: (i, j))
    shape = jax.ShapeDtypeStruct((d_model, width), F32)
    return pl.pallas_call(
        body, name="ada_backward", grid=(d_model // tr, width // tc),
        in_specs=[pl.BlockSpec((tr, N_DEV), lambda i, j: (i, 0)),
                  pl.BlockSpec((N_DEV, tc), lambda i, j: (0, j)), spec, spec, spec],
        out_specs=[spec] * 4, out_shape=[shape] * 4,
        compiler_params=_params("parallel", "parallel"),
    )(c_all_t, dmod_cols, w, m, v)


def _matmul(name, a, b, extras, *, grid, tiles, dims, a_spec, b_spec, extra_specs, out_shape, out_specs,
            epilogue, prologue=None, after=()):
    tm, tn, _ = tiles
    gm, gn, gk = grid
    n_extra, n_out = len(extras), len(out_shape)
    first_out = 2 + n_extra + len(after)

    def body(*refs):
        a_ref, b_ref = refs[0], refs[1]
        extra_refs = refs[2:2 + n_extra]
        out_refs = refs[first_out:first_out + n_out]
        acc_ref = refs[-1]
        k = pl.program_id(2)

        @pl.when(k == 0)
        def _():
            acc_ref[...] = jnp.zeros_like(acc_ref)

        av = a_ref[...]
        if prologue is not None:
            av = prologue(av)
        acc_ref[...] += lax.dot_general(av, b_ref[...], dims, preferred_element_type=F32)

        @pl.when(k == gk - 1)
        def _():
            epilogue(acc_ref[...], extra_refs, out_refs)

    return pl.pallas_call(
        body, name=name, grid=(gm, gn, gk),
        in_specs=[a_spec, b_spec, *extra_specs] + [ANY] * len(after), out_specs=out_specs, out_shape=out_shape,
        scratch_shapes=[pltpu.VMEM((tm, tn), F32)],
        compiler_params=_params("parallel", "parallel", "arbitrary"),
    )(a, b, *extras, *after)


def _store(dtype):
    def epilogue(acc, extra_refs, out_refs):
        out_refs[0][...] = acc.astype(dtype)
    return epilogue


def _residual_epilogue(acc, extra_refs, out_refs):
    res_ref, gate_ref = extra_refs
    out_refs[0][...] = res_ref[...] + gate_ref[...] * acc
    out_refs[1][...] = acc.astype(BF16)


def _square(av):
    af = av.astype(F32)
    return (af * af).astype(BF16)


MM_TILE_M = 1024
MM_TILE_N = 1024
MM_TILE_K = 1024


def _in_projection(h, w_slabs):
    seq, d_model = h.shape
    _, _, cols = w_slabs.shape
    tm, tn, tk = _tile(seq, MM_TILE_M), _tile(cols, MM_TILE_N), _tile(d_model, MM_TILE_K)
    nbj = cols // tn
    return _matmul(
        "in_projection", h, w_slabs, (), grid=(seq // tm, N_CHIPS * nbj, d_model // tk), tiles=(tm, tn, tk), dims=NN,
        a_spec=pl.BlockSpec((tm, tk), lambda i, j, k: (i, k)),
        b_spec=pl.BlockSpec((None, tk, tn), lambda i, j, k: (j // nbj, k, j % nbj)),
        extra_specs=(),
        out_shape=[jax.ShapeDtypeStruct((N_CHIPS, seq, cols), BF16)],
        out_specs=[pl.BlockSpec((None, tm, tn), lambda i, j, k: (j // nbj, i, j % nbj))],
        epilogue=_store(BF16))[0]


def _residual_projection(name, a, w, res, gate, prologue=None):
    seq, kdim = a.shape
    d_model = w.shape[1]
    tm, tn, tk = _tile(seq, MM_TILE_M), _tile(d_model, MM_TILE_N), _tile(kdim, MM_TILE_K)
    tile = pl.BlockSpec((tm, tn), lambda i, j, k: (i, j))
    return _matmul(
        name, a, w, (res, gate), grid=(seq // tm, d_model // tn, kdim // tk), tiles=(tm, tn, tk), dims=NN,
        a_spec=pl.BlockSpec((tm, tk), lambda i, j, k: (i, k)),
        b_spec=pl.BlockSpec((tk, tn), lambda i, j, k: (k, j)),
        extra_specs=(tile, pl.BlockSpec((1, tn), lambda i, j, k: (0, j))),
        out_shape=[jax.ShapeDtypeStruct((seq, d_model), F32), jax.ShapeDtypeStruct((seq, d_model), BF16)],
        out_specs=[tile, tile],
        epilogue=_residual_epilogue, prologue=prologue)


def _mlp_in(h, w_slabs):
    seq, d_model = h.shape
    _, _, cols = w_slabs.shape
    tm, tn, tk = _tile(seq, MM_TILE_M), _tile(cols, MM_TILE_N), _tile(d_model, MM_TILE_K)
    nbj = cols // tn

    def epilogue(acc, extra_refs, out_refs):
        out_refs[0][...] = jnp.maximum(acc, 0.0).astype(BF16)

    return _matmul(
        "mlp_in", h, w_slabs, (), grid=(seq // tm, N_CHIPS * nbj, d_model // tk), tiles=(tm, tn, tk), dims=NN,
        a_spec=pl.BlockSpec((tm, tk), lambda i, j, k: (i, k)),
        b_spec=pl.BlockSpec((None, tk, tn), lambda i, j, k: (j // nbj, k, j % nbj)),
        extra_specs=(),
        out_shape=[jax.ShapeDtypeStruct((seq, N_CHIPS * cols), BF16)],
        out_specs=[pl.BlockSpec((tm, tn), lambda i, j, k: (i, j))],
        epilogue=epilogue)[0]


def _grad_hidden(dmlp, w2, act):
    seq, d_model = dmlp.shape
    ff = w2.shape[0]
    tm, tn, tk = _tile(seq, MM_TILE_M), _tile(ff, MM_TILE_N), _tile(d_model, MM_TILE_K)

    def epilogue(acc, extra_refs, out_refs):
        out_refs[0][...] = (acc * (2.0 * extra_refs[0][...].astype(F32))).astype(BF16)

    tile = pl.BlockSpec((tm, tn), lambda i, j, k: (i, j))
    return _matmul(
        "grad_hidden", dmlp, w2, (act,), grid=(seq // tm, ff // tn, d_model // tk), tiles=(tm, tn, tk), dims=NT,
        a_spec=pl.BlockSpec((tm, tk), lambda i, j, k: (i, k)),
        b_spec=pl.BlockSpec((tn, tk), lambda i, j, k: (j, k)),
        extra_specs=(tile,),
        out_shape=[jax.ShapeDtypeStruct((seq, ff), BF16)], out_specs=[tile],
        epilogue=epilogue)[0]


def _weight_grad(name, a, b, prologue=None):
    seq, m = a.shape
    n = b.shape[1]
    tm, tn, tk = _tile(m, MM_TILE_M), _tile(n, MM_TILE_N), _tile(seq, MM_TILE_K)
    return _matmul(
        name, a, b, (), grid=(m // tm, n // tn, seq // tk), tiles=(tm, tn, tk), dims=TN,
        a_spec=pl.BlockSpec((tk, tm), lambda i, j, k: (k, i)),
        b_spec=pl.BlockSpec((tk, tn), lambda i, j, k: (k, j)),
        extra_specs=(),
        out_shape=[jax.ShapeDtypeStruct((m, n), BF16)],
        out_specs=[pl.BlockSpec((tm, tn), lambda i, j, k: (i, j))],
        epilogue=_store(BF16), prologue=prologue)[0]


def _weight_grad_slabs(name, a, b, slab_cols, after=()):
    seq, m = a.shape
    cols = b.shape[2] if slab_cols is None else slab_cols
    tm, tn, tk = _tile(m, MM_TILE_M), _tile(cols, MM_TILE_N), _tile(seq, MM_TILE_K)
    nbj = cols // tn
    if slab_cols is None:
        b_spec = pl.BlockSpec((None, tk, tn), lambda i, j, k: (j // nbj, k, j % nbj))
    else:
        b_spec = pl.BlockSpec((tk, tn), lambda i, j, k: (k, j))
    return _matmul(
        name, a, b, (), grid=(m // tm, N_CHIPS * nbj, seq // tk), tiles=(tm, tn, tk), dims=TN,
        a_spec=pl.BlockSpec((tk, tm), lambda i, j, k: (k, i)),
        b_spec=b_spec, extra_specs=(),
        out_shape=[jax.ShapeDtypeStruct((N_CHIPS, m, cols), BF16)],
        out_specs=[pl.BlockSpec((None, tm, tn), lambda i, j, k: (j // nbj, i, j % nbj))],
        epilogue=_store(BF16), after=after)[0]


def _grad_input_slabs(name, dy, w_slabs, after=()):
    _, d_model, cols = w_slabs.shape
    seq = dy.shape[1] if dy.ndim == 3 else dy.shape[0]
    tm, tn, tk = _tile(seq, MM_TILE_M), _tile(d_model, MM_TILE_N), _tile(cols, MM_TILE_K)
    nbk = cols // tk
    if dy.ndim == 3:
        a_spec = pl.BlockSpec((None, tm, tk), lambda i, j, k: (k // nbk, i, k % nbk))
    else:
        a_spec = pl.BlockSpec((tm, tk), lambda i, j, k: (i, k))
    return _matmul(
        name, dy, w_slabs, (), grid=(seq // tm, d_model // tn, N_CHIPS * nbk), tiles=(tm, tn, tk), dims=NT,
        a_spec=a_spec,
        b_spec=pl.BlockSpec((None, tn, tk), lambda i, j, k: (k // nbk, j, k % nbk)),
        extra_specs=(),
        out_shape=[jax.ShapeDtypeStruct((seq, d_model), F32)],
        out_specs=[pl.BlockSpec((tm, tn), lambda i, j, k: (i, j))],
        epilogue=_store(F32), after=after)[0]


def _grad_input(name, dy, w):
    seq, n = dy.shape
    kdim = w.shape[0]
    tm, tn, tk = _tile(seq, MM_TILE_M), _tile(kdim, MM_TILE_N), _tile(n, MM_TILE_K)
    return _matmul(
        name, dy, w, (), grid=(seq // tm, kdim // tn, n // tk), tiles=(tm, tn, tk), dims=NT,
        a_spec=pl.BlockSpec((tm, tk), lambda i, j, k: (i, k)),
        b_spec=pl.BlockSpec((tn, tk), lambda i, j, k: (j, k)),
        extra_specs=(),
        out_shape=[jax.ShapeDtypeStruct((seq, kdim), F32)],
        out_specs=[pl.BlockSpec((tm, tn), lambda i, j, k: (i, j))],
        epilogue=_store(F32))[0]


ROW_TILE = 128


def _norm_modulate(name, xin, g, scale, shift):
    seq, d_model = xin.shape
    tr = _tile(seq, ROW_TILE)

    def body(x_ref, g_ref, sc_ref, sh_ref, h_ref):
        xv = x_ref[...]
        r = lax.rsqrt(jnp.mean(xv * xv, axis=-1, keepdims=True) + NORM_EPS)
        h_ref[...] = (((xv * r) * g_ref[...]) * (1.0 + sc_ref[...]) + sh_ref[...]).astype(BF16)

    row = pl.BlockSpec((tr, d_model), lambda i: (i, 0))
    vec = pl.BlockSpec((1, d_model), lambda i: (0, 0))
    return pl.pallas_call(
        body, name=name, grid=(seq // tr,),
        in_specs=[row, vec, vec, vec], out_specs=row,
        out_shape=jax.ShapeDtypeStruct((seq, d_model), BF16),
        compiler_params=_params("parallel"),
    )(xin, g, scale, shift)


def _loss_head(x2, target, final_g, mlp, gate2):
    seq, d_model = x2.shape
    tr = _tile(seq, ROW_TILE)

    def body(x_ref, t_ref, fg_ref, mlp_ref, gate_ref, dx_ref, dmlp_ref, gfg_ref, dgate_ref, sq_ref):
        @pl.when(pl.program_id(0) == 0)
        def _():
            gfg_ref[...] = jnp.zeros_like(gfg_ref)
            dgate_ref[...] = jnp.zeros_like(dgate_ref)
            sq_ref[...] = jnp.zeros_like(sq_ref)

        xv = x_ref[...]
        fg = fg_ref[...]
        r = lax.rsqrt(jnp.mean(xv * xv, axis=-1, keepdims=True) + NORM_EPS)
        n = xv * r
        err = n * fg - t_ref[...]
        sq_ref[...] += jnp.sum(err * err, axis=0, keepdims=True)
        dy = err * (1.0 / d_model)
        gfg_ref[...] += jnp.sum(dy * n, axis=0, keepdims=True)
        dn = dy * fg
        dx = r * (dn - n * jnp.mean(dn * n, axis=-1, keepdims=True))
        dx_ref[...] = dx
        dgate_ref[...] += jnp.sum(dx * mlp_ref[...].astype(F32), axis=0, keepdims=True)
        dmlp_ref[...] = (dx * gate_ref[...]).astype(BF16)

    row = pl.BlockSpec((tr, d_model), lambda i: (i, 0))
    vec = pl.BlockSpec((1, d_model), lambda i: (0, 0))
    vshape = jax.ShapeDtypeStruct((1, d_model), F32)
    return pl.pallas_call(
        body, name="loss_head", grid=(seq // tr,),
        in_specs=[row, row, vec, row, vec], out_specs=[row, row, vec, vec, vec],
        out_shape=[jax.ShapeDtypeStruct((seq, d_model), F32), jax.ShapeDtypeStruct((seq, d_model), BF16),
                   vshape, vshape, vshape],
        compiler_params=_params("arbitrary"),
    )(x2, target, final_g, mlp, gate2)


def _norm_modulate_backward(name, dh, xin, g, scale, dres, branch=None, gate=None, after=()):
    seq, d_model = xin.shape
    tr = _tile(seq, ROW_TILE)
    with_branch = branch is not None
    n_in = (7 if with_branch else 5) + len(after)

    def body(*refs):
        dh_ref, x_ref, g_ref, sc_ref, dres_ref = refs[:5]
        outs = refs[n_in:]
        dx_ref, dsc_ref, dsh_ref, dg_ref = outs[:4]

        @pl.when(pl.program_id(0) == 0)
        def _():
            for ref in outs[1:5] if with_branch else outs[1:4]:
                ref[...] = jnp.zeros_like(ref)

        xv = x_ref[...]
        gv = g_ref[...]
        dhv = dh_ref[...]
        r = lax.rsqrt(jnp.mean(xv * xv, axis=-1, keepdims=True) + NORM_EPS)
        xn = xv * r
        dsh_ref[...] += jnp.sum(dhv, axis=0, keepdims=True)
        dsc_ref[...] += jnp.sum(dhv * (xn * gv), axis=0, keepdims=True)
        t = dhv * (1.0 + sc_ref[...])
        dg_ref[...] += jnp.sum(t * xn, axis=0, keepdims=True)
        dxn = t * gv
        dx = dres_ref[...] + r * (dxn - xn * jnp.mean(dxn * xn, axis=-1, keepdims=True))
        dx_ref[...] = dx
        if with_branch:
            br_ref, gate_ref = refs[5:7]
            dgate_ref, dbr_ref = outs[4:6]
            dgate_ref[...] += jnp.sum(dx * br_ref[...].astype(F32), axis=0, keepdims=True)
            dbr_ref[...] = (dx * gate_ref[...]).astype(BF16)

    row = pl.BlockSpec((tr, d_model), lambda i: (i, 0))
    vec = pl.BlockSpec((1, d_model), lambda i: (0, 0))
    vshape = jax.ShapeDtypeStruct((1, d_model), F32)
    in_specs = [row, row, vec, vec, row]
    out_specs = [row, vec, vec, vec]
    out_shape = [jax.ShapeDtypeStruct((seq, d_model), F32), vshape, vshape, vshape]
    args = [dh, xin, g, scale, dres]
    if with_branch:
        in_specs += [row, vec]
        out_specs += [vec, row]
        out_shape += [vshape, jax.ShapeDtypeStruct((seq, d_model), BF16)]
        args += [branch, gate]
    in_specs += [ANY] * len(after)
    args += list(after)
    return pl.pallas_call(
        body, name=name, grid=(seq // tr,),
        in_specs=in_specs, out_specs=out_specs, out_shape=out_shape,
        compiler_params=_params("arbitrary"),
    )(*args)


def _shifted(v, k, t):
    seq = v.shape[0]
    if k == 0:
        return v
    moved = pltpu.roll(v, (-k) % seq, 0)
    return jnp.where((t + k >= 0) & (t + k < seq), moved, 0.0)


def _window_sum(v, offsets, t):
    acc = None
    for k in offsets:
        term = _shifted(v, k, t)
        acc = term if acc is None else acc + term
    return acc


def _window_count(seq, half):
    t = lax.broadcasted_iota(jnp.int32, (seq, 1), 0)
    return (jnp.minimum(t + half, seq) - jnp.maximum(t - half, 0)).astype(F32)


def _pool_forward(proj, group_dim):
    _, seq, cols = proj.shape
    tl = _tile(group_dim, 256)
    nbl = group_dim // tl
    n_groups = cols // group_dim

    def body(v_ref, o_ref):
        g = pl.program_id(0)
        for gi, window in enumerate(POOL_WINDOWS[:n_groups]):
            @pl.when(g == gi)
            def _(window=window):
                half = window // 2
                v = v_ref[...].astype(F32)
                t = lax.broadcasted_iota(jnp.int32, v.shape, 0)
                total = _window_sum(v, range(-half, half), t)
                o_ref[...] = (total / _window_count(seq, half) - v).astype(BF16)

    return pl.pallas_call(
        body, name="pool_forward", grid=(n_groups, nbl),
        in_specs=[pl.BlockSpec((None, seq, tl), lambda g, j: (0, 0, g * nbl + j))],
        out_specs=pl.BlockSpec((seq, tl), lambda g, j: (0, g * nbl + j)),
        out_shape=jax.ShapeDtypeStruct((seq, cols), BF16),
        compiler_params=_params("parallel", "parallel"),
    )(proj)


def _group_matrix(w_ref):
    return jnp.concatenate([w_ref[r] for r in range(N_CHIPS)], axis=0)


def _pool_mix_forward(pooled, w_pm, pool_scale, gnorm_g, d_model):
    seq, cols = pooled.shape
    _, n_groups, shard_rows, group_dim = w_pm.shape
    tm = _tile(seq, 512)

    def body(p_ref, w_ref, ps_ref, g_ref, o_ref):
        a = jnp.dot(p_ref[...], _group_matrix(w_ref), preferred_element_type=F32) * ps_ref[...]
        ra = lax.rsqrt(jnp.mean(a * a, axis=-1, keepdims=True) + NORM_EPS)
        o_ref[...] = ((a * ra) * g_ref[...]).astype(BF16)

    tile = pl.BlockSpec((tm, group_dim), lambda g, i: (i, g))
    vec = pl.BlockSpec((1, group_dim), lambda g, i: (0, g))
    return pl.pallas_call(
        body, name="pool_mix_forward", grid=(n_groups, seq // tm),
        in_specs=[tile, pl.BlockSpec((N_CHIPS, None, shard_rows, group_dim), lambda g, i: (0, g, 0, 0)), vec, vec],
        out_specs=tile,
        out_shape=jax.ShapeDtypeStruct((seq, d_model), BF16),
        compiler_params=_params("parallel", "parallel"),
    )(pooled, w_pm, pool_scale, gnorm_g)


def _conv_parts(b_ref, c_ref, u_ref, w_ref, bias_ref):
    bv = b_ref[...].astype(F32)
    cu = c_ref[...].astype(F32) * u_ref[...].astype(F32)
    t = lax.broadcasted_iota(jnp.int32, cu.shape, 0)
    prev, nxt = _shifted(cu, -1, t), _shifted(cu, 1, t)
    w = w_ref[...]
    conv = w[0:1] * prev + w[1:2] * cu + w[2:3] * nxt + bias_ref[...]
    return bv, cu, prev, nxt, conv, w, t


def _conv_forward(proj, conv_w, conv_b, gnorm_g, mixed):
    _, seq, cols = proj.shape
    tl = CONV_HEAD_DIM
    first = cols // tl

    def body(b_ref, c_ref, u_ref, w_ref, bias_ref, g_ref, mixed_ref, o_ref):
        bv, _, _, _, conv, _, _ = _conv_parts(b_ref, c_ref, u_ref, w_ref, bias_ref)
        bo = bv * conv
        rb = lax.rsqrt(jnp.mean(bo * bo, axis=-1, keepdims=True) + NORM_EPS)
        o_ref[...] = ((bo * rb) * g_ref[...]).astype(BF16)

    def slab(s):
        return pl.BlockSpec((None, seq, tl), lambda j, s=s: (s, 0, j))

    vec = pl.BlockSpec((1, tl), lambda j: (0, j))
    return pl.pallas_call(
        body, name="conv_forward", grid=(cols // tl,),
        in_specs=[slab(1), slab(2), slab(3), pl.BlockSpec((3, tl), lambda j: (0, j)), vec, vec, ANY],
        out_specs=pl.BlockSpec((seq, tl), lambda j: (0, first + j)),
        out_shape=jax.ShapeDtypeStruct(mixed.shape, mixed.dtype),
        input_output_aliases={6: 0},
        compiler_params=_params("parallel"),
    )(proj, proj, proj, conv_w, conv_b, gnorm_g, mixed)


def _pool_mix_backward(dmixed, pooled, w_pm, pool_scale, gnorm_g, after=()):
    seq, cols = pooled.shape
    _, n_groups, shard_rows, group_dim = w_pm.shape
    tm = _tile(seq, 512)

    def body(dm_ref, p_ref, w_ref, ps_ref, g_ref, *rest):
        dp_ref, dpm_ref, gg_ref, gps_ref = rest[-4:]

        @pl.when(pl.program_id(1) == 0)
        def _():
            gg_ref[...] = jnp.zeros_like(gg_ref)
            gps_ref[...] = jnp.zeros_like(gps_ref)

        w = _group_matrix(w_ref)
        ps = ps_ref[...]
        a_pre = jnp.dot(p_ref[...], w, preferred_element_type=F32)
        a = a_pre * ps
        ra = lax.rsqrt(jnp.mean(a * a, axis=-1, keepdims=True) + NORM_EPS)
        an = a * ra
        dm = dm_ref[...]
        gg_ref[...] += jnp.sum(dm * an, axis=0, keepdims=True)
        dan = dm * g_ref[...]
        da = ra * (dan - an * jnp.mean(dan * an, axis=-1, keepdims=True))
        gps_ref[...] += jnp.sum(da * a_pre, axis=0, keepdims=True)
        dpm = (da * ps).astype(BF16)
        dpm_ref[...] = dpm
        dp_ref[...] = lax.dot_general(dpm, w, NT, preferred_element_type=F32)

    tile = pl.BlockSpec((tm, group_dim), lambda g, i: (i, g))
    vec = pl.BlockSpec((1, group_dim), lambda g, i: (0, g))
    vshape = jax.ShapeDtypeStruct((1, cols), F32)
    return pl.pallas_call(
        body, name="pool_mix_backward", grid=(n_groups, seq // tm),
        in_specs=[tile, tile, pl.BlockSpec((N_CHIPS, None, shard_rows, group_dim), lambda g, i: (0, g, 0, 0)),
                  vec, vec] + [ANY] * len(after),
        out_specs=[tile, tile, vec, vec],
        out_shape=[jax.ShapeDtypeStruct((seq, cols), F32), jax.ShapeDtypeStruct((seq, cols), BF16), vshape, vshape],
        compiler_params=_params("parallel", "arbitrary"),
    )(dmixed, pooled, w_pm, pool_scale, gnorm_g, *after)


def _pool_mix_weight_grad(pooled, dpm, n_groups):
    seq, cols = pooled.shape
    group_dim = cols // n_groups
    shard_rows = group_dim // N_CHIPS
    tk = _tile(seq, 1024)
    gk = seq // tk

    def body(p_ref, d_ref, o_ref, acc_ref):
        k = pl.program_id(1)

        @pl.when(k == 0)
        def _():
            acc_ref[...] = jnp.zeros_like(acc_ref)

        acc_ref[...] += lax.dot_general(p_ref[...], d_ref[...], TN, preferred_element_type=F32)

        @pl.when(k == gk - 1)
        def _():
            for r in range(N_CHIPS):
                o_ref[r] = acc_ref[r * shard_rows:(r + 1) * shard_rows, :].astype(BF16)

    tile = pl.BlockSpec((tk, group_dim), lambda g, k: (k, g))
    return pl.pallas_call(
        body, name="pool_mix_weight_grad", grid=(n_groups, gk),
        in_specs=[tile, tile],
        out_specs=pl.BlockSpec((N_CHIPS, None, shard_rows, group_dim), lambda g, k: (0, g, 0, 0)),
        out_shape=jax.ShapeDtypeStruct((N_CHIPS, n_groups, shard_rows, group_dim), BF16),
        scratch_shapes=[pltpu.VMEM((group_dim, group_dim), F32)],
        compiler_params=_params("parallel", "arbitrary"),
    )(pooled, dpm)


def _mixers_backward(dpooled, dmixed, proj, conv_w, conv_b, gnorm_g, group_dim):
    _, seq, cols = proj.shape
    tl = CONV_HEAD_DIM
    first = cols // tl
    per_group = group_dim // tl
    n_groups = cols // group_dim

    def body(dp_ref, dm_ref, b_ref, c_ref, u_ref, w_ref, bias_ref, g_ref, o_ref, gg_ref, gb_ref, gw_ref):
        j = pl.program_id(0)
        for gi, window in enumerate(POOL_WINDOWS[:n_groups]):
            @pl.when(j // per_group == gi)
            def _(window=window):
                half = window // 2
                dp = dp_ref[...]
                t = lax.broadcasted_iota(jnp.int32, dp.shape, 0)
                dq = dp / _window_count(seq, half)
                o_ref[0] = (_window_sum(dq, range(-half + 1, half + 1), t) - dp).astype(BF16)

        bv, cu, prev, nxt, conv, w, t = _conv_parts(b_ref, c_ref, u_ref, w_ref, bias_ref)
        bo = bv * conv
        rb = lax.rsqrt(jnp.mean(bo * bo, axis=-1, keepdims=True) + NORM_EPS)
        bn = bo * rb
        dm = dm_ref[...]
        gg_ref[...] = jnp.sum(dm * bn, axis=0, keepdims=True)
        dbn = dm * g_ref[...]
        dbo = rb * (dbn - bn * jnp.mean(dbn * bn, axis=-1, keepdims=True))
        o_ref[1] = (dbo * conv).astype(BF16)
        dconv = dbo * bv
        gb_ref[...] = jnp.sum(dconv, axis=0, keepdims=True)
        gw_ref[0:1, :] = jnp.sum(dconv * prev, axis=0, keepdims=True)
        gw_ref[1:2, :] = jnp.sum(dconv * cu, axis=0, keepdims=True)
        gw_ref[2:3, :] = jnp.sum(dconv * nxt, axis=0, keepdims=True)
        dcu = w[0:1] * _shifted(dconv, 1, t) + w[1:2] * dconv + w[2:3] * _shifted(dconv, -1, t)
        o_ref[2] = (dcu * u_ref[...].astype(F32)).astype(BF16)
        o_ref[3] = (dcu * c_ref[...].astype(F32)).astype(BF16)

    def slab(s):
        return pl.BlockSpec((None, seq, tl), lambda j, s=s: (s, 0, j))

    vec = pl.BlockSpec((1, tl), lambda j: (0, j))
    rows3 = pl.BlockSpec((3, tl), lambda j: (0, j))
    vshape = jax.ShapeDtypeStruct((1, cols), F32)
    return pl.pallas_call(
        body, name="mixers_backward", grid=(cols // tl,),
        in_specs=[pl.BlockSpec((seq, tl), lambda j: (0, j)), pl.BlockSpec((seq, tl), lambda j: (0, first + j)),
                  slab(1), slab(2), slab(3), rows3, vec, vec],
        out_specs=[pl.BlockSpec((N_CHIPS, seq, tl), lambda j: (0, 0, j)), vec, vec, rows3],
        out_shape=[jax.ShapeDtypeStruct((N_CHIPS, seq, cols), BF16), vshape, vshape,
                   jax.ShapeDtypeStruct((3, cols), F32)],
        compiler_params=_params("parallel"),
    )(dpooled, dmixed, proj, proj, proj, conv_w, conv_b, gnorm_g)


class _GradReduction:
    def __init__(self, tag, grads, states, pair_id, scatter_id, position):
        self.tag, self.grads, self.states = tag, grads, states
        self.pair_id, self.scatter_id = pair_id, scatter_id
        self.chip, self.core, self.other_core = position

    def exchange(self):
        self.received = _exchange_halves(f"exchange_{self.tag}", self.grads, self.pair_id)

    def combine(self, after=()):
        self.parts = [_add_half(f"add_half_{self.tag}_{a}", g, r, self.core, after)
                      for a, (g, r) in enumerate(zip(self.grads, self.received))]

    def scatter(self):
        self.landed = _scatter_partials(f"scatter_{self.tag}", self.parts, self.scatter_id)

    def reduce(self, after=()):
        self.reduced = [_reduce_chips(f"reduce_chips_{self.tag}_{a}", p, l, self.chip, after)
                        for a, (p, l) in enumerate(zip(self.parts, self.landed))]

    def swap(self):
        self.swapped = _swap_reduced(f"swap_{self.tag}", self.reduced, self.pair_id)

    def update_mine(self):
        self.mine = [_adamw_half(f"adamw_mine_{self.tag}_{a}", g, *state, self.core)
                     for a, (g, state) in enumerate(zip(self.reduced, self.states))]

    def update_other(self, after=()):
        self.results = [_adamw_half(f"adamw_other_{self.tag}_{a}", g, *state, self.other_core, done, after)
                        for a, (g, state, done) in enumerate(zip(self.swapped, self.states, self.mine))]

    def token(self, stage):
        first = getattr(self, stage)[0]
        return first if not isinstance(first, (list, tuple)) else first[0]


def kernel(x, c, w_ada, b_ada, norm1_g, w_in, pool_mix_w, pool_scale, conv_w, conv_b, gnorm_pool_g, gnorm_conv_g, w_out, norm2_g, w_mlp_in, w_mlp_out, final_g, loss_target, m_w_ada, m_b_ada, m_norm1_g, m_w_in, m_pool_mix_w, m_pool_scale, m_conv_w, m_conv_b, m_gnorm_pool_g, m_gnorm_conv_g, m_w_out, m_norm2_g, m_w_mlp_in, m_w_mlp_out, m_final_g, v_w_ada, v_b_ada, v_norm1_g, v_w_in, v_pool_mix_w, v_pool_scale, v_conv_w, v_conv_b, v_gnorm_pool_g, v_gnorm_conv_g, v_w_out, v_norm2_g, v_w_mlp_in, v_w_mlp_out, v_final_g):
    seq, d_model = x.shape[1], x.shape[2]
    cols = w_in.shape[2]
    n_groups, group_dim = pool_mix_w.shape[1], pool_mix_w.shape[3]
    shard_rows = pool_mix_w.shape[2]
    ff_cols = w_mlp_in.shape[2]
    ada_cols = w_ada.shape[2]
    conv_shard = conv_w.shape[2]
    assert pool_scale.shape[1] == cols and conv_b.shape[1] == cols and n_groups * group_dim == cols
    assert cols % CONV_HEAD_DIM == 0 and group_dim % CONV_HEAD_DIM == 0 and shard_rows * N_CHIPS == group_dim

    ix, iy, ic = _position()
    chip = 2 * ix + iy
    me = 4 * ix + 2 * iy + ic
    position = tuple(jnp.reshape(v, (1,)).astype(jnp.int32) for v in (chip, ic, 1 - ic))

    xs, target = x[0], loss_target[0]
    final_row = final_g.reshape(1, d_model)

    small = _gather_flat("gather_cond", jnp.concatenate([c[0], conv_w[0].reshape(-1)]))
    c_all = small[:, :d_model]
    conv_w_full = jnp.concatenate(
        [small[2 * j, d_model:].reshape(3, conv_shard) for j in range(N_CHIPS)], axis=1)
    b_cols = lax.dynamic_slice_in_dim(b_ada, chip * ada_cols, ada_cols, axis=1)
    mod_part = _ada_forward(c_all, w_ada[0], b_cols)
    mod_all = _gather_flat("gather_mod", mod_part.reshape(-1)).reshape(N_DEV, N_DEV, ada_cols)
    mod = jnp.concatenate(
        [lax.dynamic_slice_in_dim(mod_all[2 * j], me, 1, axis=0) for j in range(N_CHIPS)], axis=1)
    shift1, scale1, gate1, shift2, scale2, gate2 = [mod[:, i * d_model:(i + 1) * d_model] for i in range(N_MOD)]

    shards = [w_in[0].astype(BF16), pool_mix_w[0].reshape(n_groups * shard_rows, group_dim).astype(BF16),
              w_out[0].astype(BF16), w_mlp_in[0].astype(BF16), w_mlp_out[0].astype(BF16)]
    wg_in, wg_pm = _gather_weights("gather_w_in", shards[0:2], 1)
    (wg_out,) = _gather_weights("gather_w_out", shards[2:3], 2)
    (wg_1,) = _gather_weights("gather_w_mlp_in", shards[3:4], 3)
    (wg_2,) = _gather_weights("gather_w_mlp_out", shards[4:5], 4)
    wg_pm = wg_pm.reshape(N_CHIPS, n_groups, shard_rows, group_dim)
    wg_out = wg_out.reshape(d_model, d_model)
    wg_2 = wg_2.reshape(N_CHIPS * ff_cols, d_model)

    h1 = _norm_modulate("norm_modulate_1", xs, norm1_g, scale1, shift1)
    proj = _in_projection(h1, wg_in)
    pooled = _pool_forward(proj, group_dim)
    mixed = _pool_mix_forward(pooled, wg_pm, pool_scale, gnorm_pool_g, d_model)
    mixed = _conv_forward(proj, conv_w_full, conv_b, gnorm_conv_g, mixed)
    x1, attn = _residual_projection("out_projection", mixed, wg_out, xs, gate1)
    h2 = _norm_modulate("norm_modulate_2", x1, norm2_g, scale2, shift2)
    act = _mlp_in(h2, wg_1)
    x2, mlp = _residual_projection("mlp_out", act, wg_2, x1, gate2, prologue=_square)

    dx2, dmlp, g_final, dgate2, sq_err = _loss_head(x2, target, final_row, mlp, gate2)
    dhid = _grad_hidden(dmlp, wg_2, act)
    gw_2 = _weight_grad("grad_w_mlp_out", act, dmlp, prologue=_square)
    red_2 = _GradReduction("w_mlp_out", [gw_2.reshape(N_CHIPS, ff_cols, d_model)],
                           [(w_mlp_out[0], m_w_mlp_out[0], v_w_mlp_out[0])], 8, 12, position)
    red_2.exchange()
    gw_1 = _weight_grad_slabs("grad_w_mlp_in", h2, dhid, ff_cols)
    red_1 = _GradReduction("w_mlp_in", [gw_1], [(w_mlp_in[0], m_w_mlp_in[0], v_w_mlp_in[0])], 7, 11, position)
    red_1.exchange()
    red_2.combine()
    red_2.scatter()
    dh2 = _grad_input_slabs("grad_h2", dhid, wg_1, after=[red_2.token("parts")])
    red_1.combine()
    red_1.scatter()
    dx1, dscale2, dshift2, g_norm2, dgate1, dattn = _norm_modulate_backward(
        "norm_modulate_backward_2", dh2, x1, norm2_g, scale2, dx2, attn, gate1, after=[red_1.token("parts")])
    gw_out = _weight_grad("grad_w_out", mixed, dattn)
    red_out = _GradReduction("w_out", [gw_out.reshape(N_CHIPS, d_model // N_CHIPS, d_model)],
                             [(w_out[0], m_w_out[0], v_w_out[0])], 6, 10, position)
    red_out.exchange()
    dmixed = _grad_input("grad_mixed", dattn, wg_out)
    red_2.reduce(after=[dmixed])
    red_2.swap()
    red_2.update_mine()
    dpooled, dpm, g_gpool, g_pscale = _pool_mix_backward(
        dmixed, pooled, wg_pm, pool_scale, gnorm_pool_g, after=[red_2.token("mine")])
    gw_pm = _pool_mix_weight_grad(pooled, dpm, n_groups)
    dproj, g_gconv, g_convb, g_convw = _mixers_backward(
        dpooled, dmixed, proj, conv_w_full, conv_b, gnorm_conv_g, group_dim)
    red_1.reduce(after=[dproj])
    red_1.swap()
    red_1.update_mine()
    red_out.combine(after=[red_1.token("mine")])
    red_out.scatter()
    gw_in = _weight_grad_slabs("grad_w_in", h1, dproj, None, after=[red_out.token("parts")])
    pm2d = (n_groups * shard_rows, group_dim)
    red_in = _GradReduction(
        "w_in", [gw_in, gw_pm.reshape((N_CHIPS,) + pm2d)],
        [(w_in[0], m_w_in[0], v_w_in[0]),
         (pool_mix_w[0].reshape(pm2d), m_pool_mix_w[0].reshape(pm2d), v_pool_mix_w[0].reshape(pm2d))],
        5, 9, position)
    red_in.exchange()
    red_2.update_other(after=[gw_in])
    red_out.reduce(after=[red_2.token("results")])
    red_out.swap()
    red_out.update_mine()
    red_in.combine(after=[red_out.token("mine")])
    red_in.scatter()
    dh1 = _grad_input_slabs("grad_h1", dproj, wg_in, after=[red_in.token("parts")])
    red_1.update_other(after=[dh1])
    grad_x, dscale1, dshift1, g_norm1 = _norm_modulate_backward(
        "norm_modulate_backward_1", dh1, xs, norm1_g, scale1, dx1, after=[red_1.token("results")])
    red_out.update_other(after=[grad_x])

    mine = jnp.concatenate(
        [dshift1, dscale1, dgate1, dshift2, dscale2, dgate2, g_norm1, g_norm2, g_final, sq_err,
         g_pscale, g_convb, g_gpool, g_gconv, g_convw.reshape(1, 3 * cols)], axis=1)
    gathered = _gather_flat("gather_small", mine.reshape(-1))
    sums, loss = _reduce_small(gathered, d_model)
    n_rep = (N_MOD + 3) * d_model
    g_rep = jnp.concatenate([sums[:, :n_rep], sums[:, n_rep + d_model:n_rep + d_model + 4 * cols]], axis=1)
    n_small = g_rep.shape[1]

    def pack(b, n1, n2, fg, ps, cb, gp, gc):
        return jnp.concatenate([b, n1, n2, fg.reshape(1, d_model), ps, cb, gp, gc], axis=1).reshape(8, n_small // 8)

    d_rep, m_rep, v_rep = _adamw(
        "adamw_small", g_rep.reshape(8, n_small // 8),
        pack(b_ada, norm1_g, norm2_g, final_g, pool_scale, conv_b, gnorm_pool_g, gnorm_conv_g),
        pack(m_b_ada, m_norm1_g, m_norm2_g, m_final_g, m_pool_scale, m_conv_b, m_gnorm_pool_g, m_gnorm_conv_g),
        pack(v_b_ada, v_norm1_g, v_norm2_g, v_final_g, v_pool_scale, v_conv_b, v_gnorm_pool_g, v_gnorm_conv_g))

    def unpack(flat):
        flat = flat.reshape(1, n_small)
        sizes = [N_MOD * d_model, d_model, d_model, d_model, cols, cols, cols, cols]
        parts, at = [], 0
        for size in sizes:
            parts.append(flat[:, at:at + size])
            at += size
        parts[3] = parts[3].reshape(d_model)
        return parts

    g_convw_full = sums[:, n_rep + d_model + 4 * cols:].reshape(3, cols)
    g_convw_mine = lax.dynamic_slice_in_dim(g_convw_full, chip * conv_shard, conv_shard, axis=1)
    d_convw, m_convw, v_convw = _adamw("adamw_conv_w", g_convw_mine, conv_w[0], m_conv_w[0], v_conv_w[0])

    dmod_cols = lax.dynamic_slice_in_dim(gathered[:, :N_MOD * d_model], chip * ada_cols, ada_cols, axis=1)
    g_ada, d_ada, mn_ada, vn_ada = _ada_backward(c_all.T, dmod_cols, w_ada[0], m_w_ada[0], v_w_ada[0])

    red_in.reduce(after=[g_ada, red_out.token("results")])
    red_in.swap()
    red_in.update_mine()
    red_in.update_other()

    small_parts = [unpack(g_rep), unpack(d_rep), unpack(m_rep), unpack(v_rep)]
    ada_parts = [g_ada, d_ada, mn_ada, vn_ada]
    convw_parts = [g_convw_mine, d_convw, m_convw, v_convw]

    def ordered(k):
        b, n1, n2, fg, ps, cb, gp, gc = small_parts[k]
        return [ada_parts[k][None], b, n1, red_in.results[0][k][None],
                red_in.results[1][k].reshape(pool_mix_w.shape), ps, convw_parts[k][None], cb, gp, gc,
                red_out.results[0][k][None], n2, red_1.results[0][k][None], red_2.results[0][k][None], fg]

    return (loss[0, 0], grad_x[None], *ordered(0), *ordered(1), *ordered(2), *ordered(3))
```

```python
import jax
import jax.numpy as jnp
from jax import lax
from jax.experimental import pallas as pl
from jax.experimental.pallas import tpu as pltpu
from jax.experimental.pallas import tpu_sc as plsc

F32 = jnp.float32
BF16 = jnp.bfloat16
MESH = pl.DeviceIdType.MESH
ANY = pl.BlockSpec(memory_space=pl.ANY)

NORM_EPS = 1e-6
POOL_WINDOWS = (2, 4, 8, 16)
CONV_HEAD_DIM = 128
N_MOD = 6
N_CHIPS = 4
N_DEV = 8

ADAM_LR = 0.001
ADAM_B1 = 0.9
ADAM_B2 = 0.999
ADAM_EPS = 1e-08
ADAM_WD = 0.01
ADAM_STEP = 10

VMEM_LIMIT_BYTES = 56 * 1024 * 1024

NN = (((1,), (0,)), ((), ()))
NT = (((1,), (1,)), ((), ()))
TN = (((0,), (0,)), ((), ()))


def _tile(n, pref):
    t = min(n, pref)
    while n % t:
        t //= 2
    return t


def _params(*sem):
    return pltpu.CompilerParams(dimension_semantics=sem, vmem_limit_bytes=VMEM_LIMIT_BYTES)


def _position():
    return lax.axis_index("x"), lax.axis_index("y"), lax.axis_index("c")


def _flip(ix, iy, ic, mask):
    return (1 - ix if mask & 4 else ix, 1 - iy if mask & 2 else iy, 1 - ic if mask & 1 else ic)


def _allgather8(name, blk):
    rows, cols = blk.shape

    def body(x_ref, out_ref, send_sems, recv_sems, local_sem):
        ix, iy, ic = _position()
        me = 4 * ix + 2 * iy + ic
        mine = pltpu.make_async_copy(x_ref, out_ref.at[me], local_sem)
        mine.start()
        sends = []
        for mask in range(1, N_DEV):
            cp = pltpu.make_async_remote_copy(
                src_ref=x_ref, dst_ref=out_ref.at[me],
                send_sem=send_sems.at[mask - 1], recv_sem=recv_sems.at[mask - 1],
                device_id=_flip(ix, iy, ic, mask), device_id_type=MESH)
            cp.start()
            sends.append(cp)
        for mask in range(1, N_DEV):
            px, py, pc = _flip(ix, iy, ic, mask)
            pltpu.make_async_remote_copy(
                src_ref=x_ref, dst_ref=out_ref.at[4 * px + 2 * py + pc],
                send_sem=send_sems.at[mask - 1], recv_sem=recv_sems.at[mask - 1],
                device_id=(px, py, pc), device_id_type=MESH).wait_recv()
        for cp in sends:
            cp.wait_send()
        mine.wait()

    return pl.pallas_call(
        body, name=name,
        out_shape=jax.ShapeDtypeStruct((N_DEV, rows, cols), F32),
        in_specs=[pl.BlockSpec(memory_space=pltpu.VMEM)],
        out_specs=pl.BlockSpec(memory_space=pltpu.VMEM),
        scratch_shapes=[pltpu.SemaphoreType.DMA((N_DEV - 1,)), pltpu.SemaphoreType.DMA((N_DEV - 1,)),
                        pltpu.SemaphoreType.DMA],
    )(blk)


def _gather_flat(name, vec):
    n = vec.shape[0]
    npad = -(-n // 1024) * 1024
    blk = jnp.pad(vec, (0, npad - n)).reshape(8, npad // 8)
    return _allgather8(name, blk).reshape(N_DEV, npad)[:, :n]


def _chip_relations(ix, iy):
    return [(1 - ix, iy), (ix, 1 - iy), (1 - ix, 1 - iy)]


def _gather_weights(name, shards, collective_id):
    n = len(shards)

    def body(*refs):
        src, out = refs[:n], refs[n:2 * n]
        send_sems, recv_sems = refs[2 * n:]
        ix, iy, ic = _position()
        chip = 2 * ix + iy
        sibling = (ix, iy, 1 - ic)
        rels = _chip_relations(ix, iy)

        _handshake([(px, py, ic) for px, py in rels] + [sibling])

        def half_rows(a, h):
            half = shards[a].shape[0] // 2
            return pl.ds(h * half, half)

        started = []
        for a in range(n):
            for r, (px, py) in enumerate(rels):
                cp = pltpu.make_async_remote_copy(
                    src_ref=src[a].at[half_rows(a, ic)], dst_ref=out[a].at[chip, half_rows(a, ic)],
                    send_sem=send_sems.at[6 * a + r], recv_sem=recv_sems.at[6 * a + r],
                    device_id=(px, py, ic), device_id_type=MESH)
                cp.start()
                started.append(cp)
        for a in range(n):
            for r, (px, py) in enumerate(rels):
                landed = out[a].at[2 * px + py, half_rows(a, ic)]
                pltpu.make_async_remote_copy(
                    src_ref=landed, dst_ref=landed,
                    send_sem=send_sems.at[6 * a + r], recv_sem=recv_sems.at[6 * a + r],
                    device_id=(px, py, ic), device_id_type=MESH).wait_recv()
                cp = pltpu.make_async_remote_copy(
                    src_ref=landed, dst_ref=landed,
                    send_sem=send_sems.at[6 * a + 3 + r], recv_sem=recv_sems.at[6 * a + 3 + r],
                    device_id=sibling, device_id_type=MESH)
                cp.start()
                started.append(cp)
        for a in range(n):
            for r, (px, py) in enumerate(rels):
                passed = out[a].at[2 * px + py, half_rows(a, 1 - ic)]
                pltpu.make_async_remote_copy(
                    src_ref=passed, dst_ref=passed,
                    send_sem=send_sems.at[6 * a + 3 + r], recv_sem=recv_sems.at[6 * a + 3 + r],
                    device_id=sibling, device_id_type=MESH).wait_recv()
        for cp in started:
            cp.wait_send()

    out_type = [jax.ShapeDtypeStruct((N_CHIPS,) + s.shape, s.dtype) for s in shards]
    return _sequencer_call(name, body, shards, out_type, [6 * n, 6 * n], collective_id)


def _place_own(name, gathered, shard, chip):
    rows, cols = shard.shape
    tr, tc = _tile(rows, 512), _tile(cols, 2048)

    def body(chip_ref, own_ref, gathered_ref, o_ref):
        o_ref[...] = own_ref[...]

    return pl.pallas_call(
        body, name=name,
        grid_spec=pltpu.PrefetchScalarGridSpec(
            num_scalar_prefetch=1, grid=(rows // tr, cols // tc),
            in_specs=[pl.BlockSpec((tr, tc), lambda i, j, chip_ref: (i, j)), ANY],
            out_specs=pl.BlockSpec((None, tr, tc), lambda i, j, chip_ref: (chip_ref[0], i, j))),
        out_shape=jax.ShapeDtypeStruct(gathered.shape, gathered.dtype),
        input_output_aliases={2: 0},
        compiler_params=_params("parallel", "parallel"),
    )(chip, shard, gathered)


def _sequencer_call(name, body, operands, out_type, sem_counts, collective_id):
    return pl.kernel(
        body, name=name, out_type=out_type,
        mesh=plsc.ScalarSubcoreMesh(axis_name="sequencer", num_cores=1),
        scratch_types=[pltpu.SemaphoreType.DMA((n,)) for n in sem_counts],
        compiler_params=pltpu.CompilerParams(collective_id=collective_id),
    )(*operands)


def _handshake(peers):
    barrier = pltpu.get_barrier_semaphore()
    for peer in peers:
        pl.semaphore_signal(barrier, inc=1, device_id=peer, device_id_type=MESH)
    pl.semaphore_wait(barrier, len(peers))


def _exchange_halves(name, grads, collective_id):
    n = len(grads)

    def body(*refs):
        src, out = refs[:n], refs[n:2 * n]
        send_sems, recv_sems = refs[2 * n:]
        ix, iy, ic = _position()
        sibling = (ix, iy, 1 - ic)
        _handshake([sibling])
        copies = []
        for a in range(n):
            half = grads[a].shape[1] // 2
            cp = pltpu.make_async_remote_copy(
                src_ref=src[a].at[pl.ds(0, N_CHIPS), pl.ds((1 - ic) * half, half)], dst_ref=out[a],
                send_sem=send_sems.at[a], recv_sem=recv_sems.at[a],
                device_id=sibling, device_id_type=MESH)
            cp.start()
            copies.append(cp)
        for cp in copies:
            cp.wait()

    out_type = [jax.ShapeDtypeStruct((N_CHIPS, g.shape[1] // 2, g.shape[2]), g.dtype) for g in grads]
    return _sequencer_call(name, body, grads, out_type, [n, n], collective_id)


def _scatter_partials(name, parts, collective_id):
    n = len(parts)

    def body(*refs):
        src, out = refs[:n], refs[n:2 * n]
        send_sems, recv_sems = refs[2 * n:]
        ix, iy, ic = _position()
        rels = _chip_relations(ix, iy)
        _handshake([(px, py, ic) for px, py in rels])
        copies = []
        for a in range(n):
            for r, (px, py) in enumerate(rels):
                cp = pltpu.make_async_remote_copy(
                    src_ref=src[a].at[2 * px + py], dst_ref=out[a].at[r],
                    send_sem=send_sems.at[3 * a + r], recv_sem=recv_sems.at[3 * a + r],
                    device_id=(px, py, ic), device_id_type=MESH)
                cp.start()
                copies.append(cp)
        for cp in copies:
            cp.wait()

    out_type = [jax.ShapeDtypeStruct((3,) + p.shape[1:], p.dtype) for p in parts]
    return _sequencer_call(name, body, parts, out_type, [3 * n, 3 * n], collective_id)


def _swap_reduced(name, reduced, collective_id):
    n = len(reduced)

    def body(*refs):
        src, out = refs[:n], refs[n:2 * n]
        send_sems, recv_sems = refs[2 * n:]
        ix, iy, ic = _position()
        sibling = (ix, iy, 1 - ic)
        _handshake([sibling])
        copies = []
        for a in range(n):
            cp = pltpu.make_async_remote_copy(
                src_ref=src[a], dst_ref=out[a], send_sem=send_sems.at[a], recv_sem=recv_sems.at[a],
                device_id=sibling, device_id_type=MESH)
            cp.start()
            copies.append(cp)
        for cp in copies:
            cp.wait()

    out_type = [jax.ShapeDtypeStruct(r.shape, r.dtype) for r in reduced]
    return _sequencer_call(name, body, reduced, out_type, [n, n], collective_id)


def _add_half(name, grad, recv, core, after=()):
    _, rows, cols = grad.shape
    half = rows // 2
    tr, tc = _tile(half, 512), _tile(cols, 2048)
    nbr = half // tr

    def body(core_ref, g_ref, r_ref, *rest):
        rest[-1][...] = (g_ref[...].astype(F32) + r_ref[...].astype(F32)).astype(BF16)

    return pl.pallas_call(
        body, name=name,
        grid_spec=pltpu.PrefetchScalarGridSpec(
            num_scalar_prefetch=1, grid=(N_CHIPS, nbr, cols // tc),
            in_specs=[pl.BlockSpec((None, tr, tc), lambda s, i, j, core_ref: (s, core_ref[0] * nbr + i, j)),
                      pl.BlockSpec((None, tr, tc), lambda s, i, j, core_ref: (s, i, j))] + [ANY] * len(after),
            out_specs=pl.BlockSpec((None, tr, tc), lambda s, i, j, core_ref: (s, i, j))),
        out_shape=jax.ShapeDtypeStruct((N_CHIPS, half, cols), BF16),
        compiler_params=_params("parallel", "parallel", "parallel"),
    )(core, grad, recv, *after)


def _reduce_chips(name, part, recv, chip, after=()):
    _, half, cols = part.shape
    tr, tc = _tile(half, 512), _tile(cols, 2048)

    def body(chip_ref, p_ref, r_ref, *rest):
        o_ref = rest[-1]
        acc = p_ref[...].astype(F32)
        for r in range(3):
            acc = acc + r_ref[r].astype(F32)
        o_ref[...] = acc

    return pl.pallas_call(
        body, name=name,
        grid_spec=pltpu.PrefetchScalarGridSpec(
            num_scalar_prefetch=1, grid=(half // tr, cols // tc),
            in_specs=[pl.BlockSpec((None, tr, tc), lambda i, j, chip_ref: (chip_ref[0], i, j)),
                      pl.BlockSpec((3, tr, tc), lambda i, j, chip_ref: (0, i, j))] + [ANY] * len(after),
            out_specs=pl.BlockSpec((tr, tc), lambda i, j, chip_ref: (i, j))),
        out_shape=jax.ShapeDtypeStruct((half, cols), F32),
        compiler_params=_params("parallel", "parallel"),
    )(chip, part, recv, *after)


def _adamw_half(name, g_half, w, m, v, which, done=None, after=()):
    half, cols = g_half.shape
    tr, tc = _tile(half, 256), _tile(cols, 2048)
    nbr = half // tr
    bc1 = 1.0 - ADAM_B1 ** ADAM_STEP
    bc2 = 1.0 - ADAM_B2 ** ADAM_STEP

    def body(which_ref, g_ref, w_ref, m_ref, v_ref, *rest):
        go_ref, d_ref, mo_ref, vo_ref = rest[-4:]
        gv = g_ref[...]
        mn = ADAM_B1 * m_ref[...] + (1.0 - ADAM_B1) * gv
        vn = ADAM_B2 * v_ref[...] + (1.0 - ADAM_B2) * (gv * gv)
        go_ref[...] = gv
        d_ref[...] = -ADAM_LR * ((mn / bc1) / (jnp.sqrt(vn / bc2) + ADAM_EPS) + ADAM_WD * w_ref[...])
        mo_ref[...] = mn
        vo_ref[...] = vn

    mine = pl.BlockSpec((tr, tc), lambda i, j, which_ref: (which_ref[0] * nbr + i, j))
    kept = [] if done is None else list(done)
    shape = jax.ShapeDtypeStruct((2 * half, cols), F32)
    return pl.pallas_call(
        body, name=name,
        grid_spec=pltpu.PrefetchScalarGridSpec(
            num_scalar_prefetch=1, grid=(nbr, cols // tc),
            in_specs=([pl.BlockSpec((tr, tc), lambda i, j, which_ref: (i, j)), mine, mine, mine]
                      + [ANY] * (len(kept) + len(after))),
            out_specs=[mine] * 4),
        out_shape=[shape] * 4,
        input_output_aliases={5 + k: k for k in range(len(kept))},
        compiler_params=_params("parallel", "parallel"),
    )(which, g_half, w, m, v, *kept, *after)


def _adamw(name, g, w, m, v):
    rows, cols = g.shape
    tr, tc = _tile(rows, 256), _tile(cols, 2048)
    bc1 = 1.0 - ADAM_B1 ** ADAM_STEP
    bc2 = 1.0 - ADAM_B2 ** ADAM_STEP

    def body(g_ref, w_ref, m_ref, v_ref, d_ref, mo_ref, vo_ref):
        gv = g_ref[...]
        mn = ADAM_B1 * m_ref[...] + (1.0 - ADAM_B1) * gv
        vn = ADAM_B2 * v_ref[...] + (1.0 - ADAM_B2) * (gv * gv)
        d_ref[...] = -ADAM_LR * ((mn / bc1) / (jnp.sqrt(vn / bc2) + ADAM_EPS) + ADAM_WD * w_ref[...])
        mo_ref[...] = mn
        vo_ref[...] = vn

    spec = pl.BlockSpec((tr, tc), lambda i, j: (i, j))
    shape = jax.ShapeDtypeStruct((rows, cols), F32)
    return pl.pallas_call(
        body, name=name, grid=(rows // tr, cols // tc),
        in_specs=[spec] * 4, out_specs=[spec] * 3, out_shape=[shape] * 3,
        compiler_params=_params("parallel", "parallel"),
    )(g, w, m, v)


def _reduce_small(gathered, d_model):
    n = gathered.shape[1]
    loss_at = (N_MOD + 3) * d_model

    def body(g_ref, s_ref, loss_ref):
        acc = g_ref[0:1, :]
        for d in range(1, N_DEV):
            acc = acc + g_ref[d:d + 1, :]
        s_ref[...] = acc
        lanes = acc[:, loss_at:loss_at + d_model]
        loss_ref[...] = jnp.broadcast_to((0.5 / d_model) * jnp.sum(lanes, axis=1, keepdims=True), loss_ref.shape)

    return pl.pallas_call(
        body, name="reduce_small",
        out_shape=[jax.ShapeDtypeStruct((1, n), F32), jax.ShapeDtypeStruct((1, 128), F32)],
        compiler_params=pltpu.CompilerParams(vmem_limit_bytes=VMEM_LIMIT_BYTES),
    )(gathered)


def _ada_forward(c_all, w_ada, b_cols):
    d_model, width = w_ada.shape
    tn = _tile(width, 512)

    def body(c_ref, w_ref, b_ref, o_ref):
        cv = c_ref[...]
        act = cv * jax.nn.sigmoid(cv)
        o_ref[...] = lax.dot_general(act, w_ref[...], NN, precision=lax.Precision.HIGHEST,
                                     preferred_element_type=F32) + b_ref[...]

    return pl.pallas_call(
        body, name="ada_forward", grid=(width // tn,),
        in_specs=[pl.BlockSpec((N_DEV, d_model), lambda j: (0, 0)),
                  pl.BlockSpec((d_model, tn), lambda j: (0, j)),
                  pl.BlockSpec((1, tn), lambda j: (0, j))],
        out_specs=pl.BlockSpec((N_DEV, tn), lambda j: (0, j)),
        out_shape=jax.ShapeDtypeStruct((N_DEV, width), F32),
        compiler_params=_params("parallel"),
    )(c_all, w_ada, b_cols)


def _ada_backward(c_all_t, dmod_cols, w, m, v):
    d_model, width = w.shape
    tr, tc = _tile(d_model, 256), _tile(width, 1536)
    bc1 = 1.0 - ADAM_B1 ** ADAM_STEP
    bc2 = 1.0 - ADAM_B2 ** ADAM_STEP

    def body(c_ref, dm_ref, w_ref, m_ref, v_ref, g_ref, d_ref, mo_ref, vo_ref):
        cv = c_ref[...]
        act = cv * jax.nn.sigmoid(cv)
        gv = lax.dot_general(act, dm_ref[...], NN, precision=lax.Precision.HIGHEST, preferred_element_type=F32)
        mn = ADAM_B1 * m_ref[...] + (1.0 - ADAM_B1) * gv
        vn = ADAM_B2 * v_ref[...] + (1.0 - ADAM_B2) * (gv * gv)
        g_ref[...] = gv
        d_ref[...] = -ADAM_LR * ((mn / bc1) / (jnp.sqrt(vn / bc2) + ADAM_EPS) + ADAM_WD * w_ref[...])
        mo_ref[...] = mn
        vo_ref[...] = vn

    spec = pl.BlockSpec((tr, tc), lambda i, j: (i, j))
    shape = jax.ShapeDtypeStruct((d_model, width), F32)
    return pl.pallas_call(
        body, name="ada_backward", grid=(d_model // tr, width // tc),
        in_specs=[pl.BlockSpec((tr, N_DEV), lambda i, j: (i, 0)),
                  pl.BlockSpec((N_DEV, tc), lambda i, j: (0, j)), spec, spec, spec],
        out_specs=[spec] * 4, out_shape=[shape] * 4,
        compiler_params=_params("parallel", "parallel"),
    )(c_all_t, dmod_cols, w, m, v)


def _matmul(name, a, b, extras, *, grid, tiles, dims, a_spec, b_spec, extra_specs, out_shape, out_specs,
            epilogue, prologue=None, after=()):
    tm, tn, _ = tiles
    gm, gn, gk = grid
    n_extra, n_out = len(extras), len(out_shape)
    first_out = 2 + n_extra + len(after)

    def product(a_ref, b_ref):
        av = a_ref[...]
        if prologue is not None:
            av = prologue(av)
        return lax.dot_general(av, b_ref[...], dims, preferred_element_type=F32)

    def body_single(*refs):
        epilogue(product(refs[0], refs[1]), refs[2:2 + n_extra], refs[first_out:first_out + n_out])

    def body(*refs):
        a_ref, b_ref = refs[0], refs[1]
        extra_refs = refs[2:2 + n_extra]
        out_refs = refs[first_out:first_out + n_out]
        acc_ref = refs[-1]
        k = pl.program_id(2)

        @pl.when(k == 0)
        def _():
            acc_ref[...] = jnp.zeros_like(acc_ref)

        acc_ref[...] += product(a_ref, b_ref)

        @pl.when(k == gk - 1)
        def _():
            epilogue(acc_ref[...], extra_refs, out_refs)

    single = gk == 1
    return pl.pallas_call(
        body_single if single else body, name=name, grid=(gm, gn, gk),
        in_specs=[a_spec, b_spec, *extra_specs] + [ANY] * len(after), out_specs=out_specs, out_shape=out_shape,
        scratch_shapes=[] if single else [pltpu.VMEM((tm, tn), F32)],
        compiler_params=_params("parallel", "parallel", "arbitrary"),
    )(a, b, *extras, *after)


def _store(dtype):
    def epilogue(acc, extra_refs, out_refs):
        out_refs[0][...] = acc.astype(dtype)
    return epilogue


def _residual_epilogue(acc, extra_refs, out_refs):
    res_ref, gate_ref = extra_refs
    out_refs[0][...] = res_ref[...] + gate_ref[...] * acc
    out_refs[1][...] = acc.astype(BF16)


def _square(av):
    af = av.astype(F32)
    return (af * af).astype(BF16)


MM_TILE_M = 1024
MM_TILE_N = 1024
MM_WHOLE_K = 4096
MM_TILE_K = 2048


def _mm_tiles(m, n, k, tn_pref=MM_TILE_N, k_block=None):
    tk = k if k <= MM_WHOLE_K else MM_TILE_K
    if k_block is not None:
        tk = min(tk, k_block)
    return _tile(m, MM_TILE_M), _tile(n, tn_pref), _tile(k, tk)


def _in_projection(h, w_slabs):
    seq, d_model = h.shape
    _, _, cols = w_slabs.shape
    tm, tn, tk = _mm_tiles(seq, cols, d_model)
    nbj = cols // tn
    return _matmul(
        "in_projection", h, w_slabs, (), grid=(seq // tm, N_CHIPS * nbj, d_model // tk), tiles=(tm, tn, tk), dims=NN,
        a_spec=pl.BlockSpec((tm, tk), lambda i, j, k: (i, k)),
        b_spec=pl.BlockSpec((None, tk, tn), lambda i, j, k: (j // nbj, k, j % nbj)),
        extra_specs=(),
        out_shape=[jax.ShapeDtypeStruct((N_CHIPS, seq, cols), BF16)],
        out_specs=[pl.BlockSpec((None, tm, tn), lambda i, j, k: (j // nbj, i, j % nbj))],
        epilogue=_store(BF16))[0]


def _residual_projection(name, a, w, res, gate, prologue=None):
    seq, kdim = a.shape
    d_model = w.shape[1]
    tm, tn, tk = _mm_tiles(seq, d_model, kdim, tn_pref=MM_TILE_N // 2 if kdim <= MM_WHOLE_K else MM_TILE_N)
    tile = pl.BlockSpec((tm, tn), lambda i, j, k: (i, j))
    return _matmul(
        name, a, w, (res, gate), grid=(seq // tm, d_model // tn, kdim // tk), tiles=(tm, tn, tk), dims=NN,
        a_spec=pl.BlockSpec((tm, tk), lambda i, j, k: (i, k)),
        b_spec=pl.BlockSpec((tk, tn), lambda i, j, k: (k, j)),
        extra_specs=(tile, pl.BlockSpec((1, tn), lambda i, j, k: (0, j))),
        out_shape=[jax.ShapeDtypeStruct((seq, d_model), F32), jax.ShapeDtypeStruct((seq, d_model), BF16)],
        out_specs=[tile, tile],
        epilogue=_residual_epilogue, prologue=prologue)


def _mlp_in(h, w_slabs):
    seq, d_model = h.shape
    _, _, cols = w_slabs.shape
    tm, tn, tk = _mm_tiles(seq, cols, d_model)
    nbj = cols // tn

    def epilogue(acc, extra_refs, out_refs):
        out_refs[0][...] = jnp.maximum(acc, 0.0).astype(BF16)

    return _matmul(
        "mlp_in", h, w_slabs, (), grid=(seq // tm, N_CHIPS * nbj, d_model // tk), tiles=(tm, tn, tk), dims=NN,
        a_spec=pl.BlockSpec((tm, tk), lambda i, j, k: (i, k)),
        b_spec=pl.BlockSpec((None, tk, tn), lambda i, j, k: (j // nbj, k, j % nbj)),
        extra_specs=(),
        out_shape=[jax.ShapeDtypeStruct((seq, N_CHIPS * cols), BF16)],
        out_specs=[pl.BlockSpec((tm, tn), lambda i, j, k: (i, j))],
        epilogue=epilogue)[0]


def _grad_hidden(dmlp, w2, act):
    seq, d_model = dmlp.shape
    ff = w2.shape[0]
    tm, tn, tk = _mm_tiles(seq, ff, d_model)

    def epilogue(acc, extra_refs, out_refs):
        out_refs[0][...] = (acc * (2.0 * extra_refs[0][...].astype(F32))).astype(BF16)

    tile = pl.BlockSpec((tm, tn), lambda i, j, k: (i, j))
    return _matmul(
        "grad_hidden", dmlp, w2, (act,), grid=(seq // tm, ff // tn, d_model // tk), tiles=(tm, tn, tk), dims=NT,
        a_spec=pl.BlockSpec((tm, tk), lambda i, j, k: (i, k)),
        b_spec=pl.BlockSpec((tn, tk), lambda i, j, k: (j, k)),
        extra_specs=(tile,),
        out_shape=[jax.ShapeDtypeStruct((seq, ff), BF16)], out_specs=[tile],
        epilogue=epilogue)[0]


def _weight_grad(name, a, b, prologue=None):
    seq, m = a.shape
    n = b.shape[1]
    tm, tn, tk = _mm_tiles(m, n, seq)
    return _matmul(
        name, a, b, (), grid=(m // tm, n // tn, seq // tk), tiles=(tm, tn, tk), dims=TN,
        a_spec=pl.BlockSpec((tk, tm), lambda i, j, k: (k, i)),
        b_spec=pl.BlockSpec((tk, tn), lambda i, j, k: (k, j)),
        extra_specs=(),
        out_shape=[jax.ShapeDtypeStruct((m, n), BF16)],
        out_specs=[pl.BlockSpec((tm, tn), lambda i, j, k: (i, j))],
        epilogue=_store(BF16), prologue=prologue)[0]


def _weight_grad_slabs(name, a, b, slab_cols, after=()):
    seq, m = a.shape
    cols = b.shape[2] if slab_cols is None else slab_cols
    tm, tn, tk = _mm_tiles(m, cols, seq)
    nbj = cols // tn
    if slab_cols is None:
        b_spec = pl.BlockSpec((None, tk, tn), lambda i, j, k: (j // nbj, k, j % nbj))
    else:
        b_spec = pl.BlockSpec((tk, tn), lambda i, j, k: (k, j))
    return _matmul(
        name, a, b, (), grid=(m // tm, N_CHIPS * nbj, seq // tk), tiles=(tm, tn, tk), dims=TN,
        a_spec=pl.BlockSpec((tk, tm), lambda i, j, k: (k, i)),
        b_spec=b_spec, extra_specs=(),
        out_shape=[jax.ShapeDtypeStruct((N_CHIPS, m, cols), BF16)],
        out_specs=[pl.BlockSpec((None, tm, tn), lambda i, j, k: (j // nbj, i, j % nbj))],
        epilogue=_store(BF16), after=after)[0]


def _grad_input_slabs(name, dy, w_slabs, after=()):
    _, d_model, cols = w_slabs.shape
    seq = dy.shape[1] if dy.ndim == 3 else dy.shape[0]
    tm, tn, tk = _mm_tiles(seq, d_model, N_CHIPS * cols, k_block=cols)
    nbk = cols // tk
    if dy.ndim == 3:
        a_spec = pl.BlockSpec((None, tm, tk), lambda i, j, k: (k // nbk, i, k % nbk))
    else:
        a_spec = pl.BlockSpec((tm, tk), lambda i, j, k: (i, k))
    return _matmul(
        name, dy, w_slabs, (), grid=(seq // tm, d_model // tn, N_CHIPS * nbk), tiles=(tm, tn, tk), dims=NT,
        a_spec=a_spec,
        b_spec=pl.BlockSpec((None, tn, tk), lambda i, j, k: (k // nbk, j, k % nbk)),
        extra_specs=(),
        out_shape=[jax.ShapeDtypeStruct((seq, d_model), F32)],
        out_specs=[pl.BlockSpec((tm, tn), lambda i, j, k: (i, j))],
        epilogue=_store(F32), after=after)[0]


def _grad_input(name, dy, w):
    seq, n = dy.shape
    kdim = w.shape[0]
    tm, tn, tk = _mm_tiles(seq, kdim, n)
    return _matmul(
        name, dy, w, (), grid=(seq // tm, kdim // tn, n // tk), tiles=(tm, tn, tk), dims=NT,
        a_spec=pl.BlockSpec((tm, tk), lambda i, j, k: (i, k)),
        b_spec=pl.BlockSpec((tn, tk), lambda i, j, k: (j, k)),
        extra_specs=(),
        out_shape=[jax.ShapeDtypeStruct((seq, kdim), F32)],
        out_specs=[pl.BlockSpec((tm, tn), lambda i, j, k: (i, j))],
        epilogue=_store(F32))[0]


ROW_TILE = 128


def _norm_modulate(name, xin, g, scale, shift):
    seq, d_model = xin.shape
    tr = _tile(seq, ROW_TILE)

    def body(x_ref, g_ref, sc_ref, sh_ref, h_ref):
        xv = x_ref[...]
        r = lax.rsqrt(jnp.mean(xv * xv, axis=-1, keepdims=True) + NORM_EPS)
        h_ref[...] = (((xv * r) * g_ref[...]) * (1.0 + sc_ref[...]) + sh_ref[...]).astype(BF16)

    row = pl.BlockSpec((tr, d_model), lambda i: (i, 0))
    vec = pl.BlockSpec((1, d_model), lambda i: (0, 0))
    return pl.pallas_call(
        body, name=name, grid=(seq // tr,),
        in_specs=[row, vec, vec, vec], out_specs=row,
        out_shape=jax.ShapeDtypeStruct((seq, d_model), BF16),
        compiler_params=_params("parallel"),
    )(xin, g, scale, shift)


def _loss_head(x2, target, final_g, mlp, gate2):
    seq, d_model = x2.shape
    tr = _tile(seq, ROW_TILE)

    def body(x_ref, t_ref, fg_ref, mlp_ref, gate_ref, dx_ref, dmlp_ref, gfg_ref, dgate_ref, sq_ref):
        @pl.when(pl.program_id(0) == 0)
        def _():
            gfg_ref[...] = jnp.zeros_like(gfg_ref)
            dgate_ref[...] = jnp.zeros_like(dgate_ref)
            sq_ref[...] = jnp.zeros_like(sq_ref)

        xv = x_ref[...]
        fg = fg_ref[...]
        r = lax.rsqrt(jnp.mean(xv * xv, axis=-1, keepdims=True) + NORM_EPS)
        n = xv * r
        err = n * fg - t_ref[...]
        sq_ref[...] += jnp.sum(err * err, axis=0, keepdims=True)
        dy = err * (1.0 / d_model)
        gfg_ref[...] += jnp.sum(dy * n, axis=0, keepdims=True)
        dn = dy * fg
        dx = r * (dn - n * jnp.mean(dn * n, axis=-1, keepdims=True))
        dx_ref[...] = dx
        dgate_ref[...] += jnp.sum(dx * mlp_ref[...].astype(F32), axis=0, keepdims=True)
        dmlp_ref[...] = (dx * gate_ref[...]).astype(BF16)

    row = pl.BlockSpec((tr, d_model), lambda i: (i, 0))
    vec = pl.BlockSpec((1, d_model), lambda i: (0, 0))
    vshape = jax.ShapeDtypeStruct((1, d_model), F32)
    return pl.pallas_call(
        body, name="loss_head", grid=(seq // tr,),
        in_specs=[row, row, vec, row, vec], out_specs=[row, row, vec, vec, vec],
        out_shape=[jax.ShapeDtypeStruct((seq, d_model), F32), jax.ShapeDtypeStruct((seq, d_model), BF16),
                   vshape, vshape, vshape],
        compiler_params=_params("arbitrary"),
    )(x2, target, final_g, mlp, gate2)


def _norm_modulate_backward(name, dh, xin, g, scale, dres, branch=None, gate=None, after=()):
    seq, d_model = xin.shape
    tr = _tile(seq, ROW_TILE)
    with_branch = branch is not None
    n_in = (7 if with_branch else 5) + len(after)

    def body(*refs):
        dh_ref, x_ref, g_ref, sc_ref, dres_ref = refs[:5]
        outs = refs[n_in:]
        dx_ref, dsc_ref, dsh_ref, dg_ref = outs[:4]

        @pl.when(pl.program_id(0) == 0)
        def _():
            for ref in outs[1:5] if with_branch else outs[1:4]:
                ref[...] = jnp.zeros_like(ref)

        xv = x_ref[...]
        gv = g_ref[...]
        dhv = dh_ref[...]
        r = lax.rsqrt(jnp.mean(xv * xv, axis=-1, keepdims=True) + NORM_EPS)
        xn = xv * r
        dsh_ref[...] += jnp.sum(dhv, axis=0, keepdims=True)
        dsc_ref[...] += jnp.sum(dhv * (xn * gv), axis=0, keepdims=True)
        t = dhv * (1.0 + sc_ref[...])
        dg_ref[...] += jnp.sum(t * xn, axis=0, keepdims=True)
        dxn = t * gv
        dx = dres_ref[...] + r * (dxn - xn * jnp.mean(dxn * xn, axis=-1, keepdims=True))
        dx_ref[...] = dx
        if with_branch:
            br_ref, gate_ref = refs[5:7]
            dgate_ref, dbr_ref = outs[4:6]
            dgate_ref[...] += jnp.sum(dx * br_ref[...].astype(F32), axis=0, keepdims=True)
            dbr_ref[...] = (dx * gate_ref[...]).astype(BF16)

    row = pl.BlockSpec((tr, d_model), lambda i: (i, 0))
    vec = pl.BlockSpec((1, d_model), lambda i: (0, 0))
    vshape = jax.ShapeDtypeStruct((1, d_model), F32)
    in_specs = [row, row, vec, vec, row]
    out_specs = [row, vec, vec, vec]
    out_shape = [jax.ShapeDtypeStruct((seq, d_model), F32), vshape, vshape, vshape]
    args = [dh, xin, g, scale, dres]
    if with_branch:
        in_specs += [row, vec]
        out_specs += [vec, row]
        out_shape += [vshape, jax.ShapeDtypeStruct((seq, d_model), BF16)]
        args += [branch, gate]
    in_specs += [ANY] * len(after)
    args += list(after)
    return pl.pallas_call(
        body, name=name, grid=(seq // tr,),
        in_specs=in_specs, out_specs=out_specs, out_shape=out_shape,
        compiler_params=_params("arbitrary"),
    )(*args)


def _shifted(v, k, t):
    seq = v.shape[0]
    if k == 0:
        return v
    moved = pltpu.roll(v, (-k) % seq, 0)
    return jnp.where((t + k >= 0) & (t + k < seq), moved, 0.0)


def _window_sum(v, offsets, t):
    acc = None
    for k in offsets:
        term = _shifted(v, k, t)
        acc = term if acc is None else acc + term
    return acc


def _window_count(seq, half):
    t = lax.broadcasted_iota(jnp.int32, (seq, 1), 0)
    return (jnp.minimum(t + half, seq) - jnp.maximum(t - half, 0)).astype(F32)


def _pool_forward(proj, group_dim):
    _, seq, cols = proj.shape
    tl = _tile(group_dim, 256)
    nbl = group_dim // tl
    n_groups = cols // group_dim

    def body(v_ref, o_ref):
        g = pl.program_id(0)
        for gi, window in enumerate(POOL_WINDOWS[:n_groups]):
            @pl.when(g == gi)
            def _(window=window):
                half = window // 2
                v = v_ref[...].astype(F32)
                t = lax.broadcasted_iota(jnp.int32, v.shape, 0)
                total = _window_sum(v, range(-half, half), t)
                o_ref[...] = (total / _window_count(seq, half) - v).astype(BF16)

    return pl.pallas_call(
        body, name="pool_forward", grid=(n_groups, nbl),
        in_specs=[pl.BlockSpec((None, seq, tl), lambda g, j: (0, 0, g * nbl + j))],
        out_specs=pl.BlockSpec((seq, tl), lambda g, j: (0, g * nbl + j)),
        out_shape=jax.ShapeDtypeStruct((seq, cols), BF16),
        compiler_params=_params("parallel", "parallel"),
    )(proj)


def _group_matrix(w_ref):
    return jnp.concatenate([w_ref[r] for r in range(N_CHIPS)], axis=0)


def _pool_mix_forward(pooled, w_pm, pool_scale, gnorm_g, d_model):
    seq, cols = pooled.shape
    _, n_groups, shard_rows, group_dim = w_pm.shape
    tm = _tile(seq, 512)

    def body(p_ref, w_ref, ps_ref, g_ref, o_ref):
        a = jnp.dot(p_ref[...], _group_matrix(w_ref), preferred_element_type=F32) * ps_ref[...]
        ra = lax.rsqrt(jnp.mean(a * a, axis=-1, keepdims=True) + NORM_EPS)
        o_ref[...] = ((a * ra) * g_ref[...]).astype(BF16)

    tile = pl.BlockSpec((tm, group_dim), lambda g, i: (i, g))
    vec = pl.BlockSpec((1, group_dim), lambda g, i: (0, g))
    return pl.pallas_call(
        body, name="pool_mix_forward", grid=(n_groups, seq // tm),
        in_specs=[tile, pl.BlockSpec((N_CHIPS, None, shard_rows, group_dim), lambda g, i: (0, g, 0, 0)), vec, vec],
        out_specs=tile,
        out_shape=jax.ShapeDtypeStruct((seq, d_model), BF16),
        compiler_params=_params("parallel", "parallel"),
    )(pooled, w_pm, pool_scale, gnorm_g)


def _conv_parts(b_ref, c_ref, u_ref, w_ref, bias_ref):
    bv = b_ref[...].astype(F32)
    cu = c_ref[...].astype(F32) * u_ref[...].astype(F32)
    t = lax.broadcasted_iota(jnp.int32, cu.shape, 0)
    prev, nxt = _shifted(cu, -1, t), _shifted(cu, 1, t)
    w = w_ref[...]
    conv = w[0:1] * prev + w[1:2] * cu + w[2:3] * nxt + bias_ref[...]
    return bv, cu, prev, nxt, conv, w, t


def _conv_forward(proj, conv_w, conv_b, gnorm_g, mixed):
    _, seq, cols = proj.shape
    tl = CONV_HEAD_DIM
    first = cols // tl

    def body(b_ref, c_ref, u_ref, w_ref, bias_ref, g_ref, mixed_ref, o_ref):
        bv, _, _, _, conv, _, _ = _conv_parts(b_ref, c_ref, u_ref, w_ref, bias_ref)
        bo = bv * conv
        rb = lax.rsqrt(jnp.mean(bo * bo, axis=-1, keepdims=True) + NORM_EPS)
        o_ref[...] = ((bo * rb) * g_ref[...]).astype(BF16)

    def slab(s):
        return pl.BlockSpec((None, seq, tl), lambda j, s=s: (s, 0, j))

    vec = pl.BlockSpec((1, tl), lambda j: (0, j))
    return pl.pallas_call(
        body, name="conv_forward", grid=(cols // tl,),
        in_specs=[slab(1), slab(2), slab(3), pl.BlockSpec((3, tl), lambda j: (0, j)), vec, vec, ANY],
        out_specs=pl.BlockSpec((seq, tl), lambda j: (0, first + j)),
        out_shape=jax.ShapeDtypeStruct(mixed.shape, mixed.dtype),
        input_output_aliases={6: 0},
        compiler_params=_params("parallel"),
    )(proj, proj, proj, conv_w, conv_b, gnorm_g, mixed)


def _pool_mix_backward(dmixed, pooled, w_pm, pool_scale, gnorm_g, after=()):
    seq, cols = pooled.shape
    _, n_groups, shard_rows, group_dim = w_pm.shape
    tm = _tile(seq, 512)

    def body(dm_ref, p_ref, w_ref, ps_ref, g_ref, *rest):
        dp_ref, dpm_ref, gg_ref, gps_ref = rest[-4:]

        @pl.when(pl.program_id(1) == 0)
        def _():
            gg_ref[...] = jnp.zeros_like(gg_ref)
            gps_ref[...] = jnp.zeros_like(gps_ref)

        w = _group_matrix(w_ref)
        ps = ps_ref[...]
        a_pre = jnp.dot(p_ref[...], w, preferred_element_type=F32)
        a = a_pre * ps
        ra = lax.rsqrt(jnp.mean(a * a, axis=-1, keepdims=True) + NORM_EPS)
        an = a * ra
        dm = dm_ref[...]
        gg_ref[...] += jnp.sum(dm * an, axis=0, keepdims=True)
        dan = dm * g_ref[...]
        da = ra * (dan - an * jnp.mean(dan * an, axis=-1, keepdims=True))
        gps_ref[...] += jnp.sum(da * a_pre, axis=0, keepdims=True)
        dpm = (da * ps).astype(BF16)
        dpm_ref[...] = dpm
        dp_ref[...] = lax.dot_general(dpm, w, NT, preferred_element_type=F32)

    tile = pl.BlockSpec((tm, group_dim), lambda g, i: (i, g))
    vec = pl.BlockSpec((1, group_dim), lambda g, i: (0, g))
    vshape = jax.ShapeDtypeStruct((1, cols), F32)
    return pl.pallas_call(
        body, name="pool_mix_backward", grid=(n_groups, seq // tm),
        in_specs=[tile, tile, pl.BlockSpec((N_CHIPS, None, shard_rows, group_dim), lambda g, i: (0, g, 0, 0)),
                  vec, vec] + [ANY] * len(after),
        out_specs=[tile, tile, vec, vec],
        out_shape=[jax.ShapeDtypeStruct((seq, cols), F32), jax.ShapeDtypeStruct((seq, cols), BF16), vshape, vshape],
        compiler_params=_params("parallel", "arbitrary"),
    )(dmixed, pooled, w_pm, pool_scale, gnorm_g, *after)


def _pool_mix_weight_grad(pooled, dpm, n_groups):
    seq, cols = pooled.shape
    group_dim = cols // n_groups
    shard_rows = group_dim // N_CHIPS
    tk = _tile(seq, 1024)
    gk = seq // tk

    def body(p_ref, d_ref, o_ref, acc_ref):
        k = pl.program_id(1)

        @pl.when(k == 0)
        def _():
            acc_ref[...] = jnp.zeros_like(acc_ref)

        acc_ref[...] += lax.dot_general(p_ref[...], d_ref[...], TN, preferred_element_type=F32)

        @pl.when(k == gk - 1)
        def _():
            for r in range(N_CHIPS):
                o_ref[r] = acc_ref[r * shard_rows:(r + 1) * shard_rows, :].astype(BF16)

    tile = pl.BlockSpec((tk, group_dim), lambda g, k: (k, g))
    return pl.pallas_call(
        body, name="pool_mix_weight_grad", grid=(n_groups, gk),
        in_specs=[tile, tile],
        out_specs=pl.BlockSpec((N_CHIPS, None, shard_rows, group_dim), lambda g, k: (0, g, 0, 0)),
        out_shape=jax.ShapeDtypeStruct((N_CHIPS, n_groups, shard_rows, group_dim), BF16),
        scratch_shapes=[pltpu.VMEM((group_dim, group_dim), F32)],
        compiler_params=_params("parallel", "arbitrary"),
    )(pooled, dpm)


def _mixers_backward(dpooled, dmixed, proj, conv_w, conv_b, gnorm_g, group_dim):
    _, seq, cols = proj.shape
    tl = CONV_HEAD_DIM
    first = cols // tl
    per_group = group_dim // tl
    n_groups = cols // group_dim

    def body(dp_ref, dm_ref, b_ref, c_ref, u_ref, w_ref, bias_ref, g_ref, o_ref, gg_ref, gb_ref, gw_ref):
        j = pl.program_id(0)
        for gi, window in enumerate(POOL_WINDOWS[:n_groups]):
            @pl.when(j // per_group == gi)
            def _(window=window):
                half = window // 2
                dp = dp_ref[...]
                t = lax.broadcasted_iota(jnp.int32, dp.shape, 0)
                dq = dp / _window_count(seq, half)
                o_ref[0] = (_window_sum(dq, range(-half + 1, half + 1), t) - dp).astype(BF16)

        bv, cu, prev, nxt, conv, w, t = _conv_parts(b_ref, c_ref, u_ref, w_ref, bias_ref)
        bo = bv * conv
        rb = lax.rsqrt(jnp.mean(bo * bo, axis=-1, keepdims=True) + NORM_EPS)
        bn = bo * rb
        dm = dm_ref[...]
        gg_ref[...] = jnp.sum(dm * bn, axis=0, keepdims=True)
        dbn = dm * g_ref[...]
        dbo = rb * (dbn - bn * jnp.mean(dbn * bn, axis=-1, keepdims=True))
        o_ref[1] = (dbo * conv).astype(BF16)
        dconv = dbo * bv
        gb_ref[...] = jnp.sum(dconv, axis=0, keepdims=True)
        gw_ref[0:1, :] = jnp.sum(dconv * prev, axis=0, keepdims=True)
        gw_ref[1:2, :] = jnp.sum(dconv * cu, axis=0, keepdims=True)
        gw_ref[2:3, :] = jnp.sum(dconv * nxt, axis=0, keepdims=True)
        dcu = w[0:1] * _shifted(dconv, 1, t) + w[1:2] * dconv + w[2:3] * _shifted(dconv, -1, t)
        o_ref[2] = (dcu * u_ref[...].astype(F32)).astype(BF16)
        o_ref[3] = (dcu * c_ref[...].astype(F32)).astype(BF16)

    def slab(s):
        return pl.BlockSpec((None, seq, tl), lambda j, s=s: (s, 0, j))

    vec = pl.BlockSpec((1, tl), lambda j: (0, j))
    rows3 = pl.BlockSpec((3, tl), lambda j: (0, j))
    vshape = jax.ShapeDtypeStruct((1, cols), F32)
    return pl.pallas_call(
        body, name="mixers_backward", grid=(cols // tl,),
        in_specs=[pl.BlockSpec((seq, tl), lambda j: (0, j)), pl.BlockSpec((seq, tl), lambda j: (0, first + j)),
                  slab(1), slab(2), slab(3), rows3, vec, vec],
        out_specs=[pl.BlockSpec((N_CHIPS, seq, tl), lambda j: (0, 0, j)), vec, vec, rows3],
        out_shape=[jax.ShapeDtypeStruct((N_CHIPS, seq, cols), BF16), vshape, vshape,
                   jax.ShapeDtypeStruct((3, cols), F32)],
        compiler_params=_params("parallel"),
    )(dpooled, dmixed, proj, proj, proj, conv_w, conv_b, gnorm_g)


class _GradReduction:
    def __init__(self, tag, grads, states, pair_id, scatter_id, position):
        self.tag, self.grads, self.states = tag, grads, states
        self.pair_id, self.scatter_id = pair_id, scatter_id
        self.chip, self.core, self.other_core = position

    def exchange(self):
        self.received = _exchange_halves(f"exchange_{self.tag}", self.grads, self.pair_id)

    def combine(self, after=()):
        self.parts = [_add_half(f"add_half_{self.tag}_{a}", g, r, self.core, after)
                      for a, (g, r) in enumerate(zip(self.grads, self.received))]

    def scatter(self):
        self.landed = _scatter_partials(f"scatter_{self.tag}", self.parts, self.scatter_id)

    def reduce(self, after=()):
        self.reduced = [_reduce_chips(f"reduce_chips_{self.tag}_{a}", p, l, self.chip, after)
                        for a, (p, l) in enumerate(zip(self.parts, self.landed))]

    def swap(self):
        self.swapped = _swap_reduced(f"swap_{self.tag}", self.reduced, self.pair_id)

    def update_mine(self):
        self.mine = [_adamw_half(f"adamw_mine_{self.tag}_{a}", g, *state, self.core)
                     for a, (g, state) in enumerate(zip(self.reduced, self.states))]

    def update_other(self, after=()):
        self.results = [_adamw_half(f"adamw_other_{self.tag}_{a}", g, *state, self.other_core, done, after)
                        for a, (g, state, done) in enumerate(zip(self.swapped, self.states, self.mine))]

    def token(self, stage):
        first = getattr(self, stage)[0]
        return first if not isinstance(first, (list, tuple)) else first[0]


def kernel(x, c, w_ada, b_ada, norm1_g, w_in, pool_mix_w, pool_scale, conv_w, conv_b, gnorm_pool_g, gnorm_conv_g, w_out, norm2_g, w_mlp_in, w_mlp_out, final_g, loss_target, m_w_ada, m_b_ada, m_norm1_g, m_w_in, m_pool_mix_w, m_pool_scale, m_conv_w, m_conv_b, m_gnorm_pool_g, m_gnorm_conv_g, m_w_out, m_norm2_g, m_w_mlp_in, m_w_mlp_out, m_final_g, v_w_ada, v_b_ada, v_norm1_g, v_w_in, v_pool_mix_w, v_pool_scale, v_conv_w, v_conv_b, v_gnorm_pool_g, v_gnorm_conv_g, v_w_out, v_norm2_g, v_w_mlp_in, v_w_mlp_out, v_final_g):
    seq, d_model = x.shape[1], x.shape[2]
    cols = w_in.shape[2]
    n_groups, group_dim = pool_mix_w.shape[1], pool_mix_w.shape[3]
    shard_rows = pool_mix_w.shape[2]
    ff_cols = w_mlp_in.shape[2]
    ada_cols = w_ada.shape[2]
    conv_shard = conv_w.shape[2]
    assert pool_scale.shape[1] == cols and conv_b.shape[1] == cols and n_groups * group_dim == cols
    assert cols % CONV_HEAD_DIM == 0 and group_dim % CONV_HEAD_DIM == 0 and shard_rows * N_CHIPS == group_dim

    ix, iy, ic = _position()
    chip = 2 * ix + iy
    me = 4 * ix + 2 * iy + ic
    position = tuple(jnp.reshape(v, (1,)).astype(jnp.int32) for v in (chip, ic, 1 - ic))

    xs, target = x[0], loss_target[0]
    final_row = final_g.reshape(1, d_model)

    small = _gather_flat("gather_cond", jnp.concatenate([c[0], conv_w[0].reshape(-1)]))
    c_all = small[:, :d_model]
    conv_w_full = jnp.concatenate(
        [small[2 * j, d_model:].reshape(3, conv_shard) for j in range(N_CHIPS)], axis=1)
    b_cols = lax.dynamic_slice_in_dim(b_ada, chip * ada_cols, ada_cols, axis=1)
    mod_part = _ada_forward(c_all, w_ada[0], b_cols)
    mod_all = _gather_flat("gather_mod", mod_part.reshape(-1)).reshape(N_DEV, N_DEV, ada_cols)
    mod = jnp.concatenate(
        [lax.dynamic_slice_in_dim(mod_all[2 * j], me, 1, axis=0) for j in range(N_CHIPS)], axis=1)
    shift1, scale1, gate1, shift2, scale2, gate2 = [mod[:, i * d_model:(i + 1) * d_model] for i in range(N_MOD)]

    shards = [w_in[0].astype(BF16), pool_mix_w[0].reshape(n_groups * shard_rows, group_dim).astype(BF16),
              w_out[0].astype(BF16), w_mlp_in[0].astype(BF16), w_mlp_out[0].astype(BF16)]
    wg_in, wg_pm = _gather_weights("gather_w_in", shards[0:2], 1)
    (wg_out,) = _gather_weights("gather_w_out", shards[2:3], 2)
    (wg_1,) = _gather_weights("gather_w_mlp_in", shards[3:4], 3)
    (wg_2,) = _gather_weights("gather_w_mlp_out", shards[4:5], 4)
    wg_in, wg_pm, wg_out, wg_1, wg_2 = [
        _place_own(f"place_own_{a}", g, s, position[0])
        for a, (g, s) in enumerate(zip([wg_in, wg_pm, wg_out, wg_1, wg_2], shards))]
    wg_pm = wg_pm.reshape(N_CHIPS, n_groups, shard_rows, group_dim)
    wg_out = wg_out.reshape(d_model, d_model)
    wg_2 = wg_2.reshape(N_CHIPS * ff_cols, d_model)

    h1 = _norm_modulate("norm_modulate_1", xs, norm1_g, scale1, shift1)
    proj = _in_projection(h1, wg_in)
    pooled = _pool_forward(proj, group_dim)
    mixed = _pool_mix_forward(pooled, wg_pm, pool_scale, gnorm_pool_g, d_model)
    mixed = _conv_forward(proj, conv_w_full, conv_b, gnorm_conv_g, mixed)
    x1, attn = _residual_projection("out_projection", mixed, wg_out, xs, gate1)
    h2 = _norm_modulate("norm_modulate_2", x1, norm2_g, scale2, shift2)
    act = _mlp_in(h2, wg_1)
    x2, mlp = _residual_projection("mlp_out", act, wg_2, x1, gate2, prologue=_square)

    dx2, dmlp, g_final, dgate2, sq_err = _loss_head(x2, target, final_row, mlp, gate2)
    dhid = _grad_hidden(dmlp, wg_2, act)
    gw_2 = _weight_grad("grad_w_mlp_out", act, dmlp, prologue=_square)
    red_2 = _GradReduction("w_mlp_out", [gw_2.reshape(N_CHIPS, ff_cols, d_model)],
                           [(w_mlp_out[0], m_w_mlp_out[0], v_w_mlp_out[0])], 8, 12, position)
    red_2.exchange()
    gw_1 = _weight_grad_slabs("grad_w_mlp_in", h2, dhid, ff_cols)
    red_1 = _GradReduction("w_mlp_in", [gw_1], [(w_mlp_in[0], m_w_mlp_in[0], v_w_mlp_in[0])], 7, 11, position)
    red_1.exchange()
    red_2.combine()
    red_2.scatter()
    dh2 = _grad_input_slabs("grad_h2", dhid, wg_1, after=[red_2.token("parts")])
    red_1.combine()
    red_1.scatter()
    dx1, dscale2, dshift2, g_norm2, dgate1, dattn = _norm_modulate_backward(
        "norm_modulate_backward_2", dh2, x1, norm2_g, scale2, dx2, attn, gate1, after=[red_1.token("parts")])
    gw_out = _weight_grad("grad_w_out", mixed, dattn)
    red_out = _GradReduction("w_out", [gw_out.reshape(N_CHIPS, d_model // N_CHIPS, d_model)],
                             [(w_out[0], m_w_out[0], v_w_out[0])], 6, 10, position)
    red_out.exchange()
    dmixed = _grad_input("grad_mixed", dattn, wg_out)
    red_2.reduce(after=[dmixed])
    red_2.swap()
    red_2.update_mine()
    dpooled, dpm, g_gpool, g_pscale = _pool_mix_backward(
        dmixed, pooled, wg_pm, pool_scale, gnorm_pool_g, after=[red_2.token("mine")])
    gw_pm = _pool_mix_weight_grad(pooled, dpm, n_groups)
    dproj, g_gconv, g_convb, g_convw = _mixers_backward(
        dpooled, dmixed, proj, conv_w_full, conv_b, gnorm_conv_g, group_dim)
    red_1.reduce(after=[dproj])
    red_1.swap()
    red_1.update_mine()
    red_out.combine(after=[red_1.token("mine")])
    red_out.scatter()
    gw_in = _weight_grad_slabs("grad_w_in", h1, dproj, None, after=[red_out.token("parts")])
    pm2d = (n_groups * shard_rows, group_dim)
    red_in = _GradReduction(
        "w_in", [gw_in, gw_pm.reshape((N_CHIPS,) + pm2d)],
        [(w_in[0], m_w_in[0], v_w_in[0]),
         (pool_mix_w[0].reshape(pm2d), m_pool_mix_w[0].reshape(pm2d), v_pool_mix_w[0].reshape(pm2d))],
        5, 9, position)
    red_in.exchange()
    red_2.update_other(after=[gw_in])
    red_out.reduce(after=[red_2.token("results")])
    red_out.swap()
    red_out.update_mine()
    red_in.combine(after=[red_out.token("mine")])
    red_in.scatter()
    dh1 = _grad_input_slabs("grad_h1", dproj, wg_in, after=[red_in.token("parts")])
    red_1.update_other(after=[dh1])
    grad_x, dscale1, dshift1, g_norm1 = _norm_modulate_backward(
        "norm_modulate_backward_1", dh1, xs, norm1_g, scale1, dx1, after=[red_1.token("results")])
    red_out.update_other(after=[grad_x])

    mine = jnp.concatenate(
        [dshift1, dscale1, dgate1, dshift2, dscale2, dgate2, g_norm1, g_norm2, g_final, sq_err,
         g_pscale, g_convb, g_gpool, g_gconv, g_convw.reshape(1, 3 * cols)], axis=1)
    gathered = _gather_flat("gather_small", mine.reshape(-1))
    sums, loss = _reduce_small(gathered, d_model)
    n_rep = (N_MOD + 3) * d_model
    g_rep = jnp.concatenate([sums[:, :n_rep], sums[:, n_rep + d_model:n_rep + d_model + 4 * cols]], axis=1)
    n_small = g_rep.shape[1]

    def pack(b, n1, n2, fg, ps, cb, gp, gc):
        return jnp.concatenate([b, n1, n2, fg.reshape(1, d_model), ps, cb, gp, gc], axis=1).reshape(8, n_small // 8)

    d_rep, m_rep, v_rep = _adamw(
        "adamw_small", g_rep.reshape(8, n_small // 8),
        pack(b_ada, norm1_g, norm2_g, final_g, pool_scale, conv_b, gnorm_pool_g, gnorm_conv_g),
        pack(m_b_ada, m_norm1_g, m_norm2_g, m_final_g, m_pool_scale, m_conv_b, m_gnorm_pool_g, m_gnorm_conv_g),
        pack(v_b_ada, v_norm1_g, v_norm2_g, v_final_g, v_pool_scale, v_conv_b, v_gnorm_pool_g, v_gnorm_conv_g))

    def unpack(flat):
        flat = flat.reshape(1, n_small)
        sizes = [N_MOD * d_model, d_model, d_model, d_model, cols, cols, cols, cols]
        parts, at = [], 0
        for size in sizes:
            parts.append(flat[:, at:at + size])
            at += size
        parts[3] = parts[3].reshape(d_model)
        return parts

    g_convw_full = sums[:, n_rep + d_model + 4 * cols:].reshape(3, cols)
    g_convw_mine = lax.dynamic_slice_in_dim(g_convw_full, chip * conv_shard, conv_shard, axis=1)
    d_convw, m_convw, v_convw = _adamw("adamw_conv_w", g_convw_mine, conv_w[0], m_conv_w[0], v_conv_w[0])

    dmod_cols = lax.dynamic_slice_in_dim(gathered[:, :N_MOD * d_model], chip * ada_cols, ada_cols, axis=1)
    g_ada, d_ada, mn_ada, vn_ada = _ada_backward(c_all.T, dmod_cols, w_ada[0], m_w_ada[0], v_w_ada[0])

    red_in.reduce(after=[g_ada, red_out.token("results")])
    red_in.swap()
    red_in.update_mine()
    red_in.update_other()

    small_parts = [unpack(g_rep), unpack(d_rep), unpack(m_rep), unpack(v_rep)]
    ada_parts = [g_ada, d_ada, mn_ada, vn_ada]
    convw_parts = [g_convw_mine, d_convw, m_convw, v_convw]

    def ordered(k):
        b, n1, n2, fg, ps, cb, gp, gc = small_parts[k]
        return [ada_parts[k][None], b, n1, red_in.results[0][k][None],
                red_in.results[1][k].reshape(pool_mix_w.shape), ps, convw_parts[k][None], cb, gp, gc,
                red_out.results[0][k][None], n2, red_1.results[0][k][None], red_2.results[0][k][None], fg]

    return (loss[0, 0], grad_x[None], *ordered(0), *ordered(1), *ordered(2), *ordered(3))
```

```python
import jax
import jax.numpy as jnp
from jax import lax
from jax.experimental import pallas as pl
from jax.experimental.pallas import tpu as pltpu
from jax.experimental.pallas import tpu_sc as plsc

F32 = jnp.float32
BF16 = jnp.bfloat16
MESH = pl.DeviceIdType.MESH
ANY = pl.BlockSpec(memory_space=pl.ANY)

NORM_EPS = 1e-6
POOL_WINDOWS = (2, 4, 8, 16)
CONV_HEAD_DIM = 128
N_MOD = 6
N_CHIPS = 4
N_DEV = 8

ADAM_LR = 0.001
ADAM_B1 = 0.9
ADAM_B2 = 0.999
ADAM_EPS = 1e-08
ADAM_WD = 0.01
ADAM_STEP = 10

VMEM_LIMIT_BYTES = 56 * 1024 * 1024

NN = (((1,), (0,)), ((), ()))
NT = (((1,), (1,)), ((), ()))
TN = (((0,), (0,)), ((), ()))


def _tile(n, pref):
    t = min(n, pref)
    while n % t:
        t //= 2
    return t


def _params(*sem):
    return pltpu.CompilerParams(dimension_semantics=sem, vmem_limit_bytes=VMEM_LIMIT_BYTES)


def _position():
    return lax.axis_index("x"), lax.axis_index("y"), lax.axis_index("c")


def _flip(ix, iy, ic, mask):
    return (1 - ix if mask & 4 else ix, 1 - iy if mask & 2 else iy, 1 - ic if mask & 1 else ic)


def _allgather8(name, blk):
    rows, cols = blk.shape

    def body(x_ref, out_ref, send_sems, recv_sems, local_sem):
        ix, iy, ic = _position()
        me = 4 * ix + 2 * iy + ic
        mine = pltpu.make_async_copy(x_ref, out_ref.at[me], local_sem)
        mine.start()
        sends = []
        for mask in range(1, N_DEV):
            cp = pltpu.make_async_remote_copy(
                src_ref=x_ref, dst_ref=out_ref.at[me],
                send_sem=send_sems.at[mask - 1], recv_sem=recv_sems.at[mask - 1],
                device_id=_flip(ix, iy, ic, mask), device_id_type=MESH)
            cp.start()
            sends.append(cp)
        for mask in range(1, N_DEV):
            px, py, pc = _flip(ix, iy, ic, mask)
            pltpu.make_async_remote_copy(
                src_ref=x_ref, dst_ref=out_ref.at[4 * px + 2 * py + pc],
                send_sem=send_sems.at[mask - 1], recv_sem=recv_sems.at[mask - 1],
                device_id=(px, py, pc), device_id_type=MESH).wait_recv()
        for cp in sends:
            cp.wait_send()
        mine.wait()

    return pl.pallas_call(
        body, name=name,
        out_shape=jax.ShapeDtypeStruct((N_DEV, rows, cols), F32),
        in_specs=[pl.BlockSpec(memory_space=pltpu.VMEM)],
        out_specs=pl.BlockSpec(memory_space=pltpu.VMEM),
        scratch_shapes=[pltpu.SemaphoreType.DMA((N_DEV - 1,)), pltpu.SemaphoreType.DMA((N_DEV - 1,)),
                        pltpu.SemaphoreType.DMA],
    )(blk)


def _gather_flat(name, vec):
    n = vec.shape[0]
    npad = -(-n // 1024) * 1024
    blk = jnp.pad(vec, (0, npad - n)).reshape(8, npad // 8)
    return _allgather8(name, blk).reshape(N_DEV, npad)[:, :n]


def _chip_relations(ix, iy):
    return [(1 - ix, iy), (ix, 1 - iy), (1 - ix, 1 - iy)]


def _gather_weights(name, shards, collective_id):
    n = len(shards)

    def body(*refs):
        src, out = refs[:n], refs[n:2 * n]
        send_sems, recv_sems = refs[2 * n:]
        ix, iy, ic = _position()
        chip, chip_x, chip_y, chip_d = 2 * ix + iy, 2 * (1 - ix) + iy, 2 * ix + 1 - iy, 2 * (1 - ix) + 1 - iy
        beside_x, beside_y, sibling = (1 - ix, iy, ic), (ix, 1 - iy, ic), (ix, iy, 1 - ic)

        _handshake([beside_x, beside_y, sibling])

        def rows(a, core, quarter=None):
            half = shards[a].shape[0] // 2
            if quarter is None:
                return pl.ds(core * half, half)
            return pl.ds(core * half + quarter * (half // 2), half // 2)

        def copy(a, k, src_ref, dst_ref, to):
            return pltpu.make_async_remote_copy(
                src_ref=src_ref, dst_ref=dst_ref, send_sem=send_sems.at[8 * a + k], recv_sem=recv_sems.at[8 * a + k],
                device_id=to, device_id_type=MESH)

        def relay(a, k, piece, to):
            return copy(a, k, out[a].at[piece], out[a].at[piece], to)

        started = []

        def start(cp):
            cp.start()
            started.append(cp)

        for a in range(n):
            mine = src[a].at[rows(a, ic)]
            start(copy(a, 0, mine, out[a].at[chip, rows(a, ic)], beside_x))
            start(copy(a, 1, mine, out[a].at[chip, rows(a, ic)], beside_y))
        for a in range(n):
            relay(a, 0, (chip_x, rows(a, ic)), beside_x).wait_recv()
            start(relay(a, 3, (chip_x, rows(a, ic, 1)), beside_y))
            start(relay(a, 4, (chip_x, rows(a, ic)), sibling))
            relay(a, 1, (chip_y, rows(a, ic)), beside_y).wait_recv()
            start(relay(a, 2, (chip_y, rows(a, ic, 0)), beside_x))
            start(relay(a, 5, (chip_y, rows(a, ic)), sibling))
        for a in range(n):
            relay(a, 2, (chip_d, rows(a, ic, 0)), beside_x).wait_recv()
            start(relay(a, 6, (chip_d, rows(a, ic, 0)), sibling))
            relay(a, 3, (chip_d, rows(a, ic, 1)), beside_y).wait_recv()
            start(relay(a, 7, (chip_d, rows(a, ic, 1)), sibling))
        for a in range(n):
            relay(a, 4, (chip_x, rows(a, 1 - ic)), sibling).wait_recv()
            relay(a, 5, (chip_y, rows(a, 1 - ic)), sibling).wait_recv()
            relay(a, 6, (chip_d, rows(a, 1 - ic, 0)), sibling).wait_recv()
            relay(a, 7, (chip_d, rows(a, 1 - ic, 1)), sibling).wait_recv()
        for cp in started:
            cp.wait_send()

    out_type = [jax.ShapeDtypeStruct((N_CHIPS,) + s.shape, s.dtype) for s in shards]
    return _sequencer_call(name, body, shards, out_type, [8 * n, 8 * n], collective_id)


def _place_own(name, gathered, shard, chip):
    rows, cols = shard.shape
    tr, tc = _tile(rows, 512), _tile(cols, 2048)

    def body(chip_ref, own_ref, gathered_ref, o_ref):
        o_ref[...] = own_ref[...]

    return pl.pallas_call(
        body, name=name,
        grid_spec=pltpu.PrefetchScalarGridSpec(
            num_scalar_prefetch=1, grid=(rows // tr, cols // tc),
            in_specs=[pl.BlockSpec((tr, tc), lambda i, j, chip_ref: (i, j)), ANY],
            out_specs=pl.BlockSpec((None, tr, tc), lambda i, j, chip_ref: (chip_ref[0], i, j))),
        out_shape=jax.ShapeDtypeStruct(gathered.shape, gathered.dtype),
        input_output_aliases={2: 0},
        compiler_params=_params("parallel", "parallel"),
    )(chip, shard, gathered)


def _sequencer_call(name, body, operands, out_type, sem_counts, collective_id):
    return pl.kernel(
        body, name=name, out_type=out_type,
        mesh=plsc.ScalarSubcoreMesh(axis_name="sequencer", num_cores=1),
        scratch_types=[pltpu.SemaphoreType.DMA((n,)) for n in sem_counts],
        compiler_params=pltpu.CompilerParams(collective_id=collective_id),
    )(*operands)


def _handshake(peers):
    barrier = pltpu.get_barrier_semaphore()
    for peer in peers:
        pl.semaphore_signal(barrier, inc=1, device_id=peer, device_id_type=MESH)
    pl.semaphore_wait(barrier, len(peers))


def _exchange_halves(name, grads, collective_id):
    n = len(grads)

    def body(*refs):
        src, out = refs[:n], refs[n:2 * n]
        send_sems, recv_sems = refs[2 * n:]
        ix, iy, ic = _position()
        sibling = (ix, iy, 1 - ic)
        _handshake([sibling])
        copies = []
        for a in range(n):
            half = grads[a].shape[1] // 2
            cp = pltpu.make_async_remote_copy(
                src_ref=src[a].at[pl.ds(0, N_CHIPS), pl.ds((1 - ic) * half, half)], dst_ref=out[a],
                send_sem=send_sems.at[a], recv_sem=recv_sems.at[a],
                device_id=sibling, device_id_type=MESH)
            cp.start()
            copies.append(cp)
        for cp in copies:
            cp.wait()

    out_type = [jax.ShapeDtypeStruct((N_CHIPS, g.shape[1] // 2, g.shape[2]), g.dtype) for g in grads]
    return _sequencer_call(name, body, grads, out_type, [n, n], collective_id)


def _scatter_partials(name, parts, collective_id):
    n = len(parts)

    def body(*refs):
        src, out = refs[:n], refs[n:2 * n]
        send_sems, recv_sems = refs[2 * n:]
        ix, iy, ic = _position()
        rels = _chip_relations(ix, iy)
        _handshake([(px, py, ic) for px, py in rels])
        copies = []
        for a in range(n):
            for r, (px, py) in enumerate(rels):
                cp = pltpu.make_async_remote_copy(
                    src_ref=src[a].at[2 * px + py], dst_ref=out[a].at[r],
                    send_sem=send_sems.at[3 * a + r], recv_sem=recv_sems.at[3 * a + r],
                    device_id=(px, py, ic), device_id_type=MESH)
                cp.start()
                copies.append(cp)
        for cp in copies:
            cp.wait()

    out_type = [jax.ShapeDtypeStruct((3,) + p.shape[1:], p.dtype) for p in parts]
    return _sequencer_call(name, body, parts, out_type, [3 * n, 3 * n], collective_id)


def _swap_reduced(name, reduced, collective_id):
    n = len(reduced)

    def body(*refs):
        src, out = refs[:n], refs[n:2 * n]
        send_sems, recv_sems = refs[2 * n:]
        ix, iy, ic = _position()
        sibling = (ix, iy, 1 - ic)
        _handshake([sibling])
        copies = []
        for a in range(n):
            cp = pltpu.make_async_remote_copy(
                src_ref=src[a], dst_ref=out[a], send_sem=send_sems.at[a], recv_sem=recv_sems.at[a],
                device_id=sibling, device_id_type=MESH)
            cp.start()
            copies.append(cp)
        for cp in copies:
            cp.wait()

    out_type = [jax.ShapeDtypeStruct(r.shape, r.dtype) for r in reduced]
    return _sequencer_call(name, body, reduced, out_type, [n, n], collective_id)


def _add_half(name, grad, recv, core, after=()):
    _, rows, cols = grad.shape
    half = rows // 2
    tr, tc = _tile(half, 512), _tile(cols, 2048)
    nbr = half // tr

    def body(core_ref, g_ref, r_ref, *rest):
        rest[-1][...] = g_ref[...] + r_ref[...]

    return pl.pallas_call(
        body, name=name,
        grid_spec=pltpu.PrefetchScalarGridSpec(
            num_scalar_prefetch=1, grid=(N_CHIPS, nbr, cols // tc),
            in_specs=[pl.BlockSpec((None, tr, tc), lambda s, i, j, core_ref: (s, core_ref[0] * nbr + i, j)),
                      pl.BlockSpec((None, tr, tc), lambda s, i, j, core_ref: (s, i, j))] + [ANY] * len(after),
            out_specs=pl.BlockSpec((None, tr, tc), lambda s, i, j, core_ref: (s, i, j))),
        out_shape=jax.ShapeDtypeStruct((N_CHIPS, half, cols), BF16),
        compiler_params=_params("parallel", "parallel", "parallel"),
    )(core, grad, recv, *after)


def _reduce_chips(name, part, recv, chip, after=()):
    _, half, cols = part.shape
    tr, tc = _tile(half, 512), _tile(cols, 2048)

    def body(chip_ref, p_ref, r_ref, *rest):
        o_ref = rest[-1]
        acc = p_ref[...].astype(F32)
        for r in range(3):
            acc = acc + r_ref[r].astype(F32)
        o_ref[...] = acc

    return pl.pallas_call(
        body, name=name,
        grid_spec=pltpu.PrefetchScalarGridSpec(
            num_scalar_prefetch=1, grid=(half // tr, cols // tc),
            in_specs=[pl.BlockSpec((None, tr, tc), lambda i, j, chip_ref: (chip_ref[0], i, j)),
                      pl.BlockSpec((3, tr, tc), lambda i, j, chip_ref: (0, i, j))] + [ANY] * len(after),
            out_specs=pl.BlockSpec((tr, tc), lambda i, j, chip_ref: (i, j))),
        out_shape=jax.ShapeDtypeStruct((half, cols), F32),
        compiler_params=_params("parallel", "parallel"),
    )(chip, part, recv, *after)


def _adamw_half(name, g_half, w, m, v, which, done=None, after=()):
    half, cols = g_half.shape
    tr, tc = _tile(half, 256), _tile(cols, 2048)
    nbr = half // tr
    bc1 = 1.0 - ADAM_B1 ** ADAM_STEP
    bc2 = 1.0 - ADAM_B2 ** ADAM_STEP

    def body(which_ref, g_ref, w_ref, m_ref, v_ref, *rest):
        go_ref, d_ref, mo_ref, vo_ref = rest[-4:]
        gv = g_ref[...]
        mn = ADAM_B1 * m_ref[...] + (1.0 - ADAM_B1) * gv
        vn = ADAM_B2 * v_ref[...] + (1.0 - ADAM_B2) * (gv * gv)
        go_ref[...] = gv
        d_ref[...] = -ADAM_LR * ((mn / bc1) / (jnp.sqrt(vn / bc2) + ADAM_EPS) + ADAM_WD * w_ref[...])
        mo_ref[...] = mn
        vo_ref[...] = vn

    mine = pl.BlockSpec((tr, tc), lambda i, j, which_ref: (which_ref[0] * nbr + i, j))
    kept = [] if done is None else list(done)
    shape = jax.ShapeDtypeStruct((2 * half, cols), F32)
    return pl.pallas_call(
        body, name=name,
        grid_spec=pltpu.PrefetchScalarGridSpec(
            num_scalar_prefetch=1, grid=(nbr, cols // tc),
            in_specs=([pl.BlockSpec((tr, tc), lambda i, j, which_ref: (i, j)), mine, mine, mine]
                      + [ANY] * (len(kept) + len(after))),
            out_specs=[mine] * 4),
        out_shape=[shape] * 4,
        input_output_aliases={5 + k: k for k in range(len(kept))},
        compiler_params=_params("parallel", "parallel"),
    )(which, g_half, w, m, v, *kept, *after)


def _adamw(name, g, w, m, v):
    rows, cols = g.shape
    tr, tc = _tile(rows, 256), _tile(cols, 2048)
    bc1 = 1.0 - ADAM_B1 ** ADAM_STEP
    bc2 = 1.0 - ADAM_B2 ** ADAM_STEP

    def body(g_ref, w_ref, m_ref, v_ref, d_ref, mo_ref, vo_ref):
        gv = g_ref[...]
        mn = ADAM_B1 * m_ref[...] + (1.0 - ADAM_B1) * gv
        vn = ADAM_B2 * v_ref[...] + (1.0 - ADAM_B2) * (gv * gv)
        d_ref[...] = -ADAM_LR * ((mn / bc1) / (jnp.sqrt(vn / bc2) + ADAM_EPS) + ADAM_WD * w_ref[...])
        mo_ref[...] = mn
        vo_ref[...] = vn

    spec = pl.BlockSpec((tr, tc), lambda i, j: (i, j))
    shape = jax.ShapeDtypeStruct((rows, cols), F32)
    return pl.pallas_call(
        body, name=name, grid=(rows // tr, cols // tc),
        in_specs=[spec] * 4, out_specs=[spec] * 3, out_shape=[shape] * 3,
        compiler_params=_params("parallel", "parallel"),
    )(g, w, m, v)


def _reduce_small(gathered, d_model):
    n = gathered.shape[1]
    loss_at = (N_MOD + 3) * d_model

    def body(g_ref, s_ref, loss_ref):
        acc = g_ref[0:1, :]
        for d in range(1, N_DEV):
            acc = acc + g_ref[d:d + 1, :]
        s_ref[...] = acc
        lanes = acc[:, loss_at:loss_at + d_model]
        loss_ref[...] = jnp.broadcast_to((0.5 / d_model) * jnp.sum(lanes, axis=1, keepdims=True), loss_ref.shape)

    return pl.pallas_call(
        body, name="reduce_small",
        out_shape=[jax.ShapeDtypeStruct((1, n), F32), jax.ShapeDtypeStruct((1, 128), F32)],
        compiler_params=pltpu.CompilerParams(vmem_limit_bytes=VMEM_LIMIT_BYTES),
    )(gathered)


def _ada_forward(c_all, w_ada, b_cols):
    d_model, width = w_ada.shape
    tn = _tile(width, 512)

    def body(c_ref, w_ref, b_ref, o_ref):
        cv = c_ref[...]
        act = cv * jax.nn.sigmoid(cv)
        o_ref[...] = lax.dot_general(act, w_ref[...], NN, precision=lax.Precision.HIGHEST,
                                     preferred_element_type=F32) + b_ref[...]

    return pl.pallas_call(
        body, name="ada_forward", grid=(width // tn,),
        in_specs=[pl.BlockSpec((N_DEV, d_model), lambda j: (0, 0)),
                  pl.BlockSpec((d_model, tn), lambda j: (0, j)),
                  pl.BlockSpec((1, tn), lambda j: (0, j))],
        out_specs=pl.BlockSpec((N_DEV, tn), lambda j: (0, j)),
        out_shape=jax.ShapeDtypeStruct((N_DEV, width), F32),
        compiler_params=_params("parallel"),
    )(c_all, w_ada, b_cols)


def _ada_backward(c_all_t, dmod_cols, w, m, v):
    d_model, width = w.shape
    tr, tc = _tile(d_model, 256), _tile(width, 1536)
    bc1 = 1.0 - ADAM_B1 ** ADAM_STEP
    bc2 = 1.0 - ADAM_B2 ** ADAM_STEP

    def body(c_ref, dm_ref, w_ref, m_ref, v_ref, g_ref, d_ref, mo_ref, vo_ref):
        cv = c_ref[...]
        act = cv * jax.nn.sigmoid(cv)
        gv = lax.dot_general(act, dm_ref[...], NN, precision=lax.Precision.HIGHEST, preferred_element_type=F32)
        mn = ADAM_B1 * m_ref[...] + (1.0 - ADAM_B1) * gv
        vn = ADAM_B2 * v_ref[...] + (1.0 - ADAM_B2) * (gv * gv)
        g_ref[...] = gv
        d_ref[...] = -ADAM_LR * ((mn / bc1) / (jnp.sqrt(vn / bc2) + ADAM_EPS) + ADAM_WD * w_ref[...])
        mo_ref[...] = mn
        vo_ref[...] = vn

    spec = pl.BlockSpec((tr, tc), lambda i, j: (i, j))
    shape = jax.ShapeDtypeStruct((d_model, width), F32)
    return pl.pallas_call(
        body, name="ada_backward", grid=(d_model // tr, width // tc),
        in_specs=[pl.BlockSpec((tr, N_DEV), lambda i, j: (i, 0)),
                  pl.BlockSpec((N_DEV, tc), lambda i, j: (0, j)), spec, spec, spec],
        out_specs=[spec] * 4, out_shape=[shape] * 4,
        compiler_params=_params("parallel", "parallel"),
    )(c_all_t, dmod_cols, w, m, v)


def _matmul(name, a, b, extras, *, grid, tiles, dims, a_spec, b_spec, extra_specs, out_shape, out_specs,
            epilogue, prologue=None, after=()):
    tm, tn, _ = tiles
    gm, gn, gk = grid
    n_extra, n_out = len(extras), len(out_shape)
    first_out = 2 + n_extra + len(after)

    def product(a_ref, b_ref):
        av = a_ref[...]
        if prologue is not None:
            av = prologue(av)
        return lax.dot_general(av, b_ref[...], dims, preferred_element_type=F32)

    def body_single(*refs):
        epilogue(product(refs[0], refs[1]), refs[2:2 + n_extra], refs[first_out:first_out + n_out])

    def body(*refs):
        a_ref, b_ref = refs[0], refs[1]
        extra_refs = refs[2:2 + n_extra]
        out_refs = refs[first_out:first_out + n_out]
        acc_ref = refs[-1]
        k = pl.program_id(2)

        @pl.when(k == 0)
        def _():
            acc_ref[...] = jnp.zeros_like(acc_ref)

        acc_ref[...] += product(a_ref, b_ref)

        @pl.when(k == gk - 1)
        def _():
            epilogue(acc_ref[...], extra_refs, out_refs)

    single = gk == 1
    return pl.pallas_call(
        body_single if single else body, name=name, grid=(gm, gn, gk),
        in_specs=[a_spec, b_spec, *extra_specs] + [ANY] * len(after), out_specs=out_specs, out_shape=out_shape,
        scratch_shapes=[] if single else [pltpu.VMEM((tm, tn), F32)],
        compiler_params=_params("parallel", "parallel", "arbitrary"),
    )(a, b, *extras, *after)


def _store(dtype):
    def epilogue(acc, extra_refs, out_refs):
        out_refs[0][...] = acc.astype(dtype)
    return epilogue


def _residual_epilogue(acc, extra_refs, out_refs):
    res_ref, gate_ref = extra_refs
    out_refs[0][...] = res_ref[...] + gate_ref[...] * acc
    out_refs[1][...] = acc.astype(BF16)


def _square(av):
    af = av.astype(F32)
    return (af * af).astype(BF16)


MM_TILE_M = 1024
MM_TILE_N = 1024
MM_WHOLE_K = 4096
MM_TILE_K = 4096


def _mm_tiles(m, n, k, tn_pref=MM_TILE_N, k_block=None):
    tk = k if k <= MM_WHOLE_K else MM_TILE_K
    if k_block is not None:
        tk = min(tk, k_block)
    return _tile(m, MM_TILE_M), _tile(n, tn_pref), _tile(k, tk)


def _in_projection(h, w_slabs):
    seq, d_model = h.shape
    _, _, cols = w_slabs.shape
    tm, tn, tk = _mm_tiles(seq, cols, d_model)
    nbj = cols // tn
    return _matmul(
        "in_projection", h, w_slabs, (), grid=(seq // tm, N_CHIPS * nbj, d_model // tk), tiles=(tm, tn, tk), dims=NN,
        a_spec=pl.BlockSpec((tm, tk), lambda i, j, k: (i, k)),
        b_spec=pl.BlockSpec((None, tk, tn), lambda i, j, k: (j // nbj, k, j % nbj)),
        extra_specs=(),
        out_shape=[jax.ShapeDtypeStruct((N_CHIPS, seq, cols), BF16)],
        out_specs=[pl.BlockSpec((None, tm, tn), lambda i, j, k: (j // nbj, i, j % nbj))],
        epilogue=_store(BF16))[0]


def _residual_projection(name, a, w, res, gate, prologue=None):
    seq, kdim = a.shape
    d_model = w.shape[1]
    if kdim <= MM_WHOLE_K:
        tm, tn, tk = _mm_tiles(seq, d_model, kdim, tn_pref=MM_TILE_N // 2)
    else:
        tm, tn, tk = _mm_tiles(seq, d_model, kdim, k_block=MM_TILE_K // 2)
    tile = pl.BlockSpec((tm, tn), lambda i, j, k: (i, j))
    return _matmul(
        name, a, w, (res, gate), grid=(seq // tm, d_model // tn, kdim // tk), tiles=(tm, tn, tk), dims=NN,
        a_spec=pl.BlockSpec((tm, tk), lambda i, j, k: (i, k)),
        b_spec=pl.BlockSpec((tk, tn), lambda i, j, k: (k, j)),
        extra_specs=(tile, pl.BlockSpec((1, tn), lambda i, j, k: (0, j))),
        out_shape=[jax.ShapeDtypeStruct((seq, d_model), F32), jax.ShapeDtypeStruct((seq, d_model), BF16)],
        out_specs=[tile, tile],
        epilogue=_residual_epilogue, prologue=prologue)


def _mlp_in(h, w_slabs):
    seq, d_model = h.shape
    _, _, cols = w_slabs.shape
    tm, tn, tk = _mm_tiles(seq, cols, d_model)
    nbj = cols // tn

    def epilogue(acc, extra_refs, out_refs):
        out_refs[0][...] = jnp.maximum(acc, 0.0).astype(BF16)

    return _matmul(
        "mlp_in", h, w_slabs, (), grid=(seq // tm, N_CHIPS * nbj, d_model // tk), tiles=(tm, tn, tk), dims=NN,
        a_spec=pl.BlockSpec((tm, tk), lambda i, j, k: (i, k)),
        b_spec=pl.BlockSpec((None, tk, tn), lambda i, j, k: (j // nbj, k, j % nbj)),
        extra_specs=(),
        out_shape=[jax.ShapeDtypeStruct((seq, N_CHIPS * cols), BF16)],
        out_specs=[pl.BlockSpec((tm, tn), lambda i, j, k: (i, j))],
        epilogue=epilogue)[0]


def _grad_hidden(dmlp, w2, act, after=()):
    seq, d_model = dmlp.shape
    ff = w2.shape[0]
    tm, tn, tk = _mm_tiles(seq, ff, d_model)

    def epilogue(acc, extra_refs, out_refs):
        out_refs[0][...] = (acc * (2.0 * extra_refs[0][...].astype(F32))).astype(BF16)

    tile = pl.BlockSpec((tm, tn), lambda i, j, k: (i, j))
    return _matmul(
        "grad_hidden", dmlp, w2, (act,), grid=(seq // tm, ff // tn, d_model // tk), tiles=(tm, tn, tk), dims=NT,
        a_spec=pl.BlockSpec((tm, tk), lambda i, j, k: (i, k)),
        b_spec=pl.BlockSpec((tn, tk), lambda i, j, k: (j, k)),
        extra_specs=(tile,),
        out_shape=[jax.ShapeDtypeStruct((seq, ff), BF16)], out_specs=[tile],
        epilogue=epilogue, after=after)[0]


def _weight_grad(name, a, b, prologue=None):
    seq, m = a.shape
    n = b.shape[1]
    tm, tn, tk = _mm_tiles(m, n, seq)
    return _matmul(
        name, a, b, (), grid=(m // tm, n // tn, seq // tk), tiles=(tm, tn, tk), dims=TN,
        a_spec=pl.BlockSpec((tk, tm), lambda i, j, k: (k, i)),
        b_spec=pl.BlockSpec((tk, tn), lambda i, j, k: (k, j)),
        extra_specs=(),
        out_shape=[jax.ShapeDtypeStruct((m, n), BF16)],
        out_specs=[pl.BlockSpec((tm, tn), lambda i, j, k: (i, j))],
        epilogue=_store(BF16), prologue=prologue)[0]


def _weight_grad_slabs(name, a, b, slab_cols, after=()):
    seq, m = a.shape
    cols = b.shape[2] if slab_cols is None else slab_cols
    tm, tn, tk = _mm_tiles(m, cols, seq)
    nbj = cols // tn
    if slab_cols is None:
        b_spec = pl.BlockSpec((None, tk, tn), lambda i, j, k: (j // nbj, k, j % nbj))
    else:
        b_spec = pl.BlockSpec((tk, tn), lambda i, j, k: (k, j))
    return _matmul(
        name, a, b, (), grid=(m // tm, N_CHIPS * nbj, seq // tk), tiles=(tm, tn, tk), dims=TN,
        a_spec=pl.BlockSpec((tk, tm), lambda i, j, k: (k, i)),
        b_spec=b_spec, extra_specs=(),
        out_shape=[jax.ShapeDtypeStruct((N_CHIPS, m, cols), BF16)],
        out_specs=[pl.BlockSpec((None, tm, tn), lambda i, j, k: (j // nbj, i, j % nbj))],
        epilogue=_store(BF16), after=after)[0]


def _grad_input_slabs(name, dy, w_slabs, after=()):
    _, d_model, cols = w_slabs.shape
    seq = dy.shape[1] if dy.ndim == 3 else dy.shape[0]
    tm, tn, tk = _mm_tiles(seq, d_model, N_CHIPS * cols, k_block=cols)
    nbk = cols // tk
    if dy.ndim == 3:
        a_spec = pl.BlockSpec((None, tm, tk), lambda i, j, k: (k // nbk, i, k % nbk))
    else:
        a_spec = pl.BlockSpec((tm, tk), lambda i, j, k: (i, k))
    return _matmul(
        name, dy, w_slabs, (), grid=(seq // tm, d_model // tn, N_CHIPS * nbk), tiles=(tm, tn, tk), dims=NT,
        a_spec=a_spec,
        b_spec=pl.BlockSpec((None, tn, tk), lambda i, j, k: (k // nbk, j, k % nbk)),
        extra_specs=(),
        out_shape=[jax.ShapeDtypeStruct((seq, d_model), F32)],
        out_specs=[pl.BlockSpec((tm, tn), lambda i, j, k: (i, j))],
        epilogue=_store(F32), after=after)[0]


def _grad_input(name, dy, w):
    seq, n = dy.shape
    kdim = w.shape[0]
    tm, tn, tk = _mm_tiles(seq, kdim, n)
    return _matmul(
        name, dy, w, (), grid=(seq // tm, kdim // tn, n // tk), tiles=(tm, tn, tk), dims=NT,
        a_spec=pl.BlockSpec((tm, tk), lambda i, j, k: (i, k)),
        b_spec=pl.BlockSpec((tn, tk), lambda i, j, k: (j, k)),
        extra_specs=(),
        out_shape=[jax.ShapeDtypeStruct((seq, kdim), F32)],
        out_specs=[pl.BlockSpec((tm, tn), lambda i, j, k: (i, j))],
        epilogue=_store(F32))[0]


ROW_TILE = 128


def _norm_modulate(name, xin, g, scale, shift):
    seq, d_model = xin.shape
    tr = _tile(seq, ROW_TILE)

    def body(x_ref, g_ref, sc_ref, sh_ref, h_ref):
        xv = x_ref[...]
        r = lax.rsqrt(jnp.mean(xv * xv, axis=-1, keepdims=True) + NORM_EPS)
        h_ref[...] = (((xv * r) * g_ref[...]) * (1.0 + sc_ref[...]) + sh_ref[...]).astype(BF16)

    row = pl.BlockSpec((tr, d_model), lambda i: (i, 0))
    vec = pl.BlockSpec((1, d_model), lambda i: (0, 0))
    return pl.pallas_call(
        body, name=name, grid=(seq // tr,),
        in_specs=[row, vec, vec, vec], out_specs=row,
        out_shape=jax.ShapeDtypeStruct((seq, d_model), BF16),
        compiler_params=_params("parallel"),
    )(xin, g, scale, shift)


def _loss_head(x2, target, final_g, mlp, gate2):
    seq, d_model = x2.shape
    tr = _tile(seq, ROW_TILE)

    def body(x_ref, t_ref, fg_ref, mlp_ref, gate_ref, dx_ref, dmlp_ref, gfg_ref, dgate_ref, sq_ref):
        @pl.when(pl.program_id(0) == 0)
        def _():
            gfg_ref[...] = jnp.zeros_like(gfg_ref)
            dgate_ref[...] = jnp.zeros_like(dgate_ref)
            sq_ref[...] = jnp.zeros_like(sq_ref)

        xv = x_ref[...]
        fg = fg_ref[...]
        r = lax.rsqrt(jnp.mean(xv * xv, axis=-1, keepdims=True) + NORM_EPS)
        n = xv * r
        err = n * fg - t_ref[...]
        sq_ref[...] += jnp.sum(err * err, axis=0, keepdims=True)
        dy = err * (1.0 / d_model)
        gfg_ref[...] += jnp.sum(dy * n, axis=0, keepdims=True)
        dn = dy * fg
        dx = r * (dn - n * jnp.mean(dn * n, axis=-1, keepdims=True))
        dx_ref[...] = dx
        dgate_ref[...] += jnp.sum(dx * mlp_ref[...].astype(F32), axis=0, keepdims=True)
        dmlp_ref[...] = (dx * gate_ref[...]).astype(BF16)

    row = pl.BlockSpec((tr, d_model), lambda i: (i, 0))
    vec = pl.BlockSpec((1, d_model), lambda i: (0, 0))
    vshape = jax.ShapeDtypeStruct((1, d_model), F32)
    return pl.pallas_call(
        body, name="loss_head", grid=(seq // tr,),
        in_specs=[row, row, vec, row, vec], out_specs=[row, row, vec, vec, vec],
        out_shape=[jax.ShapeDtypeStruct((seq, d_model), F32), jax.ShapeDtypeStruct((seq, d_model), BF16),
                   vshape, vshape, vshape],
        compiler_params=_params("arbitrary"),
    )(x2, target, final_g, mlp, gate2)


def _norm_modulate_backward(name, dh, xin, g, scale, dres, branch=None, gate=None, after=()):
    seq, d_model = xin.shape
    tr = _tile(seq, ROW_TILE)
    with_branch = branch is not None
    n_in = (7 if with_branch else 5) + len(after)

    def body(*refs):
        dh_ref, x_ref, g_ref, sc_ref, dres_ref = refs[:5]
        outs = refs[n_in:]
        dx_ref, dsc_ref, dsh_ref, dg_ref = outs[:4]

        @pl.when(pl.program_id(0) == 0)
        def _():
            for ref in outs[1:5] if with_branch else outs[1:4]:
                ref[...] = jnp.zeros_like(ref)

        xv = x_ref[...]
        gv = g_ref[...]
        dhv = dh_ref[...]
        r = lax.rsqrt(jnp.mean(xv * xv, axis=-1, keepdims=True) + NORM_EPS)
        xn = xv * r
        dsh_ref[...] += jnp.sum(dhv, axis=0, keepdims=True)
        dsc_ref[...] += jnp.sum(dhv * (xn * gv), axis=0, keepdims=True)
        t = dhv * (1.0 + sc_ref[...])
        dg_ref[...] += jnp.sum(t * xn, axis=0, keepdims=True)
        dxn = t * gv
        dx = dres_ref[...] + r * (dxn - xn * jnp.mean(dxn * xn, axis=-1, keepdims=True))
        dx_ref[...] = dx
        if with_branch:
            br_ref, gate_ref = refs[5:7]
            dgate_ref, dbr_ref = outs[4:6]
            dgate_ref[...] += jnp.sum(dx * br_ref[...].astype(F32), axis=0, keepdims=True)
            dbr_ref[...] = (dx * gate_ref[...]).astype(BF16)

    row = pl.BlockSpec((tr, d_model), lambda i: (i, 0))
    vec = pl.BlockSpec((1, d_model), lambda i: (0, 0))
    vshape = jax.ShapeDtypeStruct((1, d_model), F32)
    in_specs = [row, row, vec, vec, row]
    out_specs = [row, vec, vec, vec]
    out_shape = [jax.ShapeDtypeStruct((seq, d_model), F32), vshape, vshape, vshape]
    args = [dh, xin, g, scale, dres]
    if with_branch:
        in_specs += [row, vec]
        out_specs += [vec, row]
        out_shape += [vshape, jax.ShapeDtypeStruct((seq, d_model), BF16)]
        args += [branch, gate]
    in_specs += [ANY] * len(after)
    args += list(after)
    return pl.pallas_call(
        body, name=name, grid=(seq // tr,),
        in_specs=in_specs, out_specs=out_specs, out_shape=out_shape,
        compiler_params=_params("arbitrary"),
    )(*args)


def _shifted(v, k, t):
    seq = v.shape[0]
    if k == 0:
        return v
    moved = pltpu.roll(v, (-k) % seq, 0)
    return jnp.where((t + k >= 0) & (t + k < seq), moved, 0.0)


def _window_sum(v, offsets, t):
    acc = None
    for k in offsets:
        term = _shifted(v, k, t)
        acc = term if acc is None else acc + term
    return acc


def _window_count(seq, half):
    t = lax.broadcasted_iota(jnp.int32, (seq, 1), 0)
    return (jnp.minimum(t + half, seq) - jnp.maximum(t - half, 0)).astype(F32)


def _pool_forward(proj, group_dim):
    _, seq, cols = proj.shape
    tl = _tile(group_dim, 256)
    nbl = group_dim // tl
    n_groups = cols // group_dim

    def body(v_ref, o_ref):
        g = pl.program_id(0)
        for gi, window in enumerate(POOL_WINDOWS[:n_groups]):
            @pl.when(g == gi)
            def _(window=window):
                half = window // 2
                v = v_ref[...].astype(F32)
                t = lax.broadcasted_iota(jnp.int32, v.shape, 0)
                total = _window_sum(v, range(-half, half), t)
                o_ref[...] = (total / _window_count(seq, half) - v).astype(BF16)

    return pl.pallas_call(
        body, name="pool_forward", grid=(n_groups, nbl),
        in_specs=[pl.BlockSpec((None, seq, tl), lambda g, j: (0, 0, g * nbl + j))],
        out_specs=pl.BlockSpec((seq, tl), lambda g, j: (0, g * nbl + j)),
        out_shape=jax.ShapeDtypeStruct((seq, cols), BF16),
        compiler_params=_params("parallel", "parallel"),
    )(proj)


def _group_matrix(w_ref):
    return jnp.concatenate([w_ref[r] for r in range(N_CHIPS)], axis=0)


def _pool_mix_forward(pooled, w_pm, pool_scale, gnorm_g, d_model):
    seq, cols = pooled.shape
    _, n_groups, shard_rows, group_dim = w_pm.shape
    tm = _tile(seq, 512)

    def body(p_ref, w_ref, ps_ref, g_ref, o_ref):
        a = jnp.dot(p_ref[...], _group_matrix(w_ref), preferred_element_type=F32) * ps_ref[...]
        ra = lax.rsqrt(jnp.mean(a * a, axis=-1, keepdims=True) + NORM_EPS)
        o_ref[...] = ((a * ra) * g_ref[...]).astype(BF16)

    tile = pl.BlockSpec((tm, group_dim), lambda g, i: (i, g))
    vec = pl.BlockSpec((1, group_dim), lambda g, i: (0, g))
    return pl.pallas_call(
        body, name="pool_mix_forward", grid=(n_groups, seq // tm),
        in_specs=[tile, pl.BlockSpec((N_CHIPS, None, shard_rows, group_dim), lambda g, i: (0, g, 0, 0)), vec, vec],
        out_specs=tile,
        out_shape=jax.ShapeDtypeStruct((seq, d_model), BF16),
        compiler_params=_params("parallel", "parallel"),
    )(pooled, w_pm, pool_scale, gnorm_g)


def _conv_parts(b_ref, c_ref, u_ref, w_ref, bias_ref):
    bv = b_ref[...].astype(F32)
    cu = c_ref[...].astype(F32) * u_ref[...].astype(F32)
    t = lax.broadcasted_iota(jnp.int32, cu.shape, 0)
    prev, nxt = _shifted(cu, -1, t), _shifted(cu, 1, t)
    w = w_ref[...]
    conv = w[0:1] * prev + w[1:2] * cu + w[2:3] * nxt + bias_ref[...]
    return bv, cu, prev, nxt, conv, w, t


def _conv_forward(proj, conv_w, conv_b, gnorm_g, mixed):
    _, seq, cols = proj.shape
    tl = CONV_HEAD_DIM
    first = cols // tl

    def body(b_ref, c_ref, u_ref, w_ref, bias_ref, g_ref, mixed_ref, o_ref):
        bv, _, _, _, conv, _, _ = _conv_parts(b_ref, c_ref, u_ref, w_ref, bias_ref)
        bo = bv * conv
        rb = lax.rsqrt(jnp.mean(bo * bo, axis=-1, keepdims=True) + NORM_EPS)
        o_ref[...] = ((bo * rb) * g_ref[...]).astype(BF16)

    def slab(s):
        return pl.BlockSpec((None, seq, tl), lambda j, s=s: (s, 0, j))

    vec = pl.BlockSpec((1, tl), lambda j: (0, j))
    return pl.pallas_call(
        body, name="conv_forward", grid=(cols // tl,),
        in_specs=[slab(1), slab(2), slab(3), pl.BlockSpec((3, tl), lambda j: (0, j)), vec, vec, ANY],
        out_specs=pl.BlockSpec((seq, tl), lambda j: (0, first + j)),
        out_shape=jax.ShapeDtypeStruct(mixed.shape, mixed.dtype),
        input_output_aliases={6: 0},
        compiler_params=_params("parallel"),
    )(proj, proj, proj, conv_w, conv_b, gnorm_g, mixed)


def _pool_mix_backward(dmixed, pooled, w_pm, pool_scale, gnorm_g, after=()):
    seq, cols = pooled.shape
    _, n_groups, shard_rows, group_dim = w_pm.shape
    tm = _tile(seq, 512)

    def body(dm_ref, p_ref, w_ref, ps_ref, g_ref, *rest):
        dp_ref, dpm_ref, gg_ref, gps_ref = rest[-4:]

        @pl.when(pl.program_id(1) == 0)
        def _():
            gg_ref[...] = jnp.zeros_like(gg_ref)
            gps_ref[...] = jnp.zeros_like(gps_ref)

        w = _group_matrix(w_ref)
        ps = ps_ref[...]
        a_pre = jnp.dot(p_ref[...], w, preferred_element_type=F32)
        a = a_pre * ps
        ra = lax.rsqrt(jnp.mean(a * a, axis=-1, keepdims=True) + NORM_EPS)
        an = a * ra
        dm = dm_ref[...]
        gg_ref[...] += jnp.sum(dm * an, axis=0, keepdims=True)
        dan = dm * g_ref[...]
        da = ra * (dan - an * jnp.mean(dan * an, axis=-1, keepdims=True))
        gps_ref[...] += jnp.sum(da * a_pre, axis=0, keepdims=True)
        dpm = (da * ps).astype(BF16)
        dpm_ref[...] = dpm
        dp_ref[...] = lax.dot_general(dpm, w, NT, preferred_element_type=F32)

    tile = pl.BlockSpec((tm, group_dim), lambda g, i: (i, g))
    vec = pl.BlockSpec((1, group_dim), lambda g, i: (0, g))
    vshape = jax.ShapeDtypeStruct((1, cols), F32)
    return pl.pallas_call(
        body, name="pool_mix_backward", grid=(n_groups, seq // tm),
        in_specs=[tile, tile, pl.BlockSpec((N_CHIPS, None, shard_rows, group_dim), lambda g, i: (0, g, 0, 0)),
                  vec, vec] + [ANY] * len(after),
        out_specs=[tile, tile, vec, vec],
        out_shape=[jax.ShapeDtypeStruct((seq, cols), F32), jax.ShapeDtypeStruct((seq, cols), BF16), vshape, vshape],
        compiler_params=_params("parallel", "arbitrary"),
    )(dmixed, pooled, w_pm, pool_scale, gnorm_g, *after)


def _pool_mix_weight_grad(pooled, dpm, n_groups):
    seq, cols = pooled.shape
    group_dim = cols // n_groups
    shard_rows = group_dim // N_CHIPS
    tk = _tile(seq, 1024)
    gk = seq // tk

    def body(p_ref, d_ref, o_ref, acc_ref):
        k = pl.program_id(1)

        @pl.when(k == 0)
        def _():
            acc_ref[...] = jnp.zeros_like(acc_ref)

        acc_ref[...] += lax.dot_general(p_ref[...], d_ref[...], TN, preferred_element_type=F32)

        @pl.when(k == gk - 1)
        def _():
            for r in range(N_CHIPS):
                o_ref[r] = acc_ref[r * shard_rows:(r + 1) * shard_rows, :].astype(BF16)

    tile = pl.BlockSpec((tk, group_dim), lambda g, k: (k, g))
    return pl.pallas_call(
        body, name="pool_mix_weight_grad", grid=(n_groups, gk),
        in_specs=[tile, tile],
        out_specs=pl.BlockSpec((N_CHIPS, None, shard_rows, group_dim), lambda g, k: (0, g, 0, 0)),
        out_shape=jax.ShapeDtypeStruct((N_CHIPS, n_groups, shard_rows, group_dim), BF16),
        scratch_shapes=[pltpu.VMEM((group_dim, group_dim), F32)],
        compiler_params=_params("parallel", "arbitrary"),
    )(pooled, dpm)


def _mixers_backward(dpooled, dmixed, proj, conv_w, conv_b, gnorm_g, group_dim):
    _, seq, cols = proj.shape
    tl = CONV_HEAD_DIM
    first = cols // tl
    per_group = group_dim // tl
    n_groups = cols // group_dim

    def body(dp_ref, dm_ref, b_ref, c_ref, u_ref, w_ref, bias_ref, g_ref, o_ref, gg_ref, gb_ref, gw_ref):
        j = pl.program_id(0)
        for gi, window in enumerate(POOL_WINDOWS[:n_groups]):
            @pl.when(j // per_group == gi)
            def _(window=window):
                half = window // 2
                dp = dp_ref[...]
                t = lax.broadcasted_iota(jnp.int32, dp.shape, 0)
                dq = dp / _window_count(seq, half)
                o_ref[0] = (_window_sum(dq, range(-half + 1, half + 1), t) - dp).astype(BF16)

        bv, cu, prev, nxt, conv, w, t = _conv_parts(b_ref, c_ref, u_ref, w_ref, bias_ref)
        bo = bv * conv
        rb = lax.rsqrt(jnp.mean(bo * bo, axis=-1, keepdims=True) + NORM_EPS)
        bn = bo * rb
        dm = dm_ref[...]
        gg_ref[...] = jnp.sum(dm * bn, axis=0, keepdims=True)
        dbn = dm * g_ref[...]
        dbo = rb * (dbn - bn * jnp.mean(dbn * bn, axis=-1, keepdims=True))
        o_ref[1] = (dbo * conv).astype(BF16)
        dconv = dbo * bv
        gb_ref[...] = jnp.sum(dconv, axis=0, keepdims=True)
        gw_ref[0:1, :] = jnp.sum(dconv * prev, axis=0, keepdims=True)
        gw_ref[1:2, :] = jnp.sum(dconv * cu, axis=0, keepdims=True)
        gw_ref[2:3, :] = jnp.sum(dconv * nxt, axis=0, keepdims=True)
        dcu = w[0:1] * _shifted(dconv, 1, t) + w[1:2] * dconv + w[2:3] * _shifted(dconv, -1, t)
        o_ref[2] = (dcu * u_ref[...].astype(F32)).astype(BF16)
        o_ref[3] = (dcu * c_ref[...].astype(F32)).astype(BF16)

    def slab(s):
        return pl.BlockSpec((None, seq, tl), lambda j, s=s: (s, 0, j))

    vec = pl.BlockSpec((1, tl), lambda j: (0, j))
    rows3 = pl.BlockSpec((3, tl), lambda j: (0, j))
    vshape = jax.ShapeDtypeStruct((1, cols), F32)
    return pl.pallas_call(
        body, name="mixers_backward", grid=(cols // tl,),
        in_specs=[pl.BlockSpec((seq, tl), lambda j: (0, j)), pl.BlockSpec((seq, tl), lambda j: (0, first + j)),
                  slab(1), slab(2), slab(3), rows3, vec, vec],
        out_specs=[pl.BlockSpec((N_CHIPS, seq, tl), lambda j: (0, 0, j)), vec, vec, rows3],
        out_shape=[jax.ShapeDtypeStruct((N_CHIPS, seq, cols), BF16), vshape, vshape,
                   jax.ShapeDtypeStruct((3, cols), F32)],
        compiler_params=_params("parallel"),
    )(dpooled, dmixed, proj, proj, proj, conv_w, conv_b, gnorm_g)


class _GradReduction:
    def __init__(self, tag, grads, states, pair_id, scatter_id, position):
        self.tag, self.grads, self.states = tag, grads, states
        self.pair_id, self.scatter_id = pair_id, scatter_id
        self.chip, self.core, self.other_core = position

    def exchange(self):
        self.received = _exchange_halves(f"exchange_{self.tag}", self.grads, self.pair_id)

    def combine(self, after=()):
        self.parts = [_add_half(f"add_half_{self.tag}_{a}", g, r, self.core, after)
                      for a, (g, r) in enumerate(zip(self.grads, self.received))]

    def scatter(self):
        self.landed = _scatter_partials(f"scatter_{self.tag}", self.parts, self.scatter_id)

    def reduce(self, after=()):
        self.reduced = [_reduce_chips(f"reduce_chips_{self.tag}_{a}", p, l, self.chip, after)
                        for a, (p, l) in enumerate(zip(self.parts, self.landed))]

    def swap(self):
        self.swapped = _swap_reduced(f"swap_{self.tag}", self.reduced, self.pair_id)

    def update_mine(self):
        self.mine = [_adamw_half(f"adamw_mine_{self.tag}_{a}", g, *state, self.core)
                     for a, (g, state) in enumerate(zip(self.reduced, self.states))]

    def update_other(self, after=()):
        self.results = [_adamw_half(f"adamw_other_{self.tag}_{a}", g, *state, self.other_core, done, after)
                        for a, (g, state, done) in enumerate(zip(self.swapped, self.states, self.mine))]

    def token(self, stage):
        first = getattr(self, stage)[0]
        return first if not isinstance(first, (list, tuple)) else first[0]


def kernel(x, c, w_ada, b_ada, norm1_g, w_in, pool_mix_w, pool_scale, conv_w, conv_b, gnorm_pool_g, gnorm_conv_g, w_out, norm2_g, w_mlp_in, w_mlp_out, final_g, loss_target, m_w_ada, m_b_ada, m_norm1_g, m_w_in, m_pool_mix_w, m_pool_scale, m_conv_w, m_conv_b, m_gnorm_pool_g, m_gnorm_conv_g, m_w_out, m_norm2_g, m_w_mlp_in, m_w_mlp_out, m_final_g, v_w_ada, v_b_ada, v_norm1_g, v_w_in, v_pool_mix_w, v_pool_scale, v_conv_w, v_conv_b, v_gnorm_pool_g, v_gnorm_conv_g, v_w_out, v_norm2_g, v_w_mlp_in, v_w_mlp_out, v_final_g):
    seq, d_model = x.shape[1], x.shape[2]
    cols = w_in.shape[2]
    n_groups, group_dim = pool_mix_w.shape[1], pool_mix_w.shape[3]
    shard_rows = pool_mix_w.shape[2]
    ff_cols = w_mlp_in.shape[2]
    ada_cols = w_ada.shape[2]
    conv_shard = conv_w.shape[2]
    assert pool_scale.shape[1] == cols and conv_b.shape[1] == cols and n_groups * group_dim == cols
    assert cols % CONV_HEAD_DIM == 0 and group_dim % CONV_HEAD_DIM == 0 and shard_rows * N_CHIPS == group_dim

    ix, iy, ic = _position()
    chip = 2 * ix + iy
    me = 4 * ix + 2 * iy + ic
    position = tuple(jnp.reshape(v, (1,)).astype(jnp.int32) for v in (chip, ic, 1 - ic))

    xs, target = x[0], loss_target[0]
    final_row = final_g.reshape(1, d_model)

    small = _gather_flat("gather_cond", jnp.concatenate([c[0], conv_w[0].reshape(-1)]))
    c_all = small[:, :d_model]
    conv_w_full = jnp.concatenate(
        [small[2 * j, d_model:].reshape(3, conv_shard) for j in range(N_CHIPS)], axis=1)
    b_cols = lax.dynamic_slice_in_dim(b_ada, chip * ada_cols, ada_cols, axis=1)
    mod_part = _ada_forward(c_all, w_ada[0], b_cols)
    mod_all = _gather_flat("gather_mod", mod_part.reshape(-1)).reshape(N_DEV, N_DEV, ada_cols)
    mod = jnp.concatenate(
        [lax.dynamic_slice_in_dim(mod_all[2 * j], me, 1, axis=0) for j in range(N_CHIPS)], axis=1)
    shift1, scale1, gate1, shift2, scale2, gate2 = [mod[:, i * d_model:(i + 1) * d_model] for i in range(N_MOD)]

    shards = [w_in[0].astype(BF16), pool_mix_w[0].reshape(n_groups * shard_rows, group_dim).astype(BF16),
              w_out[0].astype(BF16), w_mlp_in[0].astype(BF16), w_mlp_out[0].astype(BF16)]
    wg_in, wg_pm = _gather_weights("gather_w_in", shards[0:2], 1)
    (wg_out,) = _gather_weights("gather_w_out", shards[2:3], 2)
    (wg_1,) = _gather_weights("gather_w_mlp_in", shards[3:4], 3)
    (wg_2,) = _gather_weights("gather_w_mlp_out", shards[4:5], 4)
    wg_in, wg_pm, wg_out, wg_1, wg_2 = [
        _place_own(f"place_own_{a}", g, s, position[0])
        for a, (g, s) in enumerate(zip([wg_in, wg_pm, wg_out, wg_1, wg_2], shards))]
    wg_pm = wg_pm.reshape(N_CHIPS, n_groups, shard_rows, group_dim)
    wg_out = wg_out.reshape(d_model, d_model)
    wg_2 = wg_2.reshape(N_CHIPS * ff_cols, d_model)

    h1 = _norm_modulate("norm_modulate_1", xs, norm1_g, scale1, shift1)
    proj = _in_projection(h1, wg_in)
    pooled = _pool_forward(proj, group_dim)
    mixed = _pool_mix_forward(pooled, wg_pm, pool_scale, gnorm_pool_g, d_model)
    mixed = _conv_forward(proj, conv_w_full, conv_b, gnorm_conv_g, mixed)
    x1, attn = _residual_projection("out_projection", mixed, wg_out, xs, gate1)
    h2 = _norm_modulate("norm_modulate_2", x1, norm2_g, scale2, shift2)
    act = _mlp_in(h2, wg_1)
    x2, mlp = _residual_projection("mlp_out", act, wg_2, x1, gate2, prologue=_square)

    dx2, dmlp, g_final, dgate2, sq_err = _loss_head(x2, target, final_row, mlp, gate2)
    gw_2 = _weight_grad("grad_w_mlp_out", act, dmlp, prologue=_square)
    red_2 = _GradReduction("w_mlp_out", [gw_2.reshape(N_CHIPS, ff_cols, d_model)],
                           [(w_mlp_out[0], m_w_mlp_out[0], v_w_mlp_out[0])], 8, 12, position)
    red_2.exchange()
    dhid = _grad_hidden(dmlp, wg_2, act, after=[gw_2])
    red_2.combine(after=[dhid])
    red_2.scatter()
    gw_1 = _weight_grad_slabs("grad_w_mlp_in", h2, dhid, ff_cols, after=[red_2.token("parts")])
    red_1 = _GradReduction("w_mlp_in", [gw_1], [(w_mlp_in[0], m_w_mlp_in[0], v_w_mlp_in[0])], 7, 11, position)
    red_1.exchange()
    dh2 = _grad_input_slabs("grad_h2", dhid, wg_1, after=[gw_1])
    red_2.reduce(after=[dh2])
    red_2.swap()
    red_2.update_mine()
    red_1.combine(after=[red_2.token("mine")])
    red_1.scatter()
    dx1, dscale2, dshift2, g_norm2, dgate1, dattn = _norm_modulate_backward(
        "norm_modulate_backward_2", dh2, x1, norm2_g, scale2, dx2, attn, gate1, after=[red_1.token("parts")])
    gw_out = _weight_grad("grad_w_out", mixed, dattn)
    red_out = _GradReduction("w_out", [gw_out.reshape(N_CHIPS, d_model // N_CHIPS, d_model)],
                             [(w_out[0], m_w_out[0], v_w_out[0])], 6, 10, position)
    red_out.exchange()
    dmixed = _grad_input("grad_mixed", dattn, wg_out)
    dpooled, dpm, g_gpool, g_pscale = _pool_mix_backward(dmixed, pooled, wg_pm, pool_scale, gnorm_pool_g)
    gw_pm = _pool_mix_weight_grad(pooled, dpm, n_groups)
    dproj, g_gconv, g_convb, g_convw = _mixers_backward(
        dpooled, dmixed, proj, conv_w_full, conv_b, gnorm_conv_g, group_dim)
    gw_in = _weight_grad_slabs("grad_w_in", h1, dproj, None)
    pm2d = (n_groups * shard_rows, group_dim)
    red_in = _GradReduction(
        "w_in", [gw_in, gw_pm.reshape((N_CHIPS,) + pm2d)],
        [(w_in[0], m_w_in[0], v_w_in[0]),
         (pool_mix_w[0].reshape(pm2d), m_pool_mix_w[0].reshape(pm2d), v_pool_mix_w[0].reshape(pm2d))],
        5, 9, position)
    red_in.exchange()
    red_1.reduce(after=[gw_in])
    red_1.swap()
    red_1.update_mine()
    red_out.combine(after=[red_1.token("mine")])
    red_out.scatter()
    red_in.combine(after=[red_out.token("parts")])
    red_in.scatter()
    dh1 = _grad_input_slabs("grad_h1", dproj, wg_in, after=[red_in.token("parts")])
    red_2.update_other(after=[dh1])
    grad_x, dscale1, dshift1, g_norm1 = _norm_modulate_backward(
        "norm_modulate_backward_1", dh1, xs, norm1_g, scale1, dx1, after=[red_2.token("results")])
    red_out.reduce(after=[grad_x])
    red_out.swap()
    red_out.update_mine()
    red_1.update_other(after=[red_out.token("mine")])

    mine = jnp.concatenate(
        [dshift1, dscale1, dgate1, dshift2, dscale2, dgate2, g_norm1, g_norm2, g_final, sq_err,
         g_pscale, g_convb, g_gpool, g_gconv, g_convw.reshape(1, 3 * cols)], axis=1)
    gathered = _gather_flat("gather_small", mine.reshape(-1))
    sums, loss = _reduce_small(gathered, d_model)
    n_rep = (N_MOD + 3) * d_model
    g_rep = jnp.concatenate([sums[:, :n_rep], sums[:, n_rep + d_model:n_rep + d_model + 4 * cols]], axis=1)
    n_small = g_rep.shape[1]

    def pack(b, n1, n2, fg, ps, cb, gp, gc):
        return jnp.concatenate([b, n1, n2, fg.reshape(1, d_model), ps, cb, gp, gc], axis=1).reshape(8, n_small // 8)

    d_rep, m_rep, v_rep = _adamw(
        "adamw_small", g_rep.reshape(8, n_small // 8),
        pack(b_ada, norm1_g, norm2_g, final_g, pool_scale, conv_b, gnorm_pool_g, gnorm_conv_g),
        pack(m_b_ada, m_norm1_g, m_norm2_g, m_final_g, m_pool_scale, m_conv_b, m_gnorm_pool_g, m_gnorm_conv_g),
        pack(v_b_ada, v_norm1_g, v_norm2_g, v_final_g, v_pool_scale, v_conv_b, v_gnorm_pool_g, v_gnorm_conv_g))

    def unpack(flat):
        flat = flat.reshape(1, n_small)
        sizes = [N_MOD * d_model, d_model, d_model, d_model, cols, cols, cols, cols]
        parts, at = [], 0
        for size in sizes:
            parts.append(flat[:, at:at + size])
            at += size
        parts[3] = parts[3].reshape(d_model)
        return parts

    g_convw_full = sums[:, n_rep + d_model + 4 * cols:].reshape(3, cols)
    g_convw_mine = lax.dynamic_slice_in_dim(g_convw_full, chip * conv_shard, conv_shard, axis=1)
    d_convw, m_convw, v_convw = _adamw("adamw_conv_w", g_convw_mine, conv_w[0], m_conv_w[0], v_conv_w[0])

    dmod_cols = lax.dynamic_slice_in_dim(gathered[:, :N_MOD * d_model], chip * ada_cols, ada_cols, axis=1)
    g_ada, d_ada, mn_ada, vn_ada = _ada_backward(c_all.T, dmod_cols, w_ada[0], m_w_ada[0], v_w_ada[0])

    red_in.reduce(after=[g_ada, red_1.token("results")])
    red_in.swap()
    red_in.update_mine()
    red_out.update_other(after=[red_in.token("mine")])
    red_in.update_other(after=[red_out.token("results")])

    small_parts = [unpack(g_rep), unpack(d_rep), unpack(m_rep), unpack(v_rep)]
    ada_parts = [g_ada, d_ada, mn_ada, vn_ada]
    convw_parts = [g_convw_mine, d_convw, m_convw, v_convw]

    def ordered(k):
        b, n1, n2, fg, ps, cb, gp, gc = small_parts[k]
        return [ada_parts[k][None], b, n1, red_in.results[0][k][None],
                red_in.results[1][k].reshape(pool_mix_w.shape), ps, convw_parts[k][None], cb, gp, gc,
                red_out.results[0][k][None], n2, red_1.results[0][k][None], red_2.results[0][k][None], fg]

    return (loss[0, 0], grad_x[None], *ordered(0), *ordered(1), *ordered(2), *ordered(3))
```

```python
import jax
import jax.numpy as jnp
from jax import lax
from jax.experimental import pallas as pl
from jax.experimental.pallas import tpu as pltpu
from jax.experimental.pallas import tpu_sc as plsc

F32 = jnp.float32
BF16 = jnp.bfloat16
MESH = pl.DeviceIdType.MESH
ANY = pl.BlockSpec(memory_space=pl.ANY)

NORM_EPS = 1e-6
POOL_WINDOWS = (2, 4, 8, 16)
CONV_HEAD_DIM = 128
N_MOD = 6
N_CHIPS = 4
N_DEV = 8

ADAM_LR = 0.001
ADAM_B1 = 0.9
ADAM_B2 = 0.999
ADAM_EPS = 1e-08
ADAM_WD = 0.01
ADAM_STEP = 10

VMEM_LIMIT_BYTES = 56 * 1024 * 1024

NN = (((1,), (0,)), ((), ()))
NT = (((1,), (1,)), ((), ()))
TN = (((0,), (0,)), ((), ()))


def _tile(n, pref):
    t = min(n, pref)
    while n % t:
        t //= 2
    return t


STREAM_BLOCK_BYTES = 2 * 1024 * 1024


def _stream_block(rows, cols, itemsize):
    tc = cols if 8 * cols * itemsize <= STREAM_BLOCK_BYTES else _tile(cols, 2048)
    fit = max(8, STREAM_BLOCK_BYTES // (tc * itemsize))
    return _tile(rows, 1 << (fit.bit_length() - 1)), tc


def _params(*sem):
    return pltpu.CompilerParams(dimension_semantics=sem, vmem_limit_bytes=VMEM_LIMIT_BYTES)


def _position():
    return lax.axis_index("x"), lax.axis_index("y"), lax.axis_index("c")


def _flip(ix, iy, ic, mask):
    return (1 - ix if mask & 4 else ix, 1 - iy if mask & 2 else iy, 1 - ic if mask & 1 else ic)


def _allgather8(name, blk):
    rows, cols = blk.shape

    def body(x_ref, out_ref, send_sems, recv_sems, local_sem):
        ix, iy, ic = _position()
        me = 4 * ix + 2 * iy + ic
        mine = pltpu.make_async_copy(x_ref, out_ref.at[me], local_sem)
        mine.start()
        sends = []
        for mask in range(1, N_DEV):
            cp = pltpu.make_async_remote_copy(
                src_ref=x_ref, dst_ref=out_ref.at[me],
                send_sem=send_sems.at[mask - 1], recv_sem=recv_sems.at[mask - 1],
                device_id=_flip(ix, iy, ic, mask), device_id_type=MESH)
            cp.start()
            sends.append(cp)
        for mask in range(1, N_DEV):
            px, py, pc = _flip(ix, iy, ic, mask)
            pltpu.make_async_remote_copy(
                src_ref=x_ref, dst_ref=out_ref.at[4 * px + 2 * py + pc],
                send_sem=send_sems.at[mask - 1], recv_sem=recv_sems.at[mask - 1],
                device_id=(px, py, pc), device_id_type=MESH).wait_recv()
        for cp in sends:
            cp.wait_send()
        mine.wait()

    return pl.pallas_call(
        body, name=name,
        out_shape=jax.ShapeDtypeStruct((N_DEV, rows, cols), F32),
        in_specs=[pl.BlockSpec(memory_space=pltpu.VMEM)],
        out_specs=pl.BlockSpec(memory_space=pltpu.VMEM),
        scratch_shapes=[pltpu.SemaphoreType.DMA((N_DEV - 1,)), pltpu.SemaphoreType.DMA((N_DEV - 1,)),
                        pltpu.SemaphoreType.DMA],
    )(blk)


def _gather_flat(name, vec):
    n = vec.shape[0]
    npad = -(-n // 1024) * 1024
    blk = jnp.pad(vec, (0, npad - n)).reshape(8, npad // 8)
    return _allgather8(name, blk).reshape(N_DEV, npad)[:, :n]


def _chip_relations(ix, iy):
    return [(1 - ix, iy), (ix, 1 - iy), (1 - ix, 1 - iy)]


def _gather_weights(name, shards, collective_id):
    n = len(shards)

    def body(*refs):
        src, out = refs[:n], refs[n:2 * n]
        send_sems, recv_sems = refs[2 * n:]
        ix, iy, ic = _position()
        chip, chip_x, chip_y, chip_d = 2 * ix + iy, 2 * (1 - ix) + iy, 2 * ix + 1 - iy, 2 * (1 - ix) + 1 - iy
        beside_x, beside_y, sibling = (1 - ix, iy, ic), (ix, 1 - iy, ic), (ix, iy, 1 - ic)

        _handshake([beside_x, beside_y, sibling])

        def rows(a, core, quarter=None):
            half = shards[a].shape[0] // 2
            if quarter is None:
                return pl.ds(core * half, half)
            return pl.ds(core * half + quarter * (half // 2), half // 2)

        def copy(a, k, src_ref, dst_ref, to):
            return pltpu.make_async_remote_copy(
                src_ref=src_ref, dst_ref=dst_ref, send_sem=send_sems.at[8 * a + k], recv_sem=recv_sems.at[8 * a + k],
                device_id=to, device_id_type=MESH)

        def relay(a, k, piece, to):
            return copy(a, k, out[a].at[piece], out[a].at[piece], to)

        started = []

        def start(cp):
            cp.start()
            started.append(cp)

        for a in range(n):
            mine = src[a].at[rows(a, ic)]
            start(copy(a, 0, mine, out[a].at[chip, rows(a, ic)], beside_x))
            start(copy(a, 1, mine, out[a].at[chip, rows(a, ic)], beside_y))
        for a in range(n):
            relay(a, 0, (chip_x, rows(a, ic)), beside_x).wait_recv()
            start(relay(a, 3, (chip_x, rows(a, ic, 1)), beside_y))
            start(relay(a, 4, (chip_x, rows(a, ic)), sibling))
            relay(a, 1, (chip_y, rows(a, ic)), beside_y).wait_recv()
            start(relay(a, 2, (chip_y, rows(a, ic, 0)), beside_x))
            start(relay(a, 5, (chip_y, rows(a, ic)), sibling))
        for a in range(n):
            relay(a, 2, (chip_d, rows(a, ic, 0)), beside_x).wait_recv()
            start(relay(a, 6, (chip_d, rows(a, ic, 0)), sibling))
            relay(a, 3, (chip_d, rows(a, ic, 1)), beside_y).wait_recv()
            start(relay(a, 7, (chip_d, rows(a, ic, 1)), sibling))
        for a in range(n):
            relay(a, 4, (chip_x, rows(a, 1 - ic)), sibling).wait_recv()
            relay(a, 5, (chip_y, rows(a, 1 - ic)), sibling).wait_recv()
            relay(a, 6, (chip_d, rows(a, 1 - ic, 0)), sibling).wait_recv()
            relay(a, 7, (chip_d, rows(a, 1 - ic, 1)), sibling).wait_recv()
        for cp in started:
            cp.wait_send()

    out_type = [jax.ShapeDtypeStruct((N_CHIPS,) + s.shape, s.dtype) for s in shards]
    return _sequencer_call(name, body, shards, out_type, [8 * n, 8 * n], collective_id)


def _place_own(name, gathered, shard, chip):
    rows, cols = shard.shape
    tr, tc = _stream_block(rows, cols, 2)

    def body(chip_ref, own_ref, gathered_ref, o_ref):
        o_ref[...] = own_ref[...]

    return pl.pallas_call(
        body, name=name,
        grid_spec=pltpu.PrefetchScalarGridSpec(
            num_scalar_prefetch=1, grid=(rows // tr, cols // tc),
            in_specs=[pl.BlockSpec((tr, tc), lambda i, j, chip_ref: (i, j)), ANY],
            out_specs=pl.BlockSpec((None, tr, tc), lambda i, j, chip_ref: (chip_ref[0], i, j))),
        out_shape=jax.ShapeDtypeStruct(gathered.shape, gathered.dtype),
        input_output_aliases={2: 0},
        compiler_params=_params("parallel", "parallel"),
    )(chip, shard, gathered)


def _sequencer_call(name, body, operands, out_type, sem_counts, collective_id):
    return pl.kernel(
        body, name=name, out_type=out_type,
        mesh=plsc.ScalarSubcoreMesh(axis_name="sequencer", num_cores=1),
        scratch_types=[pltpu.SemaphoreType.DMA((n,)) for n in sem_counts],
        compiler_params=pltpu.CompilerParams(collective_id=collective_id),
    )(*operands)


def _handshake(peers):
    barrier = pltpu.get_barrier_semaphore()
    for peer in peers:
        pl.semaphore_signal(barrier, inc=1, device_id=peer, device_id_type=MESH)
    pl.semaphore_wait(barrier, len(peers))


def _exchange_halves(name, grads, collective_id):
    n = len(grads)

    def body(*refs):
        src, out = refs[:n], refs[n:2 * n]
        send_sems, recv_sems = refs[2 * n:]
        ix, iy, ic = _position()
        sibling = (ix, iy, 1 - ic)
        _handshake([sibling])
        copies = []
        for a in range(n):
            half = grads[a].shape[1] // 2
            cp = pltpu.make_async_remote_copy(
                src_ref=src[a].at[pl.ds(0, N_CHIPS), pl.ds((1 - ic) * half, half)], dst_ref=out[a],
                send_sem=send_sems.at[a], recv_sem=recv_sems.at[a],
                device_id=sibling, device_id_type=MESH)
            cp.start()
            copies.append(cp)
        for cp in copies:
            cp.wait()

    out_type = [jax.ShapeDtypeStruct((N_CHIPS, g.shape[1] // 2, g.shape[2]), g.dtype) for g in grads]
    return _sequencer_call(name, body, grads, out_type, [n, n], collective_id)


def _scatter_partials(name, parts, collective_id):
    n = len(parts)

    def body(*refs):
        src, out = refs[:n], refs[n:2 * n]
        send_sems, recv_sems = refs[2 * n:]
        ix, iy, ic = _position()
        rels = _chip_relations(ix, iy)
        _handshake([(px, py, ic) for px, py in rels])
        copies = []
        for a in range(n):
            for r, (px, py) in enumerate(rels):
                cp = pltpu.make_async_remote_copy(
                    src_ref=src[a].at[2 * px + py], dst_ref=out[a].at[r],
                    send_sem=send_sems.at[3 * a + r], recv_sem=recv_sems.at[3 * a + r],
                    device_id=(px, py, ic), device_id_type=MESH)
                cp.start()
                copies.append(cp)
        for cp in copies:
            cp.wait()

    out_type = [jax.ShapeDtypeStruct((3,) + p.shape[1:], p.dtype) for p in parts]
    return _sequencer_call(name, body, parts, out_type, [3 * n, 3 * n], collective_id)


def _swap_reduced(name, reduced, collective_id):
    n = len(reduced)

    def body(*refs):
        src, out = refs[:n], refs[n:2 * n]
        send_sems, recv_sems = refs[2 * n:]
        ix, iy, ic = _position()
        sibling = (ix, iy, 1 - ic)
        _handshake([sibling])
        copies = []
        for a in range(n):
            cp = pltpu.make_async_remote_copy(
                src_ref=src[a], dst_ref=out[a], send_sem=send_sems.at[a], recv_sem=recv_sems.at[a],
                device_id=sibling, device_id_type=MESH)
            cp.start()
            copies.append(cp)
        for cp in copies:
            cp.wait()

    out_type = [jax.ShapeDtypeStruct(r.shape, r.dtype) for r in reduced]
    return _sequencer_call(name, body, reduced, out_type, [n, n], collective_id)


def _add_half(name, grad, recv, core, after=()):
    _, rows, cols = grad.shape
    half = rows // 2
    tr, tc = _stream_block(half, cols, 2)
    nbr = half // tr

    def body(core_ref, g_ref, r_ref, *rest):
        rest[-1][...] = g_ref[...] + r_ref[...]

    return pl.pallas_call(
        body, name=name,
        grid_spec=pltpu.PrefetchScalarGridSpec(
            num_scalar_prefetch=1, grid=(N_CHIPS, nbr, cols // tc),
            in_specs=[pl.BlockSpec((None, tr, tc), lambda s, i, j, core_ref: (s, core_ref[0] * nbr + i, j)),
                      pl.BlockSpec((None, tr, tc), lambda s, i, j, core_ref: (s, i, j))] + [ANY] * len(after),
            out_specs=pl.BlockSpec((None, tr, tc), lambda s, i, j, core_ref: (s, i, j))),
        out_shape=jax.ShapeDtypeStruct((N_CHIPS, half, cols), BF16),
        compiler_params=_params("parallel", "parallel", "parallel"),
    )(core, grad, recv, *after)


def _reduce_chips(name, part, recv, chip, after=()):
    _, half, cols = part.shape
    tr, tc = _stream_block(half, cols, 2)

    def body(chip_ref, p_ref, r_ref, *rest):
        o_ref = rest[-1]
        acc = p_ref[...].astype(F32)
        for r in range(3):
            acc = acc + r_ref[r].astype(F32)
        o_ref[...] = acc

    return pl.pallas_call(
        body, name=name,
        grid_spec=pltpu.PrefetchScalarGridSpec(
            num_scalar_prefetch=1, grid=(half // tr, cols // tc),
            in_specs=[pl.BlockSpec((None, tr, tc), lambda i, j, chip_ref: (chip_ref[0], i, j)),
                      pl.BlockSpec((3, tr, tc), lambda i, j, chip_ref: (0, i, j))] + [ANY] * len(after),
            out_specs=pl.BlockSpec((tr, tc), lambda i, j, chip_ref: (i, j))),
        out_shape=jax.ShapeDtypeStruct((half, cols), F32),
        compiler_params=_params("parallel", "parallel"),
    )(chip, part, recv, *after)


def _adamw_half(name, g_half, w, m, v, which, done=None, after=()):
    half, cols = g_half.shape
    tr, tc = _stream_block(half, cols, 4)
    nbr = half // tr
    bc1 = 1.0 - ADAM_B1 ** ADAM_STEP
    bc2 = 1.0 - ADAM_B2 ** ADAM_STEP

    def body(which_ref, g_ref, w_ref, m_ref, v_ref, *rest):
        go_ref, d_ref, mo_ref, vo_ref = rest[-4:]
        gv = g_ref[...]
        mn = ADAM_B1 * m_ref[...] + (1.0 - ADAM_B1) * gv
        vn = ADAM_B2 * v_ref[...] + (1.0 - ADAM_B2) * (gv * gv)
        go_ref[...] = gv
        d_ref[...] = -ADAM_LR * ((mn / bc1) / (jnp.sqrt(vn / bc2) + ADAM_EPS) + ADAM_WD * w_ref[...])
        mo_ref[...] = mn
        vo_ref[...] = vn

    mine = pl.BlockSpec((tr, tc), lambda i, j, which_ref: (which_ref[0] * nbr + i, j))
    kept = [] if done is None else list(done)
    shape = jax.ShapeDtypeStruct((2 * half, cols), F32)
    return pl.pallas_call(
        body, name=name,
        grid_spec=pltpu.PrefetchScalarGridSpec(
            num_scalar_prefetch=1, grid=(nbr, cols // tc),
            in_specs=([pl.BlockSpec((tr, tc), lambda i, j, which_ref: (i, j)), mine, mine, mine]
                      + [ANY] * (len(kept) + len(after))),
            out_specs=[mine] * 4),
        out_shape=[shape] * 4,
        input_output_aliases={5 + k: k for k in range(len(kept))},
        compiler_params=_params("parallel", "parallel"),
    )(which, g_half, w, m, v, *kept, *after)


def _adamw(name, g, w, m, v):
    rows, cols = g.shape
    tr, tc = _stream_block(rows, cols, 4)
    bc1 = 1.0 - ADAM_B1 ** ADAM_STEP
    bc2 = 1.0 - ADAM_B2 ** ADAM_STEP

    def body(g_ref, w_ref, m_ref, v_ref, d_ref, mo_ref, vo_ref):
        gv = g_ref[...]
        mn = ADAM_B1 * m_ref[...] + (1.0 - ADAM_B1) * gv
        vn = ADAM_B2 * v_ref[...] + (1.0 - ADAM_B2) * (gv * gv)
        d_ref[...] = -ADAM_LR * ((mn / bc1) / (jnp.sqrt(vn / bc2) + ADAM_EPS) + ADAM_WD * w_ref[...])
        mo_ref[...] = mn
        vo_ref[...] = vn

    spec = pl.BlockSpec((tr, tc), lambda i, j: (i, j))
    shape = jax.ShapeDtypeStruct((rows, cols), F32)
    return pl.pallas_call(
        body, name=name, grid=(rows // tr, cols // tc),
        in_specs=[spec] * 4, out_specs=[spec] * 3, out_shape=[shape] * 3,
        compiler_params=_params("parallel", "parallel"),
    )(g, w, m, v)


def _reduce_small(gathered, d_model):
    n = gathered.shape[1]
    loss_at = (N_MOD + 3) * d_model

    def body(g_ref, s_ref, loss_ref):
        acc = g_ref[0:1, :]
        for d in range(1, N_DEV):
            acc = acc + g_ref[d:d + 1, :]
        s_ref[...] = acc
        lanes = acc[:, loss_at:loss_at + d_model]
        loss_ref[...] = jnp.broadcast_to((0.5 / d_model) * jnp.sum(lanes, axis=1, keepdims=True), loss_ref.shape)

    return pl.pallas_call(
        body, name="reduce_small",
        out_shape=[jax.ShapeDtypeStruct((1, n), F32), jax.ShapeDtypeStruct((1, 128), F32)],
        compiler_params=pltpu.CompilerParams(vmem_limit_bytes=VMEM_LIMIT_BYTES),
    )(gathered)


def _ada_forward(c_all, w_ada, b_cols):
    d_model, width = w_ada.shape
    tn = _tile(width, 512)

    def body(c_ref, w_ref, b_ref, o_ref):
        cv = c_ref[...]
        act = cv * jax.nn.sigmoid(cv)
        o_ref[...] = lax.dot_general(act, w_ref[...], NN, precision=lax.Precision.HIGHEST,
                                     preferred_element_type=F32) + b_ref[...]

    return pl.pallas_call(
        body, name="ada_forward", grid=(width // tn,),
        in_specs=[pl.BlockSpec((N_DEV, d_model), lambda j: (0, 0)),
                  pl.BlockSpec((d_model, tn), lambda j: (0, j)),
                  pl.BlockSpec((1, tn), lambda j: (0, j))],
        out_specs=pl.BlockSpec((N_DEV, tn), lambda j: (0, j)),
        out_shape=jax.ShapeDtypeStruct((N_DEV, width), F32),
        compiler_params=_params("parallel"),
    )(c_all, w_ada, b_cols)


def _ada_backward(c_all_t, dmod_cols, w, m, v):
    d_model, width = w.shape
    tr, tc = _stream_block(d_model, width, 4)
    bc1 = 1.0 - ADAM_B1 ** ADAM_STEP
    bc2 = 1.0 - ADAM_B2 ** ADAM_STEP

    def body(c_ref, dm_ref, w_ref, m_ref, v_ref, g_ref, d_ref, mo_ref, vo_ref):
        cv = c_ref[...]
        act = cv * jax.nn.sigmoid(cv)
        gv = lax.dot_general(act, dm_ref[...], NN, precision=lax.Precision.HIGHEST, preferred_element_type=F32)
        mn = ADAM_B1 * m_ref[...] + (1.0 - ADAM_B1) * gv
        vn = ADAM_B2 * v_ref[...] + (1.0 - ADAM_B2) * (gv * gv)
        g_ref[...] = gv
        d_ref[...] = -ADAM_LR * ((mn / bc1) / (jnp.sqrt(vn / bc2) + ADAM_EPS) + ADAM_WD * w_ref[...])
        mo_ref[...] = mn
        vo_ref[...] = vn

    spec = pl.BlockSpec((tr, tc), lambda i, j: (i, j))
    shape = jax.ShapeDtypeStruct((d_model, width), F32)
    return pl.pallas_call(
        body, name="ada_backward", grid=(d_model // tr, width // tc),
        in_specs=[pl.BlockSpec((tr, N_DEV), lambda i, j: (i, 0)),
                  pl.BlockSpec((N_DEV, tc), lambda i, j: (0, j)), spec, spec, spec],
        out_specs=[spec] * 4, out_shape=[shape] * 4,
        compiler_params=_params("parallel", "parallel"),
    )(c_all_t, dmod_cols, w, m, v)


def _matmul(name, a, b, extras, *, grid, tiles, dims, a_spec, b_spec, extra_specs, out_shape, out_specs,
            epilogue, prologue=None, after=()):
    tm, tn, _ = tiles
    gm, gn, gk = grid
    n_extra, n_out = len(extras), len(out_shape)
    first_out = 2 + n_extra + len(after)

    def product(a_ref, b_ref):
        av = a_ref[...]
        if prologue is not None:
            av = prologue(av)
        return lax.dot_general(av, b_ref[...], dims, preferred_element_type=F32)

    def body_single(*refs):
        epilogue(product(refs[0], refs[1]), refs[2:2 + n_extra], refs[first_out:first_out + n_out])

    def body(*refs):
        a_ref, b_ref = refs[0], refs[1]
        extra_refs = refs[2:2 + n_extra]
        out_refs = refs[first_out:first_out + n_out]
        acc_ref = refs[-1]
        k = pl.program_id(2)

        @pl.when(k == 0)
        def _():
            acc_ref[...] = jnp.zeros_like(acc_ref)

        acc_ref[...] += product(a_ref, b_ref)

        @pl.when(k == gk - 1)
        def _():
            epilogue(acc_ref[...], extra_refs, out_refs)

    single = gk == 1
    return pl.pallas_call(
        body_single if single else body, name=name, grid=(gm, gn, gk),
        in_specs=[a_spec, b_spec, *extra_specs] + [ANY] * len(after), out_specs=out_specs, out_shape=out_shape,
        scratch_shapes=[] if single else [pltpu.VMEM((tm, tn), F32)],
        compiler_params=_params("parallel", "parallel", "arbitrary"),
    )(a, b, *extras, *after)


def _store(dtype):
    def epilogue(acc, extra_refs, out_refs):
        out_refs[0][...] = acc.astype(dtype)
    return epilogue


def _residual_epilogue(acc, extra_refs, out_refs):
    res_ref, gate_ref = extra_refs
    out_refs[0][...] = res_ref[...] + gate_ref[...] * acc
    out_refs[1][...] = acc.astype(BF16)


def _square(av):
    af = av.astype(F32)
    return (af * af).astype(BF16)


MM_TILE_M = 1024
MM_TILE_N = 1024
MM_WHOLE_K = 4096
MM_TILE_K = 4096


def _mm_tiles(m, n, k, tn_pref=MM_TILE_N, k_block=None):
    tk = k if k <= MM_WHOLE_K else MM_TILE_K
    if k_block is not None:
        tk = min(tk, k_block)
    return _tile(m, MM_TILE_M), _tile(n, tn_pref), _tile(k, tk)


def _column_sharded_matmul(name, h, w_own, w_slabs, chip, finish, slab_major):
    seq, d_model = h.shape
    cols = w_own.shape[1]
    tm, tn, tk = _mm_tiles(seq, cols, d_model)
    assert tk == d_model
    nbj = cols // tn

    def body_own(chip_ref, a_ref, b_ref, o_ref):
        o_ref[...] = finish(jnp.dot(a_ref[...], b_ref[...], preferred_element_type=F32)).astype(BF16)

    def body_rest(chip_ref, a_ref, b_ref, own_ref, o_ref):
        o_ref[...] = finish(jnp.dot(a_ref[...], b_ref[...], preferred_element_type=F32)).astype(BF16)

    def slab(j, chip_ref):
        return (chip_ref[0] + 1 + j // nbj) % N_CHIPS

    if slab_major:
        out_shape = jax.ShapeDtypeStruct((N_CHIPS, seq, cols), BF16)
        out_block = (None, tm, tn)
        own_out = lambda i, j, chip_ref: (chip_ref[0], i, j)
        rest_out = lambda i, j, chip_ref: (slab(j, chip_ref), i, j % nbj)
    else:
        out_shape = jax.ShapeDtypeStruct((seq, N_CHIPS * cols), BF16)
        out_block = (tm, tn)
        own_out = lambda i, j, chip_ref: (i, chip_ref[0] * nbj + j)
        rest_out = lambda i, j, chip_ref: (i, slab(j, chip_ref) * nbj + j % nbj)
    rows = pl.BlockSpec((tm, tk), lambda i, j, chip_ref: (i, 0))
    own = pl.pallas_call(
        body_own, name=f"{name}_own",
        grid_spec=pltpu.PrefetchScalarGridSpec(
            num_scalar_prefetch=1, grid=(seq // tm, nbj),
            in_specs=[rows, pl.BlockSpec((tk, tn), lambda i, j, chip_ref: (0, j))],
            out_specs=pl.BlockSpec(out_block, own_out)),
        out_shape=out_shape, compiler_params=_params("parallel", "parallel"),
    )(chip, h, w_own)
    return pl.pallas_call(
        body_rest, name=f"{name}_rest",
        grid_spec=pltpu.PrefetchScalarGridSpec(
            num_scalar_prefetch=1, grid=(seq // tm, (N_CHIPS - 1) * nbj),
            in_specs=[rows, pl.BlockSpec((None, tk, tn), lambda i, j, chip_ref: (slab(j, chip_ref), 0, j % nbj)), ANY],
            out_specs=pl.BlockSpec(out_block, rest_out)),
        out_shape=out_shape, input_output_aliases={3: 0},
        compiler_params=_params("parallel", "parallel"),
    )(chip, h, w_slabs, own)


def _in_projection(h, w_own, w_slabs, chip):
    return _column_sharded_matmul("in_projection", h, w_own, w_slabs, chip, lambda acc: acc, True)


def _residual_projection(name, a, w, res, gate, prologue=None):
    seq, kdim = a.shape
    d_model = w.shape[1]
    if kdim <= MM_WHOLE_K:
        tm, tn, tk = _mm_tiles(seq, d_model, kdim, tn_pref=MM_TILE_N // 2)
    else:
        tm, tn, tk = _mm_tiles(seq, d_model, kdim, k_block=MM_TILE_K // 2)
    tile = pl.BlockSpec((tm, tn), lambda i, j, k: (i, j))
    return _matmul(
        name, a, w, (res, gate), grid=(seq // tm, d_model // tn, kdim // tk), tiles=(tm, tn, tk), dims=NN,
        a_spec=pl.BlockSpec((tm, tk), lambda i, j, k: (i, k)),
        b_spec=pl.BlockSpec((tk, tn), lambda i, j, k: (k, j)),
        extra_specs=(tile, pl.BlockSpec((1, tn), lambda i, j, k: (0, j))),
        out_shape=[jax.ShapeDtypeStruct((seq, d_model), F32), jax.ShapeDtypeStruct((seq, d_model), BF16)],
        out_specs=[tile, tile],
        epilogue=_residual_epilogue, prologue=prologue)


def _mlp_in(h, w_own, w_slabs, chip):
    return _column_sharded_matmul("mlp_in", h, w_own, w_slabs, chip, lambda acc: jnp.maximum(acc, 0.0), False)


def _grad_hidden(dmlp, w2, act, after=()):
    seq, d_model = dmlp.shape
    ff = w2.shape[0]
    tm, tn, tk = _mm_tiles(seq, ff, d_model)

    def epilogue(acc, extra_refs, out_refs):
        out_refs[0][...] = (acc * (2.0 * extra_refs[0][...].astype(F32))).astype(BF16)

    tile = pl.BlockSpec((tm, tn), lambda i, j, k: (i, j))
    return _matmul(
        "grad_hidden", dmlp, w2, (act,), grid=(seq // tm, ff // tn, d_model // tk), tiles=(tm, tn, tk), dims=NT,
        a_spec=pl.BlockSpec((tm, tk), lambda i, j, k: (i, k)),
        b_spec=pl.BlockSpec((tn, tk), lambda i, j, k: (j, k)),
        extra_specs=(tile,),
        out_shape=[jax.ShapeDtypeStruct((seq, ff), BF16)], out_specs=[tile],
        epilogue=epilogue, after=after)[0]


def _weight_grad(name, a, b, prologue=None):
    seq, m = a.shape
    n = b.shape[1]
    tm, tn, tk = _mm_tiles(m, n, seq)
    return _matmul(
        name, a, b, (), grid=(m // tm, n // tn, seq // tk), tiles=(tm, tn, tk), dims=TN,
        a_spec=pl.BlockSpec((tk, tm), lambda i, j, k: (k, i)),
        b_spec=pl.BlockSpec((tk, tn), lambda i, j, k: (k, j)),
        extra_specs=(),
        out_shape=[jax.ShapeDtypeStruct((m, n), BF16)],
        out_specs=[pl.BlockSpec((tm, tn), lambda i, j, k: (i, j))],
        epilogue=_store(BF16), prologue=prologue)[0]


def _weight_grad_slabs(name, a, b, slab_cols, after=()):
    seq, m = a.shape
    cols = b.shape[2] if slab_cols is None else slab_cols
    tm, tn, tk = _mm_tiles(m, cols, seq)
    nbj = cols // tn
    if slab_cols is None:
        b_spec = pl.BlockSpec((None, tk, tn), lambda i, j, k: (j // nbj, k, j % nbj))
    else:
        b_spec = pl.BlockSpec((tk, tn), lambda i, j, k: (k, j))
    return _matmul(
        name, a, b, (), grid=(m // tm, N_CHIPS * nbj, seq // tk), tiles=(tm, tn, tk), dims=TN,
        a_spec=pl.BlockSpec((tk, tm), lambda i, j, k: (k, i)),
        b_spec=b_spec, extra_specs=(),
        out_shape=[jax.ShapeDtypeStruct((N_CHIPS, m, cols), BF16)],
        out_specs=[pl.BlockSpec((None, tm, tn), lambda i, j, k: (j // nbj, i, j % nbj))],
        epilogue=_store(BF16), after=after)[0]


def _grad_input_slabs(name, dy, w_slabs, after=()):
    _, d_model, cols = w_slabs.shape
    seq = dy.shape[1] if dy.ndim == 3 else dy.shape[0]
    tm, tn, tk = _mm_tiles(seq, d_model, N_CHIPS * cols, k_block=cols)
    nbk = cols // tk
    if dy.ndim == 3:
        a_spec = pl.BlockSpec((None, tm, tk), lambda i, j, k: (k // nbk, i, k % nbk))
    else:
        a_spec = pl.BlockSpec((tm, tk), lambda i, j, k: (i, k))
    return _matmul(
        name, dy, w_slabs, (), grid=(seq // tm, d_model // tn, N_CHIPS * nbk), tiles=(tm, tn, tk), dims=NT,
        a_spec=a_spec,
        b_spec=pl.BlockSpec((None, tn, tk), lambda i, j, k: (k // nbk, j, k % nbk)),
        extra_specs=(),
        out_shape=[jax.ShapeDtypeStruct((seq, d_model), F32)],
        out_specs=[pl.BlockSpec((tm, tn), lambda i, j, k: (i, j))],
        epilogue=_store(F32), after=after)[0]


def _grad_input(name, dy, w):
    seq, n = dy.shape
    kdim = w.shape[0]
    tm, tn, tk = _mm_tiles(seq, kdim, n)
    return _matmul(
        name, dy, w, (), grid=(seq // tm, kdim // tn, n // tk), tiles=(tm, tn, tk), dims=NT,
        a_spec=pl.BlockSpec((tm, tk), lambda i, j, k: (i, k)),
        b_spec=pl.BlockSpec((tn, tk), lambda i, j, k: (j, k)),
        extra_specs=(),
        out_shape=[jax.ShapeDtypeStruct((seq, kdim), F32)],
        out_specs=[pl.BlockSpec((tm, tn), lambda i, j, k: (i, j))],
        epilogue=_store(F32))[0]


ROW_TILE = 128


def _norm_modulate(name, xin, g, scale, shift):
    seq, d_model = xin.shape
    tr = _tile(seq, ROW_TILE)

    def body(x_ref, g_ref, sc_ref, sh_ref, h_ref):
        xv = x_ref[...]
        r = lax.rsqrt(jnp.mean(xv * xv, axis=-1, keepdims=True) + NORM_EPS)
        h_ref[...] = (((xv * r) * g_ref[...]) * (1.0 + sc_ref[...]) + sh_ref[...]).astype(BF16)

    row = pl.BlockSpec((tr, d_model), lambda i: (i, 0))
    vec = pl.BlockSpec((1, d_model), lambda i: (0, 0))
    return pl.pallas_call(
        body, name=name, grid=(seq // tr,),
        in_specs=[row, vec, vec, vec], out_specs=row,
        out_shape=jax.ShapeDtypeStruct((seq, d_model), BF16),
        compiler_params=_params("parallel"),
    )(xin, g, scale, shift)


def _loss_head(x2, target, final_g, mlp, gate2):
    seq, d_model = x2.shape
    tr = _tile(seq, ROW_TILE)

    def body(x_ref, t_ref, fg_ref, mlp_ref, gate_ref, dx_ref, dmlp_ref, gfg_ref, dgate_ref, sq_ref):
        @pl.when(pl.program_id(0) == 0)
        def _():
            gfg_ref[...] = jnp.zeros_like(gfg_ref)
            dgate_ref[...] = jnp.zeros_like(dgate_ref)
            sq_ref[...] = jnp.zeros_like(sq_ref)

        xv = x_ref[...]
        fg = fg_ref[...]
        r = lax.rsqrt(jnp.mean(xv * xv, axis=-1, keepdims=True) + NORM_EPS)
        n = xv * r
        err = n * fg - t_ref[...]
        sq_ref[...] += jnp.sum(err * err, axis=0, keepdims=True)
        dy = err * (1.0 / d_model)
        gfg_ref[...] += jnp.sum(dy * n, axis=0, keepdims=True)
        dn = dy * fg
        dx = r * (dn - n * jnp.mean(dn * n, axis=-1, keepdims=True))
        dx_ref[...] = dx
        dgate_ref[...] += jnp.sum(dx * mlp_ref[...].astype(F32), axis=0, keepdims=True)
        dmlp_ref[...] = (dx * gate_ref[...]).astype(BF16)

    row = pl.BlockSpec((tr, d_model), lambda i: (i, 0))
    vec = pl.BlockSpec((1, d_model), lambda i: (0, 0))
    vshape = jax.ShapeDtypeStruct((1, d_model), F32)
    return pl.pallas_call(
        body, name="loss_head", grid=(seq // tr,),
        in_specs=[row, row, vec, row, vec], out_specs=[row, row, vec, vec, vec],
        out_shape=[jax.ShapeDtypeStruct((seq, d_model), F32), jax.ShapeDtypeStruct((seq, d_model), BF16),
                   vshape, vshape, vshape],
        compiler_params=_params("arbitrary"),
    )(x2, target, final_g, mlp, gate2)


def _norm_modulate_backward(name, dh, xin, g, scale, dres, branch=None, gate=None, after=()):
    seq, d_model = xin.shape
    tr = _tile(seq, ROW_TILE)
    with_branch = branch is not None
    n_in = (7 if with_branch else 5) + len(after)

    def body(*refs):
        dh_ref, x_ref, g_ref, sc_ref, dres_ref = refs[:5]
        outs = refs[n_in:]
        dx_ref, dsc_ref, dsh_ref, dg_ref = outs[:4]

        @pl.when(pl.program_id(0) == 0)
        def _():
            for ref in outs[1:5] if with_branch else outs[1:4]:
                ref[...] = jnp.zeros_like(ref)

        xv = x_ref[...]
        gv = g_ref[...]
        dhv = dh_ref[...]
        r = lax.rsqrt(jnp.mean(xv * xv, axis=-1, keepdims=True) + NORM_EPS)
        xn = xv * r
        dsh_ref[...] += jnp.sum(dhv, axis=0, keepdims=True)
        dsc_ref[...] += jnp.sum(dhv * (xn * gv), axis=0, keepdims=True)
        t = dhv * (1.0 + sc_ref[...])
        dg_ref[...] += jnp.sum(t * xn, axis=0, keepdims=True)
        dxn = t * gv
        dx = dres_ref[...] + r * (dxn - xn * jnp.mean(dxn * xn, axis=-1, keepdims=True))
        dx_ref[...] = dx
        if with_branch:
            br_ref, gate_ref = refs[5:7]
            dgate_ref, dbr_ref = outs[4:6]
            dgate_ref[...] += jnp.sum(dx * br_ref[...].astype(F32), axis=0, keepdims=True)
            dbr_ref[...] = (dx * gate_ref[...]).astype(BF16)

    row = pl.BlockSpec((tr, d_model), lambda i: (i, 0))
    vec = pl.BlockSpec((1, d_model), lambda i: (0, 0))
    vshape = jax.ShapeDtypeStruct((1, d_model), F32)
    in_specs = [row, row, vec, vec, row]
    out_specs = [row, vec, vec, vec]
    out_shape = [jax.ShapeDtypeStruct((seq, d_model), F32), vshape, vshape, vshape]
    args = [dh, xin, g, scale, dres]
    if with_branch:
        in_specs += [row, vec]
        out_specs += [vec, row]
        out_shape += [vshape, jax.ShapeDtypeStruct((seq, d_model), BF16)]
        args += [branch, gate]
    in_specs += [ANY] * len(after)
    args += list(after)
    return pl.pallas_call(
        body, name=name, grid=(seq // tr,),
        in_specs=in_specs, out_specs=out_specs, out_shape=out_shape,
        compiler_params=_params("arbitrary"),
    )(*args)


def _shifted(v, k, t):
    seq = v.shape[0]
    if k == 0:
        return v
    moved = pltpu.roll(v, (-k) % seq, 0)
    return jnp.where((t + k >= 0) & (t + k < seq), moved, 0.0)


def _window_sum(v, offsets, t):
    acc = None
    for k in offsets:
        term = _shifted(v, k, t)
        acc = term if acc is None else acc + term
    return acc


def _window_count(seq, half):
    t = lax.broadcasted_iota(jnp.int32, (seq, 1), 0)
    return (jnp.minimum(t + half, seq) - jnp.maximum(t - half, 0)).astype(F32)


def _pool_forward(proj, group_dim):
    _, seq, cols = proj.shape
    tl = _tile(group_dim, 256)
    nbl = group_dim // tl
    n_groups = cols // group_dim

    def body(v_ref, o_ref):
        g = pl.program_id(0)
        for gi, window in enumerate(POOL_WINDOWS[:n_groups]):
            @pl.when(g == gi)
            def _(window=window):
                half = window // 2
                v = v_ref[...].astype(F32)
                t = lax.broadcasted_iota(jnp.int32, v.shape, 0)
                total = _window_sum(v, range(-half, half), t)
                o_ref[...] = (total / _window_count(seq, half) - v).astype(BF16)

    return pl.pallas_call(
        body, name="pool_forward", grid=(n_groups, nbl),
        in_specs=[pl.BlockSpec((None, seq, tl), lambda g, j: (0, 0, g * nbl + j))],
        out_specs=pl.BlockSpec((seq, tl), lambda g, j: (0, g * nbl + j)),
        out_shape=jax.ShapeDtypeStruct((seq, cols), BF16),
        compiler_params=_params("parallel", "parallel"),
    )(proj)


def _group_matrix(w_ref):
    return jnp.concatenate([w_ref[r] for r in range(N_CHIPS)], axis=0)


def _pool_mix_forward(pooled, w_pm, pool_scale, gnorm_g, d_model):
    seq, cols = pooled.shape
    _, n_groups, shard_rows, group_dim = w_pm.shape
    tm = _tile(seq, 512)

    def body(p_ref, w_ref, ps_ref, g_ref, o_ref):
        a = jnp.dot(p_ref[...], _group_matrix(w_ref), preferred_element_type=F32) * ps_ref[...]
        ra = lax.rsqrt(jnp.mean(a * a, axis=-1, keepdims=True) + NORM_EPS)
        o_ref[...] = ((a * ra) * g_ref[...]).astype(BF16)

    tile = pl.BlockSpec((tm, group_dim), lambda g, i: (i, g))
    vec = pl.BlockSpec((1, group_dim), lambda g, i: (0, g))
    return pl.pallas_call(
        body, name="pool_mix_forward", grid=(n_groups, seq // tm),
        in_specs=[tile, pl.BlockSpec((N_CHIPS, None, shard_rows, group_dim), lambda g, i: (0, g, 0, 0)), vec, vec],
        out_specs=tile,
        out_shape=jax.ShapeDtypeStruct((seq, d_model), BF16),
        compiler_params=_params("parallel", "parallel"),
    )(pooled, w_pm, pool_scale, gnorm_g)


def _conv_parts(b_ref, c_ref, u_ref, w_ref, bias_ref):
    bv = b_ref[...].astype(F32)
    cu = c_ref[...].astype(F32) * u_ref[...].astype(F32)
    t = lax.broadcasted_iota(jnp.int32, cu.shape, 0)
    prev, nxt = _shifted(cu, -1, t), _shifted(cu, 1, t)
    w = w_ref[...]
    conv = w[0:1] * prev + w[1:2] * cu + w[2:3] * nxt + bias_ref[...]
    return bv, cu, prev, nxt, conv, w, t


def _conv_forward(proj, conv_w, conv_b, gnorm_g, mixed):
    _, seq, cols = proj.shape
    tl = CONV_HEAD_DIM
    first = cols // tl

    def body(b_ref, c_ref, u_ref, w_ref, bias_ref, g_ref, mixed_ref, o_ref):
        bv, _, _, _, conv, _, _ = _conv_parts(b_ref, c_ref, u_ref, w_ref, bias_ref)
        bo = bv * conv
        rb = lax.rsqrt(jnp.mean(bo * bo, axis=-1, keepdims=True) + NORM_EPS)
        o_ref[...] = ((bo * rb) * g_ref[...]).astype(BF16)

    def slab(s):
        return pl.BlockSpec((None, seq, tl), lambda j, s=s: (s, 0, j))

    vec = pl.BlockSpec((1, tl), lambda j: (0, j))
    return pl.pallas_call(
        body, name="conv_forward", grid=(cols // tl,),
        in_specs=[slab(1), slab(2), slab(3), pl.BlockSpec((3, tl), lambda j: (0, j)), vec, vec, ANY],
        out_specs=pl.BlockSpec((seq, tl), lambda j: (0, first + j)),
        out_shape=jax.ShapeDtypeStruct(mixed.shape, mixed.dtype),
        input_output_aliases={6: 0},
        compiler_params=_params("parallel"),
    )(proj, proj, proj, conv_w, conv_b, gnorm_g, mixed)


def _pool_mix_backward(dmixed, pooled, w_pm, pool_scale, gnorm_g, after=()):
    seq, cols = pooled.shape
    _, n_groups, shard_rows, group_dim = w_pm.shape
    tm = _tile(seq, 512)

    def body(dm_ref, p_ref, w_ref, ps_ref, g_ref, *rest):
        dp_ref, dpm_ref, gg_ref, gps_ref = rest[-4:]

        @pl.when(pl.program_id(1) == 0)
        def _():
            gg_ref[...] = jnp.zeros_like(gg_ref)
            gps_ref[...] = jnp.zeros_like(gps_ref)

        w = _group_matrix(w_ref)
        ps = ps_ref[...]
        a_pre = jnp.dot(p_ref[...], w, preferred_element_type=F32)
        a = a_pre * ps
        ra = lax.rsqrt(jnp.mean(a * a, axis=-1, keepdims=True) + NORM_EPS)
        an = a * ra
        dm = dm_ref[...]
        gg_ref[...] += jnp.sum(dm * an, axis=0, keepdims=True)
        dan = dm * g_ref[...]
        da = ra * (dan - an * jnp.mean(dan * an, axis=-1, keepdims=True))
        gps_ref[...] += jnp.sum(da * a_pre, axis=0, keepdims=True)
        dpm = (da * ps).astype(BF16)
        dpm_ref[...] = dpm
        dp_ref[...] = lax.dot_general(dpm, w, NT, preferred_element_type=F32)

    tile = pl.BlockSpec((tm, group_dim), lambda g, i: (i, g))
    vec = pl.BlockSpec((1, group_dim), lambda g, i: (0, g))
    vshape = jax.ShapeDtypeStruct((1, cols), F32)
    return pl.pallas_call(
        body, name="pool_mix_backward", grid=(n_groups, seq // tm),
        in_specs=[tile, tile, pl.BlockSpec((N_CHIPS, None, shard_rows, group_dim), lambda g, i: (0, g, 0, 0)),
                  vec, vec] + [ANY] * len(after),
        out_specs=[tile, tile, vec, vec],
        out_shape=[jax.ShapeDtypeStruct((seq, cols), F32), jax.ShapeDtypeStruct((seq, cols), BF16), vshape, vshape],
        compiler_params=_params("parallel", "arbitrary"),
    )(dmixed, pooled, w_pm, pool_scale, gnorm_g, *after)


def _pool_mix_weight_grad(pooled, dpm, n_groups):
    seq, cols = pooled.shape
    group_dim = cols // n_groups
    shard_rows = group_dim // N_CHIPS
    tk = _tile(seq, 1024)
    gk = seq // tk

    def body(p_ref, d_ref, o_ref, acc_ref):
        k = pl.program_id(1)

        @pl.when(k == 0)
        def _():
            acc_ref[...] = jnp.zeros_like(acc_ref)

        acc_ref[...] += lax.dot_general(p_ref[...], d_ref[...], TN, preferred_element_type=F32)

        @pl.when(k == gk - 1)
        def _():
            for r in range(N_CHIPS):
                o_ref[r] = acc_ref[r * shard_rows:(r + 1) * shard_rows, :].astype(BF16)

    tile = pl.BlockSpec((tk, group_dim), lambda g, k: (k, g))
    return pl.pallas_call(
        body, name="pool_mix_weight_grad", grid=(n_groups, gk),
        in_specs=[tile, tile],
        out_specs=pl.BlockSpec((N_CHIPS, None, shard_rows, group_dim), lambda g, k: (0, g, 0, 0)),
        out_shape=jax.ShapeDtypeStruct((N_CHIPS, n_groups, shard_rows, group_dim), BF16),
        scratch_shapes=[pltpu.VMEM((group_dim, group_dim), F32)],
        compiler_params=_params("parallel", "arbitrary"),
    )(pooled, dpm)


def _mixers_backward(dpooled, dmixed, proj, conv_w, conv_b, gnorm_g, group_dim):
    _, seq, cols = proj.shape
    tl = CONV_HEAD_DIM
    first = cols // tl
    per_group = group_dim // tl
    n_groups = cols // group_dim

    def body(dp_ref, dm_ref, b_ref, c_ref, u_ref, w_ref, bias_ref, g_ref, o_ref, gg_ref, gb_ref, gw_ref):
        j = pl.program_id(0)
        for gi, window in enumerate(POOL_WINDOWS[:n_groups]):
            @pl.when(j // per_group == gi)
            def _(window=window):
                half = window // 2
                dp = dp_ref[...]
                t = lax.broadcasted_iota(jnp.int32, dp.shape, 0)
                dq = dp / _window_count(seq, half)
                o_ref[0] = (_window_sum(dq, range(-half + 1, half + 1), t) - dp).astype(BF16)

        bv, cu, prev, nxt, conv, w, t = _conv_parts(b_ref, c_ref, u_ref, w_ref, bias_ref)
        bo = bv * conv
        rb = lax.rsqrt(jnp.mean(bo * bo, axis=-1, keepdims=True) + NORM_EPS)
        bn = bo * rb
        dm = dm_ref[...]
        gg_ref[...] = jnp.sum(dm * bn, axis=0, keepdims=True)
        dbn = dm * g_ref[...]
        dbo = rb * (dbn - bn * jnp.mean(dbn * bn, axis=-1, keepdims=True))
        o_ref[1] = (dbo * conv).astype(BF16)
        dconv = dbo * bv
        gb_ref[...] = jnp.sum(dconv, axis=0, keepdims=True)
        gw_ref[0:1, :] = jnp.sum(dconv * prev, axis=0, keepdims=True)
        gw_ref[1:2, :] = jnp.sum(dconv * cu, axis=0, keepdims=True)
        gw_ref[2:3, :] = jnp.sum(dconv * nxt, axis=0, keepdims=True)
        dcu = w[0:1] * _shifted(dconv, 1, t) + w[1:2] * dconv + w[2:3] * _shifted(dconv, -1, t)
        o_ref[2] = (dcu * u_ref[...].astype(F32)).astype(BF16)
        o_ref[3] = (dcu * c_ref[...].astype(F32)).astype(BF16)

    def slab(s):
        return pl.BlockSpec((None, seq, tl), lambda j, s=s: (s, 0, j))

    vec = pl.BlockSpec((1, tl), lambda j: (0, j))
    rows3 = pl.BlockSpec((3, tl), lambda j: (0, j))
    vshape = jax.ShapeDtypeStruct((1, cols), F32)
    return pl.pallas_call(
        body, name="mixers_backward", grid=(cols // tl,),
        in_specs=[pl.BlockSpec((seq, tl), lambda j: (0, j)), pl.BlockSpec((seq, tl), lambda j: (0, first + j)),
                  slab(1), slab(2), slab(3), rows3, vec, vec],
        out_specs=[pl.BlockSpec((N_CHIPS, seq, tl), lambda j: (0, 0, j)), vec, vec, rows3],
        out_shape=[jax.ShapeDtypeStruct((N_CHIPS, seq, cols), BF16), vshape, vshape,
                   jax.ShapeDtypeStruct((3, cols), F32)],
        compiler_params=_params("parallel"),
    )(dpooled, dmixed, proj, proj, proj, conv_w, conv_b, gnorm_g)


class _GradReduction:
    def __init__(self, tag, grads, states, pair_id, scatter_id, position):
        self.tag, self.grads, self.states = tag, grads, states
        self.pair_id, self.scatter_id = pair_id, scatter_id
        self.chip, self.core, self.other_core = position

    def exchange(self):
        self.received = _exchange_halves(f"exchange_{self.tag}", self.grads, self.pair_id)

    def combine(self, after=()):
        self.parts = [_add_half(f"add_half_{self.tag}_{a}", g, r, self.core, after)
                      for a, (g, r) in enumerate(zip(self.grads, self.received))]

    def scatter(self):
        self.landed = _scatter_partials(f"scatter_{self.tag}", self.parts, self.scatter_id)

    def reduce(self, after=()):
        self.reduced = [_reduce_chips(f"reduce_chips_{self.tag}_{a}", p, l, self.chip, after)
                        for a, (p, l) in enumerate(zip(self.parts, self.landed))]

    def swap(self):
        self.swapped = _swap_reduced(f"swap_{self.tag}", self.reduced, self.pair_id)

    def update_mine(self):
        self.mine = [_adamw_half(f"adamw_mine_{self.tag}_{a}", g, *state, self.core)
                     for a, (g, state) in enumerate(zip(self.reduced, self.states))]

    def update_other(self, after=()):
        self.results = [_adamw_half(f"adamw_other_{self.tag}_{a}", g, *state, self.other_core, done, after)
                        for a, (g, state, done) in enumerate(zip(self.swapped, self.states, self.mine))]

    def token(self, stage):
        first = getattr(self, stage)[0]
        return first if not isinstance(first, (list, tuple)) else first[0]


def kernel(x, c, w_ada, b_ada, norm1_g, w_in, pool_mix_w, pool_scale, conv_w, conv_b, gnorm_pool_g, gnorm_conv_g, w_out, norm2_g, w_mlp_in, w_mlp_out, final_g, loss_target, m_w_ada, m_b_ada, m_norm1_g, m_w_in, m_pool_mix_w, m_pool_scale, m_conv_w, m_conv_b, m_gnorm_pool_g, m_gnorm_conv_g, m_w_out, m_norm2_g, m_w_mlp_in, m_w_mlp_out, m_final_g, v_w_ada, v_b_ada, v_norm1_g, v_w_in, v_pool_mix_w, v_pool_scale, v_conv_w, v_conv_b, v_gnorm_pool_g, v_gnorm_conv_g, v_w_out, v_norm2_g, v_w_mlp_in, v_w_mlp_out, v_final_g):
    seq, d_model = x.shape[1], x.shape[2]
    cols = w_in.shape[2]
    n_groups, group_dim = pool_mix_w.shape[1], pool_mix_w.shape[3]
    shard_rows = pool_mix_w.shape[2]
    ff_cols = w_mlp_in.shape[2]
    ada_cols = w_ada.shape[2]
    conv_shard = conv_w.shape[2]
    assert pool_scale.shape[1] == cols and conv_b.shape[1] == cols and n_groups * group_dim == cols
    assert cols % CONV_HEAD_DIM == 0 and group_dim % CONV_HEAD_DIM == 0 and shard_rows * N_CHIPS == group_dim

    ix, iy, ic = _position()
    chip = 2 * ix + iy
    me = 4 * ix + 2 * iy + ic
    position = tuple(jnp.reshape(v, (1,)).astype(jnp.int32) for v in (chip, ic, 1 - ic))

    xs, target = x[0], loss_target[0]
    final_row = final_g.reshape(1, d_model)

    small = _gather_flat("gather_cond", jnp.concatenate([c[0], conv_w[0].reshape(-1)]))
    c_all = small[:, :d_model]
    conv_w_full = jnp.concatenate(
        [small[2 * j, d_model:].reshape(3, conv_shard) for j in range(N_CHIPS)], axis=1)
    b_cols = lax.dynamic_slice_in_dim(b_ada, chip * ada_cols, ada_cols, axis=1)
    mod_part = _ada_forward(c_all, w_ada[0], b_cols)
    mod_all = _gather_flat("gather_mod", mod_part.reshape(-1)).reshape(N_DEV, N_DEV, ada_cols)
    mod = jnp.concatenate(
        [lax.dynamic_slice_in_dim(mod_all[2 * j], me, 1, axis=0) for j in range(N_CHIPS)], axis=1)
    shift1, scale1, gate1, shift2, scale2, gate2 = [mod[:, i * d_model:(i + 1) * d_model] for i in range(N_MOD)]

    shards = [w_in[0].astype(BF16), pool_mix_w[0].reshape(n_groups * shard_rows, group_dim).astype(BF16),
              w_out[0].astype(BF16), w_mlp_in[0].astype(BF16), w_mlp_out[0].astype(BF16)]
    wg_in, wg_pm = _gather_weights("gather_w_in", shards[0:2], 1)
    (wg_out,) = _gather_weights("gather_w_out", shards[2:3], 2)
    (wg_1,) = _gather_weights("gather_w_mlp_in", shards[3:4], 3)
    (wg_2,) = _gather_weights("gather_w_mlp_out", shards[4:5], 4)
    wg_in, wg_pm, wg_out, wg_1, wg_2 = [
        _place_own(f"place_own_{a}", g, s, position[0])
        for a, (g, s) in enumerate(zip([wg_in, wg_pm, wg_out, wg_1, wg_2], shards))]
    wg_pm = wg_pm.reshape(N_CHIPS, n_groups, shard_rows, group_dim)
    wg_out = wg_out.reshape(d_model, d_model)
    wg_2 = wg_2.reshape(N_CHIPS * ff_cols, d_model)

    h1 = _norm_modulate("norm_modulate_1", xs, norm1_g, scale1, shift1)
    proj = _in_projection(h1, shards[0], wg_in, position[0])
    pooled = _pool_forward(proj, group_dim)
    mixed = _pool_mix_forward(pooled, wg_pm, pool_scale, gnorm_pool_g, d_model)
    mixed = _conv_forward(proj, conv_w_full, conv_b, gnorm_conv_g, mixed)
    x1, attn = _residual_projection("out_projection", mixed, wg_out, xs, gate1)
    h2 = _norm_modulate("norm_modulate_2", x1, norm2_g, scale2, shift2)
    act = _mlp_in(h2, shards[3], wg_1, position[0])
    x2, mlp = _residual_projection("mlp_out", act, wg_2, x1, gate2, prologue=_square)

    dx2, dmlp, g_final, dgate2, sq_err = _loss_head(x2, target, final_row, mlp, gate2)
    gw_2 = _weight_grad("grad_w_mlp_out", act, dmlp, prologue=_square)
    red_2 = _GradReduction("w_mlp_out", [gw_2.reshape(N_CHIPS, ff_cols, d_model)],
                           [(w_mlp_out[0], m_w_mlp_out[0], v_w_mlp_out[0])], 8, 12, position)
    red_2.exchange()
    dhid = _grad_hidden(dmlp, wg_2, act, after=[gw_2])
    red_2.combine(after=[dhid])
    red_2.scatter()
    gw_1 = _weight_grad_slabs("grad_w_mlp_in", h2, dhid, ff_cols, after=[red_2.token("parts")])
    red_1 = _GradReduction("w_mlp_in", [gw_1], [(w_mlp_in[0], m_w_mlp_in[0], v_w_mlp_in[0])], 7, 11, position)
    red_1.exchange()
    dh2 = _grad_input_slabs("grad_h2", dhid, wg_1, after=[gw_1])
    red_2.reduce(after=[dh2])
    red_2.swap()
    red_2.update_mine()
    red_1.combine(after=[red_2.token("mine")])
    red_1.scatter()
    dx1, dscale2, dshift2, g_norm2, dgate1, dattn = _norm_modulate_backward(
        "norm_modulate_backward_2", dh2, x1, norm2_g, scale2, dx2, attn, gate1, after=[red_1.token("parts")])
    gw_out = _weight_grad("grad_w_out", mixed, dattn)
    red_out = _GradReduction("w_out", [gw_out.reshape(N_CHIPS, d_model // N_CHIPS, d_model)],
                             [(w_out[0], m_w_out[0], v_w_out[0])], 6, 10, position)
    red_out.exchange()
    dmixed = _grad_input("grad_mixed", dattn, wg_out)
    dpooled, dpm, g_gpool, g_pscale = _pool_mix_backward(dmixed, pooled, wg_pm, pool_scale, gnorm_pool_g)
    gw_pm = _pool_mix_weight_grad(pooled, dpm, n_groups)
    dproj, g_gconv, g_convb, g_convw = _mixers_backward(
        dpooled, dmixed, proj, conv_w_full, conv_b, gnorm_conv_g, group_dim)
    gw_in = _weight_grad_slabs("grad_w_in", h1, dproj, None)
    pm2d = (n_groups * shard_rows, group_dim)
    red_in = _GradReduction(
        "w_in", [gw_in, gw_pm.reshape((N_CHIPS,) + pm2d)],
        [(w_in[0], m_w_in[0], v_w_in[0]),
         (pool_mix_w[0].reshape(pm2d), m_pool_mix_w[0].reshape(pm2d), v_pool_mix_w[0].reshape(pm2d))],
        5, 9, position)
    red_in.exchange()
    red_1.reduce(after=[gw_in])
    red_1.swap()
    red_1.update_mine()
    red_out.combine(after=[red_1.token("mine")])
    red_out.scatter()
    red_in.combine(after=[red_out.token("parts")])
    red_in.scatter()
    dh1 = _grad_input_slabs("grad_h1", dproj, wg_in, after=[red_in.token("parts")])
    red_2.update_other(after=[dh1])
    grad_x, dscale1, dshift1, g_norm1 = _norm_modulate_backward(
        "norm_modulate_backward_1", dh1, xs, norm1_g, scale1, dx1, after=[red_2.token("results")])
    red_out.reduce(after=[grad_x])
    red_out.swap()
    red_out.update_mine()
    red_1.update_other(after=[red_out.token("mine")])

    mine = jnp.concatenate(
        [dshift1, dscale1, dgate1, dshift2, dscale2, dgate2, g_norm1, g_norm2, g_final, sq_err,
         g_pscale, g_convb, g_gpool, g_gconv, g_convw.reshape(1, 3 * cols)], axis=1)
    gathered = _gather_flat("gather_small", mine.reshape(-1))
    sums, loss = _reduce_small(gathered, d_model)
    n_rep = (N_MOD + 3) * d_model
    g_rep = jnp.concatenate([sums[:, :n_rep], sums[:, n_rep + d_model:n_rep + d_model + 4 * cols]], axis=1)
    n_small = g_rep.shape[1]

    def pack(b, n1, n2, fg, ps, cb, gp, gc):
        return jnp.concatenate([b, n1, n2, fg.reshape(1, d_model), ps, cb, gp, gc], axis=1).reshape(8, n_small // 8)

    d_rep, m_rep, v_rep = _adamw(
        "adamw_small", g_rep.reshape(8, n_small // 8),
        pack(b_ada, norm1_g, norm2_g, final_g, pool_scale, conv_b, gnorm_pool_g, gnorm_conv_g),
        pack(m_b_ada, m_norm1_g, m_norm2_g, m_final_g, m_pool_scale, m_conv_b, m_gnorm_pool_g, m_gnorm_conv_g),
        pack(v_b_ada, v_norm1_g, v_norm2_g, v_final_g, v_pool_scale, v_conv_b, v_gnorm_pool_g, v_gnorm_conv_g))

    def unpack(flat):
        flat = flat.reshape(1, n_small)
        sizes = [N_MOD * d_model, d_model, d_model, d_model, cols, cols, cols, cols]
        parts, at = [], 0
        for size in sizes:
            parts.append(flat[:, at:at + size])
            at += size
        parts[3] = parts[3].reshape(d_model)
        return parts

    g_convw_full = sums[:, n_rep + d_model + 4 * cols:].reshape(3, cols)
    g_convw_mine = lax.dynamic_slice_in_dim(g_convw_full, chip * conv_shard, conv_shard, axis=1)
    d_convw, m_convw, v_convw = _adamw("adamw_conv_w", g_convw_mine, conv_w[0], m_conv_w[0], v_conv_w[0])

    dmod_cols = lax.dynamic_slice_in_dim(gathered[:, :N_MOD * d_model], chip * ada_cols, ada_cols, axis=1)
    g_ada, d_ada, mn_ada, vn_ada = _ada_backward(c_all.T, dmod_cols, w_ada[0], m_w_ada[0], v_w_ada[0])

    red_in.reduce(after=[g_ada, red_1.token("results")])
    red_in.swap()
    red_in.update_mine()
    red_out.update_other(after=[red_in.token("mine")])
    red_in.update_other(after=[red_out.token("results")])

    small_parts = [unpack(g_rep), unpack(d_rep), unpack(m_rep), unpack(v_rep)]
    ada_parts = [g_ada, d_ada, mn_ada, vn_ada]
    convw_parts = [g_convw_mine, d_convw, m_convw, v_convw]

    def ordered(k):
        b, n1, n2, fg, ps, cb, gp, gc = small_parts[k]
        return [ada_parts[k][None], b, n1, red_in.results[0][k][None],
                red_in.results[1][k].reshape(pool_mix_w.shape), ps, convw_parts[k][None], cb, gp, gc,
                red_out.results[0][k][None], n2, red_1.results[0][k][None], red_2.results[0][k][None], fg]

    return (loss[0, 0], grad_x[None], *ordered(0), *ordered(1), *ordered(2), *ordered(3))
```

```python
import jax
import jax.numpy as jnp
from jax import lax
from jax.experimental import pallas as pl
from jax.experimental.pallas import tpu as pltpu
from jax.experimental.pallas import tpu_sc as plsc

F32 = jnp.float32
BF16 = jnp.bfloat16
MESH = pl.DeviceIdType.MESH
ANY = pl.BlockSpec(memory_space=pl.ANY)

NORM_EPS = 1e-6
POOL_WINDOWS = (2, 4, 8, 16)
CONV_HEAD_DIM = 128
N_MOD = 6
N_CHIPS = 4
N_DEV = 8

ADAM_LR = 0.001
ADAM_B1 = 0.9
ADAM_B2 = 0.999
ADAM_EPS = 1e-08
ADAM_WD = 0.01
ADAM_STEP = 10
ADAM_BIAS1 = 1.0 - ADAM_B1 ** ADAM_STEP
ADAM_BIAS2 = 1.0 - ADAM_B2 ** ADAM_STEP

V7X_VMEM_BYTES = 64 * 1024 * 1024
VMEM_LIMIT_BYTES = V7X_VMEM_BYTES - 8 * 1024 * 1024

NN = (((1,), (0,)), ((), ()))
NT = (((1,), (1,)), ((), ()))
TN = (((0,), (0,)), ((), ()))


def _tile(n, pref):
    t = min(n, pref)
    while n % t:
        t //= 2
    return t


STREAM_BLOCK_BYTES = 2 * 1024 * 1024


def _stream_block(rows, cols, itemsize):
    tc = cols if 8 * cols * itemsize <= STREAM_BLOCK_BYTES else _tile(cols, 2048)
    fit = max(8, STREAM_BLOCK_BYTES // (tc * itemsize))
    return _tile(rows, 1 << (fit.bit_length() - 1)), tc


def _params(*sem):
    return pltpu.CompilerParams(dimension_semantics=sem, vmem_limit_bytes=VMEM_LIMIT_BYTES)


def _position():
    return lax.axis_index("x"), lax.axis_index("y"), lax.axis_index("c")


def _flip(ix, iy, ic, mask):
    return (1 - ix if mask & 4 else ix, 1 - iy if mask & 2 else iy, 1 - ic if mask & 1 else ic)


def _allgather8(name, blk):
    rows, cols = blk.shape

    def body(x_ref, out_ref, send_sems, recv_sems, local_sem):
        ix, iy, ic = _position()
        me = 4 * ix + 2 * iy + ic
        mine = pltpu.make_async_copy(x_ref, out_ref.at[me], local_sem)
        mine.start()
        sends = []
        for mask in range(1, N_DEV):
            cp = pltpu.make_async_remote_copy(
                src_ref=x_ref, dst_ref=out_ref.at[me],
                send_sem=send_sems.at[mask - 1], recv_sem=recv_sems.at[mask - 1],
                device_id=_flip(ix, iy, ic, mask), device_id_type=MESH)
            cp.start()
            sends.append(cp)
        for mask in range(1, N_DEV):
            px, py, pc = _flip(ix, iy, ic, mask)
            pltpu.make_async_remote_copy(
                src_ref=x_ref, dst_ref=out_ref.at[4 * px + 2 * py + pc],
                send_sem=send_sems.at[mask - 1], recv_sem=recv_sems.at[mask - 1],
                device_id=(px, py, pc), device_id_type=MESH).wait_recv()
        for cp in sends:
            cp.wait_send()
        mine.wait()

    return pl.pallas_call(
        body, name=name,
        out_shape=jax.ShapeDtypeStruct((N_DEV, rows, cols), F32),
        in_specs=[pl.BlockSpec(memory_space=pltpu.VMEM)],
        out_specs=pl.BlockSpec(memory_space=pltpu.VMEM),
        scratch_shapes=[pltpu.SemaphoreType.DMA((N_DEV - 1,)), pltpu.SemaphoreType.DMA((N_DEV - 1,)),
                        pltpu.SemaphoreType.DMA],
    )(blk)


def _gather_flat(name, vec):
    n = vec.shape[0]
    npad = -(-n // 1024) * 1024
    blk = jnp.pad(vec, (0, npad - n)).reshape(8, npad // 8)
    return _allgather8(name, blk).reshape(N_DEV, npad)[:, :n]


def _chip_relations(ix, iy):
    return [(1 - ix, iy), (ix, 1 - iy), (1 - ix, 1 - iy)]


def _gather_weights(name, shards, collective_id):
    n = len(shards)

    def body(*refs):
        src, out = refs[:n], refs[n:2 * n]
        send_sems, recv_sems = refs[2 * n:]
        ix, iy, ic = _position()
        chip, chip_x, chip_y, chip_d = 2 * ix + iy, 2 * (1 - ix) + iy, 2 * ix + 1 - iy, 2 * (1 - ix) + 1 - iy
        beside_x, beside_y, sibling = (1 - ix, iy, ic), (ix, 1 - iy, ic), (ix, iy, 1 - ic)

        _handshake([beside_x, beside_y, sibling])

        def rows(a, core, quarter=None):
            half = shards[a].shape[0] // 2
            if quarter is None:
                return pl.ds(core * half, half)
            return pl.ds(core * half + quarter * (half // 2), half // 2)

        def copy(a, k, src_ref, dst_ref, to):
            return pltpu.make_async_remote_copy(
                src_ref=src_ref, dst_ref=dst_ref, send_sem=send_sems.at[8 * a + k], recv_sem=recv_sems.at[8 * a + k],
                device_id=to, device_id_type=MESH)

        def relay(a, k, piece, to):
            return copy(a, k, out[a].at[piece], out[a].at[piece], to)

        started = []

        def start(cp):
            cp.start()
            started.append(cp)

        for a in range(n):
            mine = src[a].at[rows(a, ic)]
            start(copy(a, 0, mine, out[a].at[chip, rows(a, ic)], beside_x))
            start(copy(a, 1, mine, out[a].at[chip, rows(a, ic)], beside_y))
        for a in range(n):
            relay(a, 0, (chip_x, rows(a, ic)), beside_x).wait_recv()
            start(relay(a, 3, (chip_x, rows(a, ic, 1)), beside_y))
            start(relay(a, 4, (chip_x, rows(a, ic)), sibling))
            relay(a, 1, (chip_y, rows(a, ic)), beside_y).wait_recv()
            start(relay(a, 2, (chip_y, rows(a, ic, 0)), beside_x))
            start(relay(a, 5, (chip_y, rows(a, ic)), sibling))
        for a in range(n):
            relay(a, 2, (chip_d, rows(a, ic, 0)), beside_x).wait_recv()
            start(relay(a, 6, (chip_d, rows(a, ic, 0)), sibling))
            relay(a, 3, (chip_d, rows(a, ic, 1)), beside_y).wait_recv()
            start(relay(a, 7, (chip_d, rows(a, ic, 1)), sibling))
        for a in range(n):
            relay(a, 4, (chip_x, rows(a, 1 - ic)), sibling).wait_recv()
            relay(a, 5, (chip_y, rows(a, 1 - ic)), sibling).wait_recv()
            relay(a, 6, (chip_d, rows(a, 1 - ic, 0)), sibling).wait_recv()
            relay(a, 7, (chip_d, rows(a, 1 - ic, 1)), sibling).wait_recv()
        for cp in started:
            cp.wait_send()

    out_type = [jax.ShapeDtypeStruct((N_CHIPS,) + s.shape, s.dtype) for s in shards]
    return _sequencer_call(name, body, shards, out_type, [8 * n, 8 * n], collective_id)


def _place_own(name, gathered, shard, chip):
    rows, cols = shard.shape
    tr, tc = _stream_block(rows, cols, 2)

    def body(chip_ref, own_ref, gathered_ref, o_ref):
        o_ref[...] = own_ref[...]

    return pl.pallas_call(
        body, name=name,
        grid_spec=pltpu.PrefetchScalarGridSpec(
            num_scalar_prefetch=1, grid=(rows // tr, cols // tc),
            in_specs=[pl.BlockSpec((tr, tc), lambda i, j, chip_ref: (i, j)), ANY],
            out_specs=pl.BlockSpec((None, tr, tc), lambda i, j, chip_ref: (chip_ref[0], i, j))),
        out_shape=jax.ShapeDtypeStruct(gathered.shape, gathered.dtype),
        input_output_aliases={2: 0},
        compiler_params=_params("parallel", "parallel"),
    )(chip, shard, gathered)


def _sequencer_call(name, body, operands, out_type, sem_counts, collective_id):
    return pl.kernel(
        body, name=name, out_type=out_type,
        mesh=plsc.ScalarSubcoreMesh(axis_name="sequencer", num_cores=1),
        scratch_types=[pltpu.SemaphoreType.DMA((n,)) for n in sem_counts],
        compiler_params=pltpu.CompilerParams(collective_id=collective_id),
    )(*operands)


def _handshake(peers):
    barrier = pltpu.get_barrier_semaphore()
    for peer in peers:
        pl.semaphore_signal(barrier, inc=1, device_id=peer, device_id_type=MESH)
    pl.semaphore_wait(barrier, len(peers))


def _exchange_halves(name, grads, collective_id):
    n = len(grads)

    def body(*refs):
        src, out = refs[:n], refs[n:2 * n]
        send_sems, recv_sems = refs[2 * n:]
        ix, iy, ic = _position()
        sibling = (ix, iy, 1 - ic)
        _handshake([sibling])
        copies = []
        for a in range(n):
            half = grads[a].shape[1] // 2
            cp = pltpu.make_async_remote_copy(
                src_ref=src[a].at[pl.ds(0, N_CHIPS), pl.ds((1 - ic) * half, half)], dst_ref=out[a],
                send_sem=send_sems.at[a], recv_sem=recv_sems.at[a],
                device_id=sibling, device_id_type=MESH)
            cp.start()
            copies.append(cp)
        for cp in copies:
            cp.wait()

    out_type = [jax.ShapeDtypeStruct((N_CHIPS, g.shape[1] // 2, g.shape[2]), g.dtype) for g in grads]
    return _sequencer_call(name, body, grads, out_type, [n, n], collective_id)


def _scatter_partials(name, parts, collective_id):
    n = len(parts)

    def body(*refs):
        src, out = refs[:n], refs[n:2 * n]
        send_sems, recv_sems = refs[2 * n:]
        ix, iy, ic = _position()
        rels = _chip_relations(ix, iy)
        _handshake([(px, py, ic) for px, py in rels])
        copies = []
        for a in range(n):
            for r, (px, py) in enumerate(rels):
                cp = pltpu.make_async_remote_copy(
                    src_ref=src[a].at[2 * px + py], dst_ref=out[a].at[r],
                    send_sem=send_sems.at[3 * a + r], recv_sem=recv_sems.at[3 * a + r],
                    device_id=(px, py, ic), device_id_type=MESH)
                cp.start()
                copies.append(cp)
        for cp in copies:
            cp.wait()

    out_type = [jax.ShapeDtypeStruct((3,) + p.shape[1:], p.dtype) for p in parts]
    return _sequencer_call(name, body, parts, out_type, [3 * n, 3 * n], collective_id)


def _swap_reduced(name, reduced, collective_id):
    n = len(reduced)

    def body(*refs):
        src, out = refs[:n], refs[n:2 * n]
        send_sems, recv_sems = refs[2 * n:]
        ix, iy, ic = _position()
        sibling = (ix, iy, 1 - ic)
        _handshake([sibling])
        copies = []
        for a in range(n):
            cp = pltpu.make_async_remote_copy(
                src_ref=src[a], dst_ref=out[a], send_sem=send_sems.at[a], recv_sem=recv_sems.at[a],
                device_id=sibling, device_id_type=MESH)
            cp.start()
            copies.append(cp)
        for cp in copies:
            cp.wait()

    out_type = [jax.ShapeDtypeStruct(r.shape, r.dtype) for r in reduced]
    return _sequencer_call(name, body, reduced, out_type, [n, n], collective_id)


def _add_half(name, grad, recv, core, after=()):
    _, rows, cols = grad.shape
    half = rows // 2
    tr, tc = _stream_block(half, cols, 2)
    nbr = half // tr

    def body(core_ref, g_ref, r_ref, *rest):
        rest[-1][...] = g_ref[...] + r_ref[...]

    return pl.pallas_call(
        body, name=name,
        grid_spec=pltpu.PrefetchScalarGridSpec(
            num_scalar_prefetch=1, grid=(N_CHIPS, nbr, cols // tc),
            in_specs=[pl.BlockSpec((None, tr, tc), lambda s, i, j, core_ref: (s, core_ref[0] * nbr + i, j)),
                      pl.BlockSpec((None, tr, tc), lambda s, i, j, core_ref: (s, i, j))] + [ANY] * len(after),
            out_specs=pl.BlockSpec((None, tr, tc), lambda s, i, j, core_ref: (s, i, j))),
        out_shape=jax.ShapeDtypeStruct((N_CHIPS, half, cols), BF16),
        compiler_params=_params("parallel", "parallel", "parallel"),
    )(core, grad, recv, *after)


def _adamw_update(gv, wv, mv, vv):
    mn = ADAM_B1 * mv + (1.0 - ADAM_B1) * gv
    vn = ADAM_B2 * vv + (1.0 - ADAM_B2) * (gv * gv)
    delta = -ADAM_LR * ((mn / ADAM_BIAS1) / (jnp.sqrt(vn / ADAM_BIAS2) + ADAM_EPS) + ADAM_WD * wv)
    return delta, mn, vn


def _reduce_update(name, part, recv, w, m, v, chip_core, after=()):
    _, half, cols = part.shape
    tr, tc = _stream_block(half, cols, 4)
    nbr = half // tr

    def body(pos_ref, p_ref, r_ref, w_ref, m_ref, v_ref, *rest):
        red_ref, g_ref, d_ref, mo_ref, vo_ref = rest[-5:]
        gv = p_ref[...].astype(F32)
        for r in range(3):
            gv = gv + r_ref[r].astype(F32)
        red_ref[...] = gv
        g_ref[...] = gv
        d_ref[...], mo_ref[...], vo_ref[...] = _adamw_update(gv, w_ref[...], m_ref[...], v_ref[...])

    mine = pl.BlockSpec((tr, tc), lambda i, j, pos_ref: (pos_ref[1] * nbr + i, j))
    shape = jax.ShapeDtypeStruct((2 * half, cols), F32)
    reduced, *updated = pl.pallas_call(
        body, name=name,
        grid_spec=pltpu.PrefetchScalarGridSpec(
            num_scalar_prefetch=1, grid=(nbr, cols // tc),
            in_specs=[pl.BlockSpec((None, tr, tc), lambda i, j, pos_ref: (pos_ref[0], i, j)),
                      pl.BlockSpec((3, tr, tc), lambda i, j, pos_ref: (0, i, j)), mine, mine, mine]
            + [ANY] * len(after),
            out_specs=[pl.BlockSpec((tr, tc), lambda i, j, pos_ref: (i, j)), mine, mine, mine, mine]),
        out_shape=[jax.ShapeDtypeStruct((half, cols), F32), shape, shape, shape, shape],
        compiler_params=_params("parallel", "parallel"),
    )(chip_core, part, recv, w, m, v, *after)
    return reduced, updated


def _adamw_half(name, g_half, w, m, v, which, done, after=()):
    half, cols = g_half.shape
    tr, tc = _stream_block(half, cols, 4)
    nbr = half // tr

    def body(which_ref, g_ref, w_ref, m_ref, v_ref, *rest):
        go_ref, d_ref, mo_ref, vo_ref = rest[-4:]
        gv = g_ref[...]
        go_ref[...] = gv
        d_ref[...], mo_ref[...], vo_ref[...] = _adamw_update(gv, w_ref[...], m_ref[...], v_ref[...])

    mine = pl.BlockSpec((tr, tc), lambda i, j, which_ref: (which_ref[0] * nbr + i, j))
    shape = jax.ShapeDtypeStruct((2 * half, cols), F32)
    return pl.pallas_call(
        body, name=name,
        grid_spec=pltpu.PrefetchScalarGridSpec(
            num_scalar_prefetch=1, grid=(nbr, cols // tc),
            in_specs=([pl.BlockSpec((tr, tc), lambda i, j, which_ref: (i, j)), mine, mine, mine]
                      + [ANY] * (len(done) + len(after))),
            out_specs=[mine] * 4),
        out_shape=[shape] * 4,
        input_output_aliases={5 + k: k for k in range(len(done))},
        compiler_params=_params("parallel", "parallel"),
    )(which, g_half, w, m, v, *done, *after)


def _adamw(name, g, w, m, v):
    rows, cols = g.shape
    tr, tc = _stream_block(rows, cols, 4)

    def body(g_ref, w_ref, m_ref, v_ref, d_ref, mo_ref, vo_ref):
        d_ref[...], mo_ref[...], vo_ref[...] = _adamw_update(g_ref[...], w_ref[...], m_ref[...], v_ref[...])

    spec = pl.BlockSpec((tr, tc), lambda i, j: (i, j))
    shape = jax.ShapeDtypeStruct((rows, cols), F32)
    return pl.pallas_call(
        body, name=name, grid=(rows // tr, cols // tc),
        in_specs=[spec] * 4, out_specs=[spec] * 3, out_shape=[shape] * 3,
        compiler_params=_params("parallel", "parallel"),
    )(g, w, m, v)


def _reduce_small(gathered, d_model):
    n = gathered.shape[1]
    loss_at = (N_MOD + 3) * d_model

    def body(g_ref, s_ref, loss_ref):
        acc = g_ref[0:1, :]
        for d in range(1, N_DEV):
            acc = acc + g_ref[d:d + 1, :]
        s_ref[...] = acc
        lanes = acc[:, loss_at:loss_at + d_model]
        loss_ref[...] = jnp.broadcast_to((0.5 / d_model) * jnp.sum(lanes, axis=1, keepdims=True), loss_ref.shape)

    return pl.pallas_call(
        body, name="reduce_small",
        out_shape=[jax.ShapeDtypeStruct((1, n), F32), jax.ShapeDtypeStruct((1, 128), F32)],
        compiler_params=pltpu.CompilerParams(vmem_limit_bytes=VMEM_LIMIT_BYTES),
    )(gathered)


def _ada_forward(c_all, w_ada, b_cols):
    d_model, width = w_ada.shape
    tn = _tile(width, 512)

    def body(c_ref, w_ref, b_ref, o_ref):
        cv = c_ref[...]
        act = cv * jax.nn.sigmoid(cv)
        o_ref[...] = lax.dot_general(act, w_ref[...], NN, precision=lax.Precision.HIGHEST,
                                     preferred_element_type=F32) + b_ref[...]

    return pl.pallas_call(
        body, name="ada_forward", grid=(width // tn,),
        in_specs=[pl.BlockSpec((N_DEV, d_model), lambda j: (0, 0)),
                  pl.BlockSpec((d_model, tn), lambda j: (0, j)),
                  pl.BlockSpec((1, tn), lambda j: (0, j))],
        out_specs=pl.BlockSpec((N_DEV, tn), lambda j: (0, j)),
        out_shape=jax.ShapeDtypeStruct((N_DEV, width), F32),
        compiler_params=_params("parallel"),
    )(c_all, w_ada, b_cols)


def _ada_backward(c_all_t, dmod_cols, w, m, v):
    d_model, width = w.shape
    tr, tc = _stream_block(d_model, width, 4)

    def body(c_ref, dm_ref, w_ref, m_ref, v_ref, g_ref, d_ref, mo_ref, vo_ref):
        cv = c_ref[...]
        act = cv * jax.nn.sigmoid(cv)
        gv = lax.dot_general(act, dm_ref[...], NN, precision=lax.Precision.HIGHEST, preferred_element_type=F32)
        g_ref[...] = gv
        d_ref[...], mo_ref[...], vo_ref[...] = _adamw_update(gv, w_ref[...], m_ref[...], v_ref[...])

    spec = pl.BlockSpec((tr, tc), lambda i, j: (i, j))
    shape = jax.ShapeDtypeStruct((d_model, width), F32)
    return pl.pallas_call(
        body, name="ada_backward", grid=(d_model // tr, width // tc),
        in_specs=[pl.BlockSpec((tr, N_DEV), lambda i, j: (i, 0)),
                  pl.BlockSpec((N_DEV, tc), lambda i, j: (0, j)), spec, spec, spec],
        out_specs=[spec] * 4, out_shape=[shape] * 4,
        compiler_params=_params("parallel", "parallel"),
    )(c_all_t, dmod_cols, w, m, v)


def _matmul(name, a, b, extras, *, grid, tiles, dims, a_spec, b_spec, extra_specs, out_shape, out_specs,
            epilogue, prologue=None, after=()):
    tm, tn, _ = tiles
    gm, gn, gk = grid
    n_extra, n_out = len(extras), len(out_shape)
    first_out = 2 + n_extra + len(after)

    def product(a_ref, b_ref):
        av = a_ref[...]
        if prologue is not None:
            av = prologue(av)
        return lax.dot_general(av, b_ref[...], dims, preferred_element_type=F32)

    def body_single(*refs):
        epilogue(product(refs[0], refs[1]), refs[2:2 + n_extra], refs[first_out:first_out + n_out])

    def body(*refs):
        a_ref, b_ref = refs[0], refs[1]
        extra_refs = refs[2:2 + n_extra]
        out_refs = refs[first_out:first_out + n_out]
        acc_ref = refs[-1]
        k = pl.program_id(2)

        @pl.when(k == 0)
        def _():
            acc_ref[...] = jnp.zeros_like(acc_ref)

        acc_ref[...] += product(a_ref, b_ref)

        @pl.when(k == gk - 1)
        def _():
            epilogue(acc_ref[...], extra_refs, out_refs)

    single = gk == 1
    return pl.pallas_call(
        body_single if single else body, name=name, grid=(gm, gn, gk),
        in_specs=[a_spec, b_spec, *extra_specs] + [ANY] * len(after), out_specs=out_specs, out_shape=out_shape,
        scratch_shapes=[] if single else [pltpu.VMEM((tm, tn), F32)],
        compiler_params=_params("parallel", "parallel", "arbitrary"),
    )(a, b, *extras, *after)


def _store(dtype):
    def epilogue(acc, extra_refs, out_refs):
        out_refs[0][...] = acc.astype(dtype)
    return epilogue


def _residual_epilogue(acc, extra_refs, out_refs):
    res_ref, gate_ref = extra_refs
    out_refs[0][...] = res_ref[...] + gate_ref[...] * acc
    out_refs[1][...] = acc.astype(BF16)


def _square(av):
    af = av.astype(F32)
    return (af * af).astype(BF16)


MM_TILE_M = 1024
MM_TILE_N = 1024
MM_WHOLE_K = 4096
MM_TILE_K = 4096


def _mm_tiles(m, n, k, tn_pref=MM_TILE_N, k_block=None):
    tk = k if k <= MM_WHOLE_K else MM_TILE_K
    if k_block is not None:
        tk = min(tk, k_block)
    return _tile(m, MM_TILE_M), _tile(n, tn_pref), _tile(k, tk)


def _column_sharded_matmul(name, h, w_own, w_slabs, chip, finish, slab_major):
    seq, d_model = h.shape
    cols = w_own.shape[1]
    tm, tn, tk = _mm_tiles(seq, cols, d_model)
    assert tk == d_model
    nbj = cols // tn

    def body_own(chip_ref, a_ref, b_ref, o_ref):
        o_ref[...] = finish(jnp.dot(a_ref[...], b_ref[...], preferred_element_type=F32)).astype(BF16)

    def body_rest(chip_ref, a_ref, b_ref, own_ref, o_ref):
        o_ref[...] = finish(jnp.dot(a_ref[...], b_ref[...], preferred_element_type=F32)).astype(BF16)

    def slab(j, chip_ref):
        return (chip_ref[0] + 1 + j // nbj) % N_CHIPS

    if slab_major:
        out_shape = jax.ShapeDtypeStruct((N_CHIPS, seq, cols), BF16)
        out_block = (None, tm, tn)
        own_out = lambda i, j, chip_ref: (chip_ref[0], i, j)
        rest_out = lambda i, j, chip_ref: (slab(j, chip_ref), i, j % nbj)
    else:
        out_shape = jax.ShapeDtypeStruct((seq, N_CHIPS * cols), BF16)
        out_block = (tm, tn)
        own_out = lambda i, j, chip_ref: (i, chip_ref[0] * nbj + j)
        rest_out = lambda i, j, chip_ref: (i, slab(j, chip_ref) * nbj + j % nbj)
    rows = pl.BlockSpec((tm, tk), lambda i, j, chip_ref: (i, 0))
    own = pl.pallas_call(
        body_own, name=f"{name}_own",
        grid_spec=pltpu.PrefetchScalarGridSpec(
            num_scalar_prefetch=1, grid=(seq // tm, nbj),
            in_specs=[rows, pl.BlockSpec((tk, tn), lambda i, j, chip_ref: (0, j))],
            out_specs=pl.BlockSpec(out_block, own_out)),
        out_shape=out_shape, compiler_params=_params("parallel", "parallel"),
    )(chip, h, w_own)
    return pl.pallas_call(
        body_rest, name=f"{name}_rest",
        grid_spec=pltpu.PrefetchScalarGridSpec(
            num_scalar_prefetch=1, grid=(seq // tm, (N_CHIPS - 1) * nbj),
            in_specs=[rows, pl.BlockSpec((None, tk, tn), lambda i, j, chip_ref: (slab(j, chip_ref), 0, j % nbj)), ANY],
            out_specs=pl.BlockSpec(out_block, rest_out)),
        out_shape=out_shape, input_output_aliases={3: 0},
        compiler_params=_params("parallel", "parallel"),
    )(chip, h, w_slabs, own)


def _in_projection(h, w_own, w_slabs, chip):
    return _column_sharded_matmul("in_projection", h, w_own, w_slabs, chip, lambda acc: acc, True)


def _residual_projection(name, a, w, res, gate, prologue=None):
    seq, kdim = a.shape
    d_model = w.shape[1]
    if kdim <= MM_WHOLE_K:
        tm, tn, tk = _mm_tiles(seq, d_model, kdim, tn_pref=MM_TILE_N // 2)
    else:
        tm, tn, tk = _mm_tiles(seq, d_model, kdim, k_block=MM_TILE_K // 2)
    tile = pl.BlockSpec((tm, tn), lambda i, j, k: (i, j))
    return _matmul(
        name, a, w, (res, gate), grid=(seq // tm, d_model // tn, kdim // tk), tiles=(tm, tn, tk), dims=NN,
        a_spec=pl.BlockSpec((tm, tk), lambda i, j, k: (i, k)),
        b_spec=pl.BlockSpec((tk, tn), lambda i, j, k: (k, j)),
        extra_specs=(tile, pl.BlockSpec((1, tn), lambda i, j, k: (0, j))),
        out_shape=[jax.ShapeDtypeStruct((seq, d_model), F32), jax.ShapeDtypeStruct((seq, d_model), BF16)],
        out_specs=[tile, tile],
        epilogue=_residual_epilogue, prologue=prologue)


def _mlp_in(h, w_own, w_slabs, chip):
    return _column_sharded_matmul("mlp_in", h, w_own, w_slabs, chip, lambda acc: jnp.maximum(acc, 0.0), False)


def _grad_hidden(dmlp, w2, act, after=()):
    seq, d_model = dmlp.shape
    ff = w2.shape[0]
    tm, tn, tk = _mm_tiles(seq, ff, d_model)

    def epilogue(acc, extra_refs, out_refs):
        out_refs[0][...] = (acc * (2.0 * extra_refs[0][...].astype(F32))).astype(BF16)

    tile = pl.BlockSpec((tm, tn), lambda i, j, k: (i, j))
    return _matmul(
        "grad_hidden", dmlp, w2, (act,), grid=(seq // tm, ff // tn, d_model // tk), tiles=(tm, tn, tk), dims=NT,
        a_spec=pl.BlockSpec((tm, tk), lambda i, j, k: (i, k)),
        b_spec=pl.BlockSpec((tn, tk), lambda i, j, k: (j, k)),
        extra_specs=(tile,),
        out_shape=[jax.ShapeDtypeStruct((seq, ff), BF16)], out_specs=[tile],
        epilogue=epilogue, after=after)[0]


def _weight_grad(name, a, b, prologue=None):
    seq, m = a.shape
    n = b.shape[1]
    tm, tn, tk = _mm_tiles(m, n, seq)
    return _matmul(
        name, a, b, (), grid=(m // tm, n // tn, seq // tk), tiles=(tm, tn, tk), dims=TN,
        a_spec=pl.BlockSpec((tk, tm), lambda i, j, k: (k, i)),
        b_spec=pl.BlockSpec((tk, tn), lambda i, j, k: (k, j)),
        extra_specs=(),
        out_shape=[jax.ShapeDtypeStruct((m, n), BF16)],
        out_specs=[pl.BlockSpec((tm, tn), lambda i, j, k: (i, j))],
        epilogue=_store(BF16), prologue=prologue)[0]


def _weight_grad_slabs(name, a, b, slab_cols, after=()):
    seq, m = a.shape
    cols = b.shape[2] if slab_cols is None else slab_cols
    tm, tn, tk = _mm_tiles(m, cols, seq)
    nbj = cols // tn
    if slab_cols is None:
        b_spec = pl.BlockSpec((None, tk, tn), lambda i, j, k: (j // nbj, k, j % nbj))
    else:
        b_spec = pl.BlockSpec((tk, tn), lambda i, j, k: (k, j))
    return _matmul(
        name, a, b, (), grid=(m // tm, N_CHIPS * nbj, seq // tk), tiles=(tm, tn, tk), dims=TN,
        a_spec=pl.BlockSpec((tk, tm), lambda i, j, k: (k, i)),
        b_spec=b_spec, extra_specs=(),
        out_shape=[jax.ShapeDtypeStruct((N_CHIPS, m, cols), BF16)],
        out_specs=[pl.BlockSpec((None, tm, tn), lambda i, j, k: (j // nbj, i, j % nbj))],
        epilogue=_store(BF16), after=after)[0]


def _grad_input_slabs(name, dy, w_slabs, after=()):
    _, d_model, cols = w_slabs.shape
    seq = dy.shape[1] if dy.ndim == 3 else dy.shape[0]
    tm, tn, tk = _mm_tiles(seq, d_model, N_CHIPS * cols, k_block=cols)
    nbk = cols // tk
    if dy.ndim == 3:
        a_spec = pl.BlockSpec((None, tm, tk), lambda i, j, k: (k // nbk, i, k % nbk))
    else:
        a_spec = pl.BlockSpec((tm, tk), lambda i, j, k: (i, k))
    return _matmul(
        name, dy, w_slabs, (), grid=(seq // tm, d_model // tn, N_CHIPS * nbk), tiles=(tm, tn, tk), dims=NT,
        a_spec=a_spec,
        b_spec=pl.BlockSpec((None, tn, tk), lambda i, j, k: (k // nbk, j, k % nbk)),
        extra_specs=(),
        out_shape=[jax.ShapeDtypeStruct((seq, d_model), F32)],
        out_specs=[pl.BlockSpec((tm, tn), lambda i, j, k: (i, j))],
        epilogue=_store(F32), after=after)[0]


def _grad_input(name, dy, w):
    seq, n = dy.shape
    kdim = w.shape[0]
    tm, tn, tk = _mm_tiles(seq, kdim, n)
    return _matmul(
        name, dy, w, (), grid=(seq // tm, kdim // tn, n // tk), tiles=(tm, tn, tk), dims=NT,
        a_spec=pl.BlockSpec((tm, tk), lambda i, j, k: (i, k)),
        b_spec=pl.BlockSpec((tn, tk), lambda i, j, k: (j, k)),
        extra_specs=(),
        out_shape=[jax.ShapeDtypeStruct((seq, kdim), F32)],
        out_specs=[pl.BlockSpec((tm, tn), lambda i, j, k: (i, j))],
        epilogue=_store(F32))[0]


ROW_TILE = 128
ROW_TILE_FORWARD = 256


def _norm_modulate(name, xin, g, scale, shift):
    seq, d_model = xin.shape
    tr = _tile(seq, ROW_TILE_FORWARD)

    def body(x_ref, g_ref, sc_ref, sh_ref, h_ref):
        xv = x_ref[...]
        r = lax.rsqrt(jnp.mean(xv * xv, axis=-1, keepdims=True) + NORM_EPS)
        h_ref[...] = (((xv * r) * g_ref[...]) * (1.0 + sc_ref[...]) + sh_ref[...]).astype(BF16)

    row = pl.BlockSpec((tr, d_model), lambda i: (i, 0))
    vec = pl.BlockSpec((1, d_model), lambda i: (0, 0))
    return pl.pallas_call(
        body, name=name, grid=(seq // tr,),
        in_specs=[row, vec, vec, vec], out_specs=row,
        out_shape=jax.ShapeDtypeStruct((seq, d_model), BF16),
        compiler_params=_params("parallel"),
    )(xin, g, scale, shift)


def _loss_head(x2, target, final_g, mlp, gate2):
    seq, d_model = x2.shape
    tr = _tile(seq, ROW_TILE)

    def body(x_ref, t_ref, fg_ref, mlp_ref, gate_ref, dx_ref, dmlp_ref, gfg_ref, dgate_ref, sq_ref):
        @pl.when(pl.program_id(0) == 0)
        def _():
            gfg_ref[...] = jnp.zeros_like(gfg_ref)
            dgate_ref[...] = jnp.zeros_like(dgate_ref)
            sq_ref[...] = jnp.zeros_like(sq_ref)

        xv = x_ref[...]
        fg = fg_ref[...]
        r = lax.rsqrt(jnp.mean(xv * xv, axis=-1, keepdims=True) + NORM_EPS)
        n = xv * r
        err = n * fg - t_ref[...]
        sq_ref[...] += jnp.sum(err * err, axis=0, keepdims=True)
        dy = err * (1.0 / d_model)
        gfg_ref[...] += jnp.sum(dy * n, axis=0, keepdims=True)
        dn = dy * fg
        dx = r * (dn - n * jnp.mean(dn * n, axis=-1, keepdims=True))
        dx_ref[...] = dx
        dgate_ref[...] += jnp.sum(dx * mlp_ref[...].astype(F32), axis=0, keepdims=True)
        dmlp_ref[...] = (dx * gate_ref[...]).astype(BF16)

    row = pl.BlockSpec((tr, d_model), lambda i: (i, 0))
    vec = pl.BlockSpec((1, d_model), lambda i: (0, 0))
    vshape = jax.ShapeDtypeStruct((1, d_model), F32)
    return pl.pallas_call(
        body, name="loss_head", grid=(seq // tr,),
        in_specs=[row, row, vec, row, vec], out_specs=[row, row, vec, vec, vec],
        out_shape=[jax.ShapeDtypeStruct((seq, d_model), F32), jax.ShapeDtypeStruct((seq, d_model), BF16),
                   vshape, vshape, vshape],
        compiler_params=_params("arbitrary"),
    )(x2, target, final_g, mlp, gate2)


def _norm_modulate_backward(name, dh, xin, g, scale, dres, branch=None, gate=None, after=()):
    seq, d_model = xin.shape
    tr = _tile(seq, ROW_TILE)
    with_branch = branch is not None
    n_in = (7 if with_branch else 5) + len(after)

    def body(*refs):
        dh_ref, x_ref, g_ref, sc_ref, dres_ref = refs[:5]
        outs = refs[n_in:]
        dx_ref, dsc_ref, dsh_ref, dg_ref = outs[:4]

        @pl.when(pl.program_id(0) == 0)
        def _():
            for ref in outs[1:5] if with_branch else outs[1:4]:
                ref[...] = jnp.zeros_like(ref)

        xv = x_ref[...]
        gv = g_ref[...]
        dhv = dh_ref[...]
        r = lax.rsqrt(jnp.mean(xv * xv, axis=-1, keepdims=True) + NORM_EPS)
        xn = xv * r
        dsh_ref[...] += jnp.sum(dhv, axis=0, keepdims=True)
        dsc_ref[...] += jnp.sum(dhv * (xn * gv), axis=0, keepdims=True)
        t = dhv * (1.0 + sc_ref[...])
        dg_ref[...] += jnp.sum(t * xn, axis=0, keepdims=True)
        dxn = t * gv
        dx = dres_ref[...] + r * (dxn - xn * jnp.mean(dxn * xn, axis=-1, keepdims=True))
        dx_ref[...] = dx
        if with_branch:
            br_ref, gate_ref = refs[5:7]
            dgate_ref, dbr_ref = outs[4:6]
            dgate_ref[...] += jnp.sum(dx * br_ref[...].astype(F32), axis=0, keepdims=True)
            dbr_ref[...] = (dx * gate_ref[...]).astype(BF16)

    row = pl.BlockSpec((tr, d_model), lambda i: (i, 0))
    vec = pl.BlockSpec((1, d_model), lambda i: (0, 0))
    vshape = jax.ShapeDtypeStruct((1, d_model), F32)
    in_specs = [row, row, vec, vec, row]
    out_specs = [row, vec, vec, vec]
    out_shape = [jax.ShapeDtypeStruct((seq, d_model), F32), vshape, vshape, vshape]
    args = [dh, xin, g, scale, dres]
    if with_branch:
        in_specs += [row, vec]
        out_specs += [vec, row]
        out_shape += [vshape, jax.ShapeDtypeStruct((seq, d_model), BF16)]
        args += [branch, gate]
    in_specs += [ANY] * len(after)
    args += list(after)
    return pl.pallas_call(
        body, name=name, grid=(seq // tr,),
        in_specs=in_specs, out_specs=out_specs, out_shape=out_shape,
        compiler_params=_params("arbitrary"),
    )(*args)


def _shifted(v, k, t):
    seq = v.shape[0]
    if k == 0:
        return v
    moved = pltpu.roll(v, (-k) % seq, 0)
    return jnp.where((t + k >= 0) & (t + k < seq), moved, 0.0)


def _window_sum(v, offsets, t):
    acc = None
    for k in offsets:
        term = _shifted(v, k, t)
        acc = term if acc is None else acc + term
    return acc


def _window_count(seq, half):
    t = lax.broadcasted_iota(jnp.int32, (seq, 1), 0)
    return (jnp.minimum(t + half, seq) - jnp.maximum(t - half, 0)).astype(F32)


def _pool_forward(proj, group_dim):
    _, seq, cols = proj.shape
    tl = _tile(group_dim, 256)
    nbl = group_dim // tl
    n_groups = cols // group_dim

    def body(v_ref, o_ref):
        g = pl.program_id(0)
        for gi, window in enumerate(POOL_WINDOWS[:n_groups]):
            @pl.when(g == gi)
            def _(window=window):
                half = window // 2
                v = v_ref[...].astype(F32)
                t = lax.broadcasted_iota(jnp.int32, v.shape, 0)
                total = _window_sum(v, range(-half, half), t)
                o_ref[...] = (total / _window_count(seq, half) - v).astype(BF16)

    return pl.pallas_call(
        body, name="pool_forward", grid=(n_groups, nbl),
        in_specs=[pl.BlockSpec((None, seq, tl), lambda g, j: (0, 0, g * nbl + j))],
        out_specs=pl.BlockSpec((seq, tl), lambda g, j: (0, g * nbl + j)),
        out_shape=jax.ShapeDtypeStruct((seq, cols), BF16),
        compiler_params=_params("parallel", "parallel"),
    )(proj)


def _group_matrix(w_ref):
    return jnp.concatenate([w_ref[r] for r in range(N_CHIPS)], axis=0)


def _pool_mix_forward(pooled, w_pm, pool_scale, gnorm_g, d_model):
    seq, cols = pooled.shape
    _, n_groups, shard_rows, group_dim = w_pm.shape
    tm = _tile(seq, 512)

    def body(p_ref, w_ref, ps_ref, g_ref, o_ref):
        a = jnp.dot(p_ref[...], _group_matrix(w_ref), preferred_element_type=F32) * ps_ref[...]
        ra = lax.rsqrt(jnp.mean(a * a, axis=-1, keepdims=True) + NORM_EPS)
        o_ref[...] = ((a * ra) * g_ref[...]).astype(BF16)

    tile = pl.BlockSpec((tm, group_dim), lambda g, i: (i, g))
    vec = pl.BlockSpec((1, group_dim), lambda g, i: (0, g))
    return pl.pallas_call(
        body, name="pool_mix_forward", grid=(n_groups, seq // tm),
        in_specs=[tile, pl.BlockSpec((N_CHIPS, None, shard_rows, group_dim), lambda g, i: (0, g, 0, 0)), vec, vec],
        out_specs=tile,
        out_shape=jax.ShapeDtypeStruct((seq, d_model), BF16),
        compiler_params=_params("parallel", "parallel"),
    )(pooled, w_pm, pool_scale, gnorm_g)


def _conv_parts(b_ref, c_ref, u_ref, w_ref, bias_ref):
    bv = b_ref[...].astype(F32)
    cu = c_ref[...].astype(F32) * u_ref[...].astype(F32)
    t = lax.broadcasted_iota(jnp.int32, cu.shape, 0)
    prev, nxt = _shifted(cu, -1, t), _shifted(cu, 1, t)
    w = w_ref[...]
    conv = w[0:1] * prev + w[1:2] * cu + w[2:3] * nxt + bias_ref[...]
    return bv, cu, prev, nxt, conv, w, t


def _conv_forward(proj, conv_w, conv_b, gnorm_g, mixed):
    _, seq, cols = proj.shape
    tl = CONV_HEAD_DIM
    first = cols // tl

    def body(b_ref, c_ref, u_ref, w_ref, bias_ref, g_ref, mixed_ref, o_ref):
        bv, _, _, _, conv, _, _ = _conv_parts(b_ref, c_ref, u_ref, w_ref, bias_ref)
        bo = bv * conv
        rb = lax.rsqrt(jnp.mean(bo * bo, axis=-1, keepdims=True) + NORM_EPS)
        o_ref[...] = ((bo * rb) * g_ref[...]).astype(BF16)

    def slab(s):
        return pl.BlockSpec((None, seq, tl), lambda j, s=s: (s, 0, j))

    vec = pl.BlockSpec((1, tl), lambda j: (0, j))
    return pl.pallas_call(
        body, name="conv_forward", grid=(cols // tl,),
        in_specs=[slab(1), slab(2), slab(3), pl.BlockSpec((3, tl), lambda j: (0, j)), vec, vec, ANY],
        out_specs=pl.BlockSpec((seq, tl), lambda j: (0, first + j)),
        out_shape=jax.ShapeDtypeStruct(mixed.shape, mixed.dtype),
        input_output_aliases={6: 0},
        compiler_params=_params("parallel"),
    )(proj, proj, proj, conv_w, conv_b, gnorm_g, mixed)


def _pool_mix_backward(dmixed, pooled, w_pm, pool_scale, gnorm_g, after=()):
    seq, cols = pooled.shape
    _, n_groups, shard_rows, group_dim = w_pm.shape
    tm = _tile(seq, 512)

    def body(dm_ref, p_ref, w_ref, ps_ref, g_ref, *rest):
        dp_ref, dpm_ref, gg_ref, gps_ref = rest[-4:]

        @pl.when(pl.program_id(1) == 0)
        def _():
            gg_ref[...] = jnp.zeros_like(gg_ref)
            gps_ref[...] = jnp.zeros_like(gps_ref)

        w = _group_matrix(w_ref)
        ps = ps_ref[...]
        a_pre = jnp.dot(p_ref[...], w, preferred_element_type=F32)
        a = a_pre * ps
        ra = lax.rsqrt(jnp.mean(a * a, axis=-1, keepdims=True) + NORM_EPS)
        an = a * ra
        dm = dm_ref[...]
        gg_ref[...] += jnp.sum(dm * an, axis=0, keepdims=True)
        dan = dm * g_ref[...]
        da = ra * (dan - an * jnp.mean(dan * an, axis=-1, keepdims=True))
        gps_ref[...] += jnp.sum(da * a_pre, axis=0, keepdims=True)
        dpm = (da * ps).astype(BF16)
        dpm_ref[...] = dpm
        dp_ref[...] = lax.dot_general(dpm, w, NT, preferred_element_type=F32)

    tile = pl.BlockSpec((tm, group_dim), lambda g, i: (i, g))
    vec = pl.BlockSpec((1, group_dim), lambda g, i: (0, g))
    vshape = jax.ShapeDtypeStruct((1, cols), F32)
    return pl.pallas_call(
        body, name="pool_mix_backward", grid=(n_groups, seq // tm),
        in_specs=[tile, tile, pl.BlockSpec((N_CHIPS, None, shard_rows, group_dim), lambda g, i: (0, g, 0, 0)),
                  vec, vec] + [ANY] * len(after),
        out_specs=[tile, tile, vec, vec],
        out_shape=[jax.ShapeDtypeStruct((seq, cols), F32), jax.ShapeDtypeStruct((seq, cols), BF16), vshape, vshape],
        compiler_params=_params("parallel", "arbitrary"),
    )(dmixed, pooled, w_pm, pool_scale, gnorm_g, *after)


def _pool_mix_weight_grad(pooled, dpm, n_groups):
    seq, cols = pooled.shape
    group_dim = cols // n_groups
    shard_rows = group_dim // N_CHIPS
    tk = _tile(seq, 1024)
    gk = seq // tk

    def body(p_ref, d_ref, o_ref, acc_ref):
        k = pl.program_id(1)

        @pl.when(k == 0)
        def _():
            acc_ref[...] = jnp.zeros_like(acc_ref)

        acc_ref[...] += lax.dot_general(p_ref[...], d_ref[...], TN, preferred_element_type=F32)

        @pl.when(k == gk - 1)
        def _():
            for r in range(N_CHIPS):
                o_ref[r] = acc_ref[r * shard_rows:(r + 1) * shard_rows, :].astype(BF16)

    tile = pl.BlockSpec((tk, group_dim), lambda g, k: (k, g))
    return pl.pallas_call(
        body, name="pool_mix_weight_grad", grid=(n_groups, gk),
        in_specs=[tile, tile],
        out_specs=pl.BlockSpec((N_CHIPS, None, shard_rows, group_dim), lambda g, k: (0, g, 0, 0)),
        out_shape=jax.ShapeDtypeStruct((N_CHIPS, n_groups, shard_rows, group_dim), BF16),
        scratch_shapes=[pltpu.VMEM((group_dim, group_dim), F32)],
        compiler_params=_params("parallel", "arbitrary"),
    )(pooled, dpm)


def _mixers_backward(dpooled, dmixed, proj, conv_w, conv_b, gnorm_g, group_dim):
    _, seq, cols = proj.shape
    tl = CONV_HEAD_DIM
    first = cols // tl
    per_group = group_dim // tl
    n_groups = cols // group_dim

    def body(dp_ref, dm_ref, b_ref, c_ref, u_ref, w_ref, bias_ref, g_ref, o_ref, gg_ref, gb_ref, gw_ref):
        j = pl.program_id(0)
        for gi, window in enumerate(POOL_WINDOWS[:n_groups]):
            @pl.when(j // per_group == gi)
            def _(window=window):
                half = window // 2
                dp = dp_ref[...]
                t = lax.broadcasted_iota(jnp.int32, dp.shape, 0)
                dq = dp / _window_count(seq, half)
                o_ref[0] = (_window_sum(dq, range(-half + 1, half + 1), t) - dp).astype(BF16)

        bv, cu, prev, nxt, conv, w, t = _conv_parts(b_ref, c_ref, u_ref, w_ref, bias_ref)
        bo = bv * conv
        rb = lax.rsqrt(jnp.mean(bo * bo, axis=-1, keepdims=True) + NORM_EPS)
        bn = bo * rb
        dm = dm_ref[...]
        gg_ref[...] = jnp.sum(dm * bn, axis=0, keepdims=True)
        dbn = dm * g_ref[...]
        dbo = rb * (dbn - bn * jnp.mean(dbn * bn, axis=-1, keepdims=True))
        o_ref[1] = (dbo * conv).astype(BF16)
        dconv = dbo * bv
        gb_ref[...] = jnp.sum(dconv, axis=0, keepdims=True)
        gw_ref[0:1, :] = jnp.sum(dconv * prev, axis=0, keepdims=True)
        gw_ref[1:2, :] = jnp.sum(dconv * cu, axis=0, keepdims=True)
        gw_ref[2:3, :] = jnp.sum(dconv * nxt, axis=0, keepdims=True)
        dcu = w[0:1] * _shifted(dconv, 1, t) + w[1:2] * dconv + w[2:3] * _shifted(dconv, -1, t)
        o_ref[2] = (dcu * u_ref[...].astype(F32)).astype(BF16)
        o_ref[3] = (dcu * c_ref[...].astype(F32)).astype(BF16)

    def slab(s):
        return pl.BlockSpec((None, seq, tl), lambda j, s=s: (s, 0, j))

    vec = pl.BlockSpec((1, tl), lambda j: (0, j))
    rows3 = pl.BlockSpec((3, tl), lambda j: (0, j))
    vshape = jax.ShapeDtypeStruct((1, cols), F32)
    return pl.pallas_call(
        body, name="mixers_backward", grid=(cols // tl,),
        in_specs=[pl.BlockSpec((seq, tl), lambda j: (0, j)), pl.BlockSpec((seq, tl), lambda j: (0, first + j)),
                  slab(1), slab(2), slab(3), rows3, vec, vec],
        out_specs=[pl.BlockSpec((N_CHIPS, seq, tl), lambda j: (0, 0, j)), vec, vec, rows3],
        out_shape=[jax.ShapeDtypeStruct((N_CHIPS, seq, cols), BF16), vshape, vshape,
                   jax.ShapeDtypeStruct((3, cols), F32)],
        compiler_params=_params("parallel"),
    )(dpooled, dmixed, proj, proj, proj, conv_w, conv_b, gnorm_g)


class _GradReduction:
    def __init__(self, tag, grads, states, pair_id, scatter_id, position):
        self.tag, self.grads, self.states = tag, grads, states
        self.pair_id, self.scatter_id = pair_id, scatter_id
        self.chip_core, self.core, self.other_core = position

    def exchange(self):
        self.received = _exchange_halves(f"exchange_{self.tag}", self.grads, self.pair_id)

    def combine(self, after=()):
        self.parts = [_add_half(f"add_half_{self.tag}_{a}", g, r, self.core, after)
                      for a, (g, r) in enumerate(zip(self.grads, self.received))]

    def scatter(self):
        self.landed = _scatter_partials(f"scatter_{self.tag}", self.parts, self.scatter_id)

    def reduce(self, after=()):
        done = [_reduce_update(f"reduce_update_{self.tag}_{a}", p, l, *state, self.chip_core, after)
                for a, (p, l, state) in enumerate(zip(self.parts, self.landed, self.states))]
        self.reduced = [reduced for reduced, _ in done]
        self.mine = [updated for _, updated in done]

    def swap(self):
        self.swapped = _swap_reduced(f"swap_{self.tag}", self.reduced, self.pair_id)

    def update_other(self, after=()):
        self.results = [_adamw_half(f"adamw_other_{self.tag}_{a}", g, *state, self.other_core, done, after)
                        for a, (g, state, done) in enumerate(zip(self.swapped, self.states, self.mine))]

    def token(self, stage):
        first = getattr(self, stage)[0]
        return first if not isinstance(first, (list, tuple)) else first[0]


def kernel(x, c, w_ada, b_ada, norm1_g, w_in, pool_mix_w, pool_scale, conv_w, conv_b, gnorm_pool_g, gnorm_conv_g, w_out, norm2_g, w_mlp_in, w_mlp_out, final_g, loss_target, m_w_ada, m_b_ada, m_norm1_g, m_w_in, m_pool_mix_w, m_pool_scale, m_conv_w, m_conv_b, m_gnorm_pool_g, m_gnorm_conv_g, m_w_out, m_norm2_g, m_w_mlp_in, m_w_mlp_out, m_final_g, v_w_ada, v_b_ada, v_norm1_g, v_w_in, v_pool_mix_w, v_pool_scale, v_conv_w, v_conv_b, v_gnorm_pool_g, v_gnorm_conv_g, v_w_out, v_norm2_g, v_w_mlp_in, v_w_mlp_out, v_final_g):
    seq, d_model = x.shape[1], x.shape[2]
    cols = w_in.shape[2]
    n_groups, group_dim = pool_mix_w.shape[1], pool_mix_w.shape[3]
    shard_rows = pool_mix_w.shape[2]
    ff_cols = w_mlp_in.shape[2]
    ada_cols = w_ada.shape[2]
    conv_shard = conv_w.shape[2]
    assert pool_scale.shape[1] == cols and conv_b.shape[1] == cols and n_groups * group_dim == cols
    assert cols % CONV_HEAD_DIM == 0 and group_dim % CONV_HEAD_DIM == 0 and shard_rows * N_CHIPS == group_dim

    ix, iy, ic = _position()
    chip = 2 * ix + iy
    me = 4 * ix + 2 * iy + ic
    position = tuple(jnp.stack(v).astype(jnp.int32) for v in ([chip, ic], [ic], [1 - ic]))

    xs, target = x[0], loss_target[0]
    final_row = final_g.reshape(1, d_model)

    small = _gather_flat("gather_cond", jnp.concatenate([c[0], conv_w[0].reshape(-1)]))
    c_all = small[:, :d_model]
    conv_w_full = jnp.concatenate(
        [small[2 * j, d_model:].reshape(3, conv_shard) for j in range(N_CHIPS)], axis=1)
    b_cols = lax.dynamic_slice_in_dim(b_ada, chip * ada_cols, ada_cols, axis=1)
    mod_part = _ada_forward(c_all, w_ada[0], b_cols)
    mod_all = _gather_flat("gather_mod", mod_part.reshape(-1)).reshape(N_DEV, N_DEV, ada_cols)
    mod = jnp.concatenate(
        [lax.dynamic_slice_in_dim(mod_all[2 * j], me, 1, axis=0) for j in range(N_CHIPS)], axis=1)
    shift1, scale1, gate1, shift2, scale2, gate2 = [mod[:, i * d_model:(i + 1) * d_model] for i in range(N_MOD)]

    shards = [w_in[0].astype(BF16), pool_mix_w[0].reshape(n_groups * shard_rows, group_dim).astype(BF16),
              w_out[0].astype(BF16), w_mlp_in[0].astype(BF16), w_mlp_out[0].astype(BF16)]
    wg_in, wg_pm = _gather_weights("gather_w_in", shards[0:2], 1)
    (wg_out,) = _gather_weights("gather_w_out", shards[2:3], 2)
    (wg_1,) = _gather_weights("gather_w_mlp_in", shards[3:4], 3)
    (wg_2,) = _gather_weights("gather_w_mlp_out", shards[4:5], 4)
    wg_in, wg_pm, wg_out, wg_1, wg_2 = [
        _place_own(f"place_own_{a}", g, s, position[0])
        for a, (g, s) in enumerate(zip([wg_in, wg_pm, wg_out, wg_1, wg_2], shards))]
    wg_pm = wg_pm.reshape(N_CHIPS, n_groups, shard_rows, group_dim)
    wg_out = wg_out.reshape(d_model, d_model)
    wg_2 = wg_2.reshape(N_CHIPS * ff_cols, d_model)

    h1 = _norm_modulate("norm_modulate_1", xs, norm1_g, scale1, shift1)
    proj = _in_projection(h1, shards[0], wg_in, position[0])
    pooled = _pool_forward(proj, group_dim)
    mixed = _pool_mix_forward(pooled, wg_pm, pool_scale, gnorm_pool_g, d_model)
    mixed = _conv_forward(proj, conv_w_full, conv_b, gnorm_conv_g, mixed)
    x1, attn = _residual_projection("out_projection", mixed, wg_out, xs, gate1)
    h2 = _norm_modulate("norm_modulate_2", x1, norm2_g, scale2, shift2)
    act = _mlp_in(h2, shards[3], wg_1, position[0])
    x2, mlp = _residual_projection("mlp_out", act, wg_2, x1, gate2, prologue=_square)

    dx2, dmlp, g_final, dgate2, sq_err = _loss_head(x2, target, final_row, mlp, gate2)
    gw_2 = _weight_grad("grad_w_mlp_out", act, dmlp, prologue=_square)
    red_2 = _GradReduction("w_mlp_out", [gw_2.reshape(N_CHIPS, ff_cols, d_model)],
                           [(w_mlp_out[0], m_w_mlp_out[0], v_w_mlp_out[0])], 8, 12, position)
    red_2.exchange()
    dhid = _grad_hidden(dmlp, wg_2, act, after=[gw_2])
    red_2.combine(after=[dhid])
    red_2.scatter()
    gw_1 = _weight_grad_slabs("grad_w_mlp_in", h2, dhid, ff_cols, after=[red_2.token("parts")])
    red_1 = _GradReduction("w_mlp_in", [gw_1], [(w_mlp_in[0], m_w_mlp_in[0], v_w_mlp_in[0])], 7, 11, position)
    red_1.exchange()
    dh2 = _grad_input_slabs("grad_h2", dhid, wg_1, after=[gw_1])
    red_2.reduce(after=[dh2])
    red_2.swap()
    red_1.combine(after=[red_2.token("mine")])
    red_1.scatter()
    dx1, dscale2, dshift2, g_norm2, dgate1, dattn = _norm_modulate_backward(
        "norm_modulate_backward_2", dh2, x1, norm2_g, scale2, dx2, attn, gate1, after=[red_1.token("parts")])
    gw_out = _weight_grad("grad_w_out", mixed, dattn)
    red_out = _GradReduction("w_out", [gw_out.reshape(N_CHIPS, d_model // N_CHIPS, d_model)],
                             [(w_out[0], m_w_out[0], v_w_out[0])], 6, 10, position)
    red_out.exchange()
    dmixed = _grad_input("grad_mixed", dattn, wg_out)
    dpooled, dpm, g_gpool, g_pscale = _pool_mix_backward(dmixed, pooled, wg_pm, pool_scale, gnorm_pool_g)
    gw_pm = _pool_mix_weight_grad(pooled, dpm, n_groups)
    dproj, g_gconv, g_convb, g_convw = _mixers_backward(
        dpooled, dmixed, proj, conv_w_full, conv_b, gnorm_conv_g, group_dim)
    gw_in = _weight_grad_slabs("grad_w_in", h1, dproj, None)
    pm2d = (n_groups * shard_rows, group_dim)
    red_in = _GradReduction(
        "w_in", [gw_in, gw_pm.reshape((N_CHIPS,) + pm2d)],
        [(w_in[0], m_w_in[0], v_w_in[0]),
         (pool_mix_w[0].reshape(pm2d), m_pool_mix_w[0].reshape(pm2d), v_pool_mix_w[0].reshape(pm2d))],
        5, 9, position)
    red_in.exchange()
    red_1.reduce(after=[gw_in])
    red_1.swap()
    red_out.combine(after=[red_1.token("mine")])
    red_out.scatter()
    red_in.combine(after=[red_out.token("parts")])
    red_in.scatter()
    dh1 = _grad_input_slabs("grad_h1", dproj, wg_in, after=[red_in.token("parts")])
    red_2.update_other(after=[dh1])
    grad_x, dscale1, dshift1, g_norm1 = _norm_modulate_backward(
        "norm_modulate_backward_1", dh1, xs, norm1_g, scale1, dx1, after=[red_2.token("results")])
    red_out.reduce(after=[grad_x])
    red_out.swap()
    red_1.update_other(after=[red_out.token("mine")])

    mine = jnp.concatenate(
        [dshift1, dscale1, dgate1, dshift2, dscale2, dgate2, g_norm1, g_norm2, g_final, sq_err,
         g_pscale, g_convb, g_gpool, g_gconv, g_convw.reshape(1, 3 * cols)], axis=1)
    gathered = _gather_flat("gather_small", mine.reshape(-1))
    sums, loss = _reduce_small(gathered, d_model)
    n_rep = (N_MOD + 3) * d_model
    g_rep = jnp.concatenate([sums[:, :n_rep], sums[:, n_rep + d_model:n_rep + d_model + 4 * cols]], axis=1)
    n_small = g_rep.shape[1]

    def pack(b, n1, n2, fg, ps, cb, gp, gc):
        return jnp.concatenate([b, n1, n2, fg.reshape(1, d_model), ps, cb, gp, gc], axis=1).reshape(8, n_small // 8)

    d_rep, m_rep, v_rep = _adamw(
        "adamw_small", g_rep.reshape(8, n_small // 8),
        pack(b_ada, norm1_g, norm2_g, final_g, pool_scale, conv_b, gnorm_pool_g, gnorm_conv_g),
        pack(m_b_ada, m_norm1_g, m_norm2_g, m_final_g, m_pool_scale, m_conv_b, m_gnorm_pool_g, m_gnorm_conv_g),
        pack(v_b_ada, v_norm1_g, v_norm2_g, v_final_g, v_pool_scale, v_conv_b, v_gnorm_pool_g, v_gnorm_conv_g))

    def unpack(flat):
        flat = flat.reshape(1, n_small)
        sizes = [N_MOD * d_model, d_model, d_model, d_model, cols, cols, cols, cols]
        parts, at = [], 0
        for size in sizes:
            parts.append(flat[:, at:at + size])
            at += size
        parts[3] = parts[3].reshape(d_model)
        return parts

    g_convw_full = sums[:, n_rep + d_model + 4 * cols:].reshape(3, cols)
    g_convw_mine = lax.dynamic_slice_in_dim(g_convw_full, chip * conv_shard, conv_shard, axis=1)
    d_convw, m_convw, v_convw = _adamw("adamw_conv_w", g_convw_mine, conv_w[0], m_conv_w[0], v_conv_w[0])

    dmod_cols = lax.dynamic_slice_in_dim(gathered[:, :N_MOD * d_model], chip * ada_cols, ada_cols, axis=1)
    g_ada, d_ada, mn_ada, vn_ada = _ada_backward(c_all.T, dmod_cols, w_ada[0], m_w_ada[0], v_w_ada[0])

    red_in.reduce(after=[g_ada, red_1.token("results")])
    red_in.swap()
    red_out.update_other(after=[red_in.token("mine")])
    red_in.update_other(after=[red_out.token("results")])

    small_parts = [unpack(g_rep), unpack(d_rep), unpack(m_rep), unpack(v_rep)]
    ada_parts = [g_ada, d_ada, mn_ada, vn_ada]
    convw_parts = [g_convw_mine, d_convw, m_convw, v_convw]

    def ordered(k):
        b, n1, n2, fg, ps, cb, gp, gc = small_parts[k]
        return [ada_parts[k][None], b, n1, red_in.results[0][k][None],
                red_in.results[1][k].reshape(pool_mix_w.shape), ps, convw_parts[k][None], cb, gp, gc,
                red_out.results[0][k][None], n2, red_1.results[0][k][None], red_2.results[0][k][None], fg]

    return (loss[0, 0], grad_x[None], *ordered(0), *ordered(1), *ordered(2), *ordered(3))
```

```python
import jax
import jax.numpy as jnp
from jax import lax
from jax.experimental import pallas as pl
from jax.experimental.pallas import tpu as pltpu
from jax.experimental.pallas import tpu_sc as plsc

F32 = jnp.float32
BF16 = jnp.bfloat16
MESH = pl.DeviceIdType.MESH
ANY = pl.BlockSpec(memory_space=pl.ANY)

NORM_EPS = 1e-6
POOL_WINDOWS = (2, 4, 8, 16)
CONV_HEAD_DIM = 128
N_MOD = 6
N_CHIPS = 4
N_DEV = 8

ADAM_LR = 0.001
ADAM_B1 = 0.9
ADAM_B2 = 0.999
ADAM_EPS = 1e-08
ADAM_WD = 0.01
ADAM_STEP = 10
ADAM_BIAS1 = 1.0 - ADAM_B1 ** ADAM_STEP
ADAM_BIAS2 = 1.0 - ADAM_B2 ** ADAM_STEP

V7X_VMEM_BYTES = 64 * 1024 * 1024
VMEM_LIMIT_BYTES = V7X_VMEM_BYTES - 8 * 1024 * 1024

NN = (((1,), (0,)), ((), ()))
NT = (((1,), (1,)), ((), ()))
TN = (((0,), (0,)), ((), ()))


def _tile(n, pref):
    t = min(n, pref)
    while n % t:
        t //= 2
    return t


STREAM_BLOCK_BYTES = 2 * 1024 * 1024


def _stream_block(rows, cols, itemsize):
    tc = cols if 8 * cols * itemsize <= STREAM_BLOCK_BYTES else _tile(cols, 2048)
    fit = max(8, STREAM_BLOCK_BYTES // (tc * itemsize))
    return _tile(rows, 1 << (fit.bit_length() - 1)), tc


def _params(*sem):
    return pltpu.CompilerParams(dimension_semantics=sem, vmem_limit_bytes=VMEM_LIMIT_BYTES)


def _position():
    return lax.axis_index("x"), lax.axis_index("y"), lax.axis_index("c")


def _flip(ix, iy, ic, mask):
    return (1 - ix if mask & 4 else ix, 1 - iy if mask & 2 else iy, 1 - ic if mask & 1 else ic)


def _allgather8(name, blk):
    rows, cols = blk.shape

    def body(x_ref, out_ref, send_sems, recv_sems, local_sem):
        ix, iy, ic = _position()
        me = 4 * ix + 2 * iy + ic
        mine = pltpu.make_async_copy(x_ref, out_ref.at[me], local_sem)
        mine.start()
        sends = []
        for mask in range(1, N_DEV):
            cp = pltpu.make_async_remote_copy(
                src_ref=x_ref, dst_ref=out_ref.at[me],
                send_sem=send_sems.at[mask - 1], recv_sem=recv_sems.at[mask - 1],
                device_id=_flip(ix, iy, ic, mask), device_id_type=MESH)
            cp.start()
            sends.append(cp)
        for mask in range(1, N_DEV):
            px, py, pc = _flip(ix, iy, ic, mask)
            pltpu.make_async_remote_copy(
                src_ref=x_ref, dst_ref=out_ref.at[4 * px + 2 * py + pc],
                send_sem=send_sems.at[mask - 1], recv_sem=recv_sems.at[mask - 1],
                device_id=(px, py, pc), device_id_type=MESH).wait_recv()
        for cp in sends:
            cp.wait_send()
        mine.wait()

    return pl.pallas_call(
        body, name=name,
        out_shape=jax.ShapeDtypeStruct((N_DEV, rows, cols), F32),
        in_specs=[pl.BlockSpec(memory_space=pltpu.VMEM)],
        out_specs=pl.BlockSpec(memory_space=pltpu.VMEM),
        scratch_shapes=[pltpu.SemaphoreType.DMA((N_DEV - 1,)), pltpu.SemaphoreType.DMA((N_DEV - 1,)),
                        pltpu.SemaphoreType.DMA],
    )(blk)


def _gather_flat(name, vec):
    n = vec.shape[0]
    npad = -(-n // 1024) * 1024
    blk = jnp.pad(vec, (0, npad - n)).reshape(8, npad // 8)
    return _allgather8(name, blk).reshape(N_DEV, npad)[:, :n]


def _chip_relations(ix, iy):
    return [(1 - ix, iy), (ix, 1 - iy), (1 - ix, 1 - iy)]


def _gather_weights(name, shards, collective_id):
    n = len(shards)

    def body(*refs):
        src, out = refs[:n], refs[n:2 * n]
        send_sems, recv_sems = refs[2 * n:]
        ix, iy, ic = _position()
        chip, chip_x, chip_y, chip_d = 2 * ix + iy, 2 * (1 - ix) + iy, 2 * ix + 1 - iy, 2 * (1 - ix) + 1 - iy
        beside_x, beside_y, sibling = (1 - ix, iy, ic), (ix, 1 - iy, ic), (ix, iy, 1 - ic)

        _handshake([beside_x, beside_y, sibling])

        def rows(a, core, quarter=None):
            half = shards[a].shape[0] // 2
            if quarter is None:
                return pl.ds(core * half, half)
            return pl.ds(core * half + quarter * (half // 2), half // 2)

        def copy(a, k, src_ref, dst_ref, to):
            return pltpu.make_async_remote_copy(
                src_ref=src_ref, dst_ref=dst_ref, send_sem=send_sems.at[8 * a + k], recv_sem=recv_sems.at[8 * a + k],
                device_id=to, device_id_type=MESH)

        def relay(a, k, piece, to):
            return copy(a, k, out[a].at[piece], out[a].at[piece], to)

        started = []

        def start(cp):
            cp.start()
            started.append(cp)

        for a in range(n):
            mine = src[a].at[rows(a, ic)]
            start(copy(a, 0, mine, out[a].at[chip, rows(a, ic)], beside_x))
            start(copy(a, 1, mine, out[a].at[chip, rows(a, ic)], beside_y))
        for a in range(n):
            relay(a, 0, (chip_x, rows(a, ic)), beside_x).wait_recv()
            start(relay(a, 3, (chip_x, rows(a, ic, 1)), beside_y))
            start(relay(a, 4, (chip_x, rows(a, ic)), sibling))
            relay(a, 1, (chip_y, rows(a, ic)), beside_y).wait_recv()
            start(relay(a, 2, (chip_y, rows(a, ic, 0)), beside_x))
            start(relay(a, 5, (chip_y, rows(a, ic)), sibling))
        for a in range(n):
            relay(a, 2, (chip_d, rows(a, ic, 0)), beside_x).wait_recv()
            start(relay(a, 6, (chip_d, rows(a, ic, 0)), sibling))
            relay(a, 3, (chip_d, rows(a, ic, 1)), beside_y).wait_recv()
            start(relay(a, 7, (chip_d, rows(a, ic, 1)), sibling))
        for a in range(n):
            relay(a, 4, (chip_x, rows(a, 1 - ic)), sibling).wait_recv()
            relay(a, 5, (chip_y, rows(a, 1 - ic)), sibling).wait_recv()
            relay(a, 6, (chip_d, rows(a, 1 - ic, 0)), sibling).wait_recv()
            relay(a, 7, (chip_d, rows(a, 1 - ic, 1)), sibling).wait_recv()
        for cp in started:
            cp.wait_send()

    out_type = [jax.ShapeDtypeStruct((N_CHIPS,) + s.shape, s.dtype) for s in shards]
    return _sequencer_call(name, body, shards, out_type, [8 * n, 8 * n], collective_id)


def _place_own(name, gathered, shard, chip):
    rows, cols = shard.shape
    tr, tc = _stream_block(rows, cols, 2)

    def body(chip_ref, own_ref, gathered_ref, o_ref):
        o_ref[...] = own_ref[...]

    return pl.pallas_call(
        body, name=name,
        grid_spec=pltpu.PrefetchScalarGridSpec(
            num_scalar_prefetch=1, grid=(rows // tr, cols // tc),
            in_specs=[pl.BlockSpec((tr, tc), lambda i, j, chip_ref: (i, j)), ANY],
            out_specs=pl.BlockSpec((None, tr, tc), lambda i, j, chip_ref: (chip_ref[0], i, j))),
        out_shape=jax.ShapeDtypeStruct(gathered.shape, gathered.dtype),
        input_output_aliases={2: 0},
        compiler_params=_params("parallel", "parallel"),
    )(chip, shard, gathered)


def _sequencer_call(name, body, operands, out_type, sem_counts, collective_id):
    return pl.kernel(
        body, name=name, out_type=out_type,
        mesh=plsc.ScalarSubcoreMesh(axis_name="sequencer", num_cores=1),
        scratch_types=[pltpu.SemaphoreType.DMA((n,)) for n in sem_counts],
        compiler_params=pltpu.CompilerParams(collective_id=collective_id),
    )(*operands)


def _handshake(peers):
    barrier = pltpu.get_barrier_semaphore()
    for peer in peers:
        pl.semaphore_signal(barrier, inc=1, device_id=peer, device_id_type=MESH)
    pl.semaphore_wait(barrier, len(peers))


def _exchange_halves(name, grads, collective_id):
    n = len(grads)

    def body(*refs):
        src, out = refs[:n], refs[n:2 * n]
        send_sems, recv_sems = refs[2 * n:]
        ix, iy, ic = _position()
        sibling = (ix, iy, 1 - ic)
        _handshake([sibling])
        copies = []
        for a in range(n):
            half = grads[a].shape[1] // 2
            cp = pltpu.make_async_remote_copy(
                src_ref=src[a].at[pl.ds(0, N_CHIPS), pl.ds((1 - ic) * half, half)], dst_ref=out[a],
                send_sem=send_sems.at[a], recv_sem=recv_sems.at[a],
                device_id=sibling, device_id_type=MESH)
            cp.start()
            copies.append(cp)
        for cp in copies:
            cp.wait()

    out_type = [jax.ShapeDtypeStruct((N_CHIPS, g.shape[1] // 2, g.shape[2]), g.dtype) for g in grads]
    return _sequencer_call(name, body, grads, out_type, [n, n], collective_id)


def _scatter_partials(name, parts, collective_id):
    n = len(parts)

    def body(*refs):
        src, out = refs[:n], refs[n:2 * n]
        send_sems, recv_sems = refs[2 * n:]
        ix, iy, ic = _position()
        rels = _chip_relations(ix, iy)
        _handshake([(px, py, ic) for px, py in rels])
        copies = []
        for a in range(n):
            for r, (px, py) in enumerate(rels):
                cp = pltpu.make_async_remote_copy(
                    src_ref=src[a].at[2 * px + py], dst_ref=out[a].at[r],
                    send_sem=send_sems.at[3 * a + r], recv_sem=recv_sems.at[3 * a + r],
                    device_id=(px, py, ic), device_id_type=MESH)
                cp.start()
                copies.append(cp)
        for cp in copies:
            cp.wait()

    out_type = [jax.ShapeDtypeStruct((3,) + p.shape[1:], p.dtype) for p in parts]
    return _sequencer_call(name, body, parts, out_type, [3 * n, 3 * n], collective_id)


def _swap_reduced(name, reduced, collective_id):
    n = len(reduced)

    def body(*refs):
        src, out = refs[:n], refs[n:2 * n]
        send_sems, recv_sems = refs[2 * n:]
        ix, iy, ic = _position()
        sibling = (ix, iy, 1 - ic)
        _handshake([sibling])
        copies = []
        for a in range(n):
            cp = pltpu.make_async_remote_copy(
                src_ref=src[a], dst_ref=out[a], send_sem=send_sems.at[a], recv_sem=recv_sems.at[a],
                device_id=sibling, device_id_type=MESH)
            cp.start()
            copies.append(cp)
        for cp in copies:
            cp.wait()

    out_type = [jax.ShapeDtypeStruct(r.shape, r.dtype) for r in reduced]
    return _sequencer_call(name, body, reduced, out_type, [n, n], collective_id)


def _add_half(name, grad, recv, core, after=()):
    _, rows, cols = grad.shape
    half = rows // 2
    tr, tc = _stream_block(half, cols, 2)
    nbr = half // tr

    def body(core_ref, g_ref, r_ref, *rest):
        rest[-1][...] = g_ref[...] + r_ref[...]

    return pl.pallas_call(
        body, name=name,
        grid_spec=pltpu.PrefetchScalarGridSpec(
            num_scalar_prefetch=1, grid=(N_CHIPS, nbr, cols // tc),
            in_specs=[pl.BlockSpec((None, tr, tc), lambda s, i, j, core_ref: (s, core_ref[0] * nbr + i, j)),
                      pl.BlockSpec((None, tr, tc), lambda s, i, j, core_ref: (s, i, j))] + [ANY] * len(after),
            out_specs=pl.BlockSpec((None, tr, tc), lambda s, i, j, core_ref: (s, i, j))),
        out_shape=jax.ShapeDtypeStruct((N_CHIPS, half, cols), BF16),
        compiler_params=_params("parallel", "parallel", "parallel"),
    )(core, grad, recv, *after)


def _adamw_update(gv, wv, mv, vv):
    mn = ADAM_B1 * mv + (1.0 - ADAM_B1) * gv
    vn = ADAM_B2 * vv + (1.0 - ADAM_B2) * (gv * gv)
    delta = -ADAM_LR * ((mn / ADAM_BIAS1) / (jnp.sqrt(vn / ADAM_BIAS2) + ADAM_EPS) + ADAM_WD * wv)
    return delta, mn, vn


def _reduce_update(name, part, recv, w, m, v, chip_core, after=()):
    _, half, cols = part.shape
    tr, tc = _stream_block(half, cols, 4)
    nbr = half // tr

    def body(pos_ref, p_ref, r_ref, w_ref, m_ref, v_ref, *rest):
        red_ref, g_ref, d_ref, mo_ref, vo_ref = rest[-5:]
        gv = p_ref[...].astype(F32)
        for r in range(3):
            gv = gv + r_ref[r].astype(F32)
        red_ref[...] = gv
        g_ref[...] = gv
        d_ref[...], mo_ref[...], vo_ref[...] = _adamw_update(gv, w_ref[...], m_ref[...], v_ref[...])

    mine = pl.BlockSpec((tr, tc), lambda i, j, pos_ref: (pos_ref[1] * nbr + i, j))
    shape = jax.ShapeDtypeStruct((2 * half, cols), F32)
    reduced, *updated = pl.pallas_call(
        body, name=name,
        grid_spec=pltpu.PrefetchScalarGridSpec(
            num_scalar_prefetch=1, grid=(nbr, cols // tc),
            in_specs=[pl.BlockSpec((None, tr, tc), lambda i, j, pos_ref: (pos_ref[0], i, j)),
                      pl.BlockSpec((3, tr, tc), lambda i, j, pos_ref: (0, i, j)), mine, mine, mine]
            + [ANY] * len(after),
            out_specs=[pl.BlockSpec((tr, tc), lambda i, j, pos_ref: (i, j)), mine, mine, mine, mine]),
        out_shape=[jax.ShapeDtypeStruct((half, cols), F32), shape, shape, shape, shape],
        compiler_params=_params("parallel", "parallel"),
    )(chip_core, part, recv, w, m, v, *after)
    return reduced, updated


def _adamw_half(name, g_half, w, m, v, which, done, after=()):
    half, cols = g_half.shape
    tr, tc = _stream_block(half, cols, 4)
    nbr = half // tr

    def body(which_ref, g_ref, w_ref, m_ref, v_ref, *rest):
        go_ref, d_ref, mo_ref, vo_ref = rest[-4:]
        gv = g_ref[...]
        go_ref[...] = gv
        d_ref[...], mo_ref[...], vo_ref[...] = _adamw_update(gv, w_ref[...], m_ref[...], v_ref[...])

    mine = pl.BlockSpec((tr, tc), lambda i, j, which_ref: (which_ref[0] * nbr + i, j))
    shape = jax.ShapeDtypeStruct((2 * half, cols), F32)
    return pl.pallas_call(
        body, name=name,
        grid_spec=pltpu.PrefetchScalarGridSpec(
            num_scalar_prefetch=1, grid=(nbr, cols // tc),
            in_specs=([pl.BlockSpec((tr, tc), lambda i, j, which_ref: (i, j)), mine, mine, mine]
                      + [ANY] * (len(done) + len(after))),
            out_specs=[mine] * 4),
        out_shape=[shape] * 4,
        input_output_aliases={5 + k: k for k in range(len(done))},
        compiler_params=_params("parallel", "parallel"),
    )(which, g_half, w, m, v, *done, *after)


def _adamw(name, g, w, m, v):
    rows, cols = g.shape
    tr, tc = _stream_block(rows, cols, 4)

    def body(g_ref, w_ref, m_ref, v_ref, d_ref, mo_ref, vo_ref):
        d_ref[...], mo_ref[...], vo_ref[...] = _adamw_update(g_ref[...], w_ref[...], m_ref[...], v_ref[...])

    spec = pl.BlockSpec((tr, tc), lambda i, j: (i, j))
    shape = jax.ShapeDtypeStruct((rows, cols), F32)
    return pl.pallas_call(
        body, name=name, grid=(rows // tr, cols // tc),
        in_specs=[spec] * 4, out_specs=[spec] * 3, out_shape=[shape] * 3,
        compiler_params=_params("parallel", "parallel"),
    )(g, w, m, v)


def _reduce_small(gathered, d_model):
    n = gathered.shape[1]
    loss_at = (N_MOD + 3) * d_model

    def body(g_ref, s_ref, loss_ref):
        acc = g_ref[0:1, :]
        for d in range(1, N_DEV):
            acc = acc + g_ref[d:d + 1, :]
        s_ref[...] = acc
        lanes = acc[:, loss_at:loss_at + d_model]
        loss_ref[...] = jnp.broadcast_to((0.5 / d_model) * jnp.sum(lanes, axis=1, keepdims=True), loss_ref.shape)

    return pl.pallas_call(
        body, name="reduce_small",
        out_shape=[jax.ShapeDtypeStruct((1, n), F32), jax.ShapeDtypeStruct((1, 128), F32)],
        compiler_params=pltpu.CompilerParams(vmem_limit_bytes=VMEM_LIMIT_BYTES),
    )(gathered)


def _ada_forward(c_all, w_ada, b_cols):
    d_model, width = w_ada.shape
    tn = _tile(width, 512)

    def body(c_ref, w_ref, b_ref, o_ref):
        cv = c_ref[...]
        act = cv * jax.nn.sigmoid(cv)
        o_ref[...] = lax.dot_general(act, w_ref[...], NN, precision=lax.Precision.HIGHEST,
                                     preferred_element_type=F32) + b_ref[...]

    return pl.pallas_call(
        body, name="ada_forward", grid=(width // tn,),
        in_specs=[pl.BlockSpec((N_DEV, d_model), lambda j: (0, 0)),
                  pl.BlockSpec((d_model, tn), lambda j: (0, j)),
                  pl.BlockSpec((1, tn), lambda j: (0, j))],
        out_specs=pl.BlockSpec((N_DEV, tn), lambda j: (0, j)),
        out_shape=jax.ShapeDtypeStruct((N_DEV, width), F32),
        compiler_params=_params("parallel"),
    )(c_all, w_ada, b_cols)


def _ada_backward(c_all_t, dmod_cols, w, m, v):
    d_model, width = w.shape
    tr, tc = _stream_block(d_model, width, 4)

    def body(c_ref, dm_ref, w_ref, m_ref, v_ref, g_ref, d_ref, mo_ref, vo_ref):
        cv = c_ref[...]
        act = cv * jax.nn.sigmoid(cv)
        gv = lax.dot_general(act, dm_ref[...], NN, precision=lax.Precision.HIGHEST, preferred_element_type=F32)
        g_ref[...] = gv
        d_ref[...], mo_ref[...], vo_ref[...] = _adamw_update(gv, w_ref[...], m_ref[...], v_ref[...])

    spec = pl.BlockSpec((tr, tc), lambda i, j: (i, j))
    shape = jax.ShapeDtypeStruct((d_model, width), F32)
    return pl.pallas_call(
        body, name="ada_backward", grid=(d_model // tr, width // tc),
        in_specs=[pl.BlockSpec((tr, N_DEV), lambda i, j: (i, 0)),
                  pl.BlockSpec((N_DEV, tc), lambda i, j: (0, j)), spec, spec, spec],
        out_specs=[spec] * 4, out_shape=[shape] * 4,
        compiler_params=_params("parallel", "parallel"),
    )(c_all_t, dmod_cols, w, m, v)


def _matmul(name, a, b, extras, *, grid, tiles, dims, a_spec, b_spec, extra_specs, out_shape, out_specs,
            epilogue, prologue=None, after=()):
    tm, tn, _ = tiles
    gm, gn, gk = grid
    n_extra, n_out = len(extras), len(out_shape)
    first_out = 2 + n_extra + len(after)

    def product(a_ref, b_ref):
        av = a_ref[...]
        if prologue is not None:
            av = prologue(av)
        return lax.dot_general(av, b_ref[...], dims, preferred_element_type=F32)

    def body_single(*refs):
        epilogue(product(refs[0], refs[1]), refs[2:2 + n_extra], refs[first_out:first_out + n_out])

    def body(*refs):
        a_ref, b_ref = refs[0], refs[1]
        extra_refs = refs[2:2 + n_extra]
        out_refs = refs[first_out:first_out + n_out]
        acc_ref = refs[-1]
        k = pl.program_id(2)

        @pl.when(k == 0)
        def _():
            acc_ref[...] = jnp.zeros_like(acc_ref)

        acc_ref[...] += product(a_ref, b_ref)

        @pl.when(k == gk - 1)
        def _():
            epilogue(acc_ref[...], extra_refs, out_refs)

    single = gk == 1
    return pl.pallas_call(
        body_single if single else body, name=name, grid=(gm, gn, gk),
        in_specs=[a_spec, b_spec, *extra_specs] + [ANY] * len(after), out_specs=out_specs, out_shape=out_shape,
        scratch_shapes=[] if single else [pltpu.VMEM((tm, tn), F32)],
        compiler_params=_params("parallel", "parallel", "arbitrary"),
    )(a, b, *extras, *after)


def _store(dtype):
    def epilogue(acc, extra_refs, out_refs):
        out_refs[0][...] = acc.astype(dtype)
    return epilogue


def _residual_epilogue(acc, extra_refs, out_refs):
    res_ref, gate_ref = extra_refs
    out_refs[0][...] = res_ref[...] + gate_ref[...] * acc
    out_refs[1][...] = acc.astype(BF16)


def _square(av):
    af = av.astype(F32)
    return (af * af).astype(BF16)


MM_TILE_M = 1024
MM_TILE_N = 1024
MM_WHOLE_K = 4096
MM_TILE_K = 4096


def _mm_tiles(m, n, k, tn_pref=MM_TILE_N, k_block=None):
    tk = k if k <= MM_WHOLE_K else MM_TILE_K
    if k_block is not None:
        tk = min(tk, k_block)
    return _tile(m, MM_TILE_M), _tile(n, tn_pref), _tile(k, tk)


def _column_sharded_matmul(name, h, w_own, w_slabs, chip, finish, slab_major):
    seq, d_model = h.shape
    cols = w_own.shape[1]
    tm, tn, tk = _mm_tiles(seq, cols, d_model)
    assert tk == d_model
    nbj = cols // tn

    def body_own(chip_ref, a_ref, b_ref, o_ref):
        o_ref[...] = finish(jnp.dot(a_ref[...], b_ref[...], preferred_element_type=F32)).astype(BF16)

    def body_rest(chip_ref, a_ref, b_ref, own_ref, o_ref):
        o_ref[...] = finish(jnp.dot(a_ref[...], b_ref[...], preferred_element_type=F32)).astype(BF16)

    def slab(j, chip_ref):
        return (chip_ref[0] + 1 + j // nbj) % N_CHIPS

    if slab_major:
        out_shape = jax.ShapeDtypeStruct((N_CHIPS, seq, cols), BF16)
        out_block = (None, tm, tn)
        own_out = lambda i, j, chip_ref: (chip_ref[0], i, j)
        rest_out = lambda i, j, chip_ref: (slab(j, chip_ref), i, j % nbj)
    else:
        out_shape = jax.ShapeDtypeStruct((seq, N_CHIPS * cols), BF16)
        out_block = (tm, tn)
        own_out = lambda i, j, chip_ref: (i, chip_ref[0] * nbj + j)
        rest_out = lambda i, j, chip_ref: (i, slab(j, chip_ref) * nbj + j % nbj)
    rows = pl.BlockSpec((tm, tk), lambda i, j, chip_ref: (i, 0))
    own = pl.pallas_call(
        body_own, name=f"{name}_own",
        grid_spec=pltpu.PrefetchScalarGridSpec(
            num_scalar_prefetch=1, grid=(seq // tm, nbj),
            in_specs=[rows, pl.BlockSpec((tk, tn), lambda i, j, chip_ref: (0, j))],
            out_specs=pl.BlockSpec(out_block, own_out)),
        out_shape=out_shape, compiler_params=_params("parallel", "parallel"),
    )(chip, h, w_own)
    return pl.pallas_call(
        body_rest, name=f"{name}_rest",
        grid_spec=pltpu.PrefetchScalarGridSpec(
            num_scalar_prefetch=1, grid=(seq // tm, (N_CHIPS - 1) * nbj),
            in_specs=[rows, pl.BlockSpec((None, tk, tn), lambda i, j, chip_ref: (slab(j, chip_ref), 0, j % nbj)), ANY],
            out_specs=pl.BlockSpec(out_block, rest_out)),
        out_shape=out_shape, input_output_aliases={3: 0},
        compiler_params=_params("parallel", "parallel"),
    )(chip, h, w_slabs, own)


def _in_projection(h, w_own, w_slabs, chip):
    return _column_sharded_matmul("in_projection", h, w_own, w_slabs, chip, lambda acc: acc, True)


def _residual_projection(name, a, w, res, gate):
    seq, kdim = a.shape
    d_model = w.shape[1]
    tm, tn, tk = _mm_tiles(seq, d_model, kdim, tn_pref=MM_TILE_N // 2)
    tile = pl.BlockSpec((tm, tn), lambda i, j, k: (i, j))
    return _matmul(
        name, a, w, (res, gate), grid=(seq // tm, d_model // tn, kdim // tk), tiles=(tm, tn, tk), dims=NN,
        a_spec=pl.BlockSpec((tm, tk), lambda i, j, k: (i, k)),
        b_spec=pl.BlockSpec((tk, tn), lambda i, j, k: (k, j)),
        extra_specs=(tile, pl.BlockSpec((1, tn), lambda i, j, k: (0, j))),
        out_shape=[jax.ShapeDtypeStruct((seq, d_model), F32), jax.ShapeDtypeStruct((seq, d_model), BF16)],
        out_specs=[tile, tile],
        epilogue=_residual_epilogue)


def _mlp_out(act, w_own, w_slabs, res, gate, chip):
    seq = act.shape[0]
    rows, d_model = w_own.shape
    tm, tn, tk = _mm_tiles(seq, d_model, rows)
    assert tk == rows
    tn_rest = _tile(d_model, MM_TILE_N // 2)

    def body_own(chip_ref, a_ref, b_ref, o_ref):
        o_ref[...] = jnp.dot(_square(a_ref[...]), b_ref[...], preferred_element_type=F32)

    def body_rest(chip_ref, a_ref, b_ref, own_ref, res_ref, gate_ref, x_ref, branch_ref, acc_ref):
        k = pl.program_id(2)

        @pl.when(k == 0)
        def _():
            acc_ref[...] = own_ref[...]

        acc_ref[...] += jnp.dot(_square(a_ref[...]), b_ref[...], preferred_element_type=F32)

        @pl.when(k == N_CHIPS - 2)
        def _():
            _residual_epilogue(acc_ref[...], (res_ref, gate_ref), (x_ref, branch_ref))

    def slab(k, chip_ref):
        return (chip_ref[0] + 1 + k) % N_CHIPS

    own = pl.pallas_call(
        body_own, name="mlp_out_own",
        grid_spec=pltpu.PrefetchScalarGridSpec(
            num_scalar_prefetch=1, grid=(seq // tm, d_model // tn),
            in_specs=[pl.BlockSpec((tm, tk), lambda i, j, chip_ref: (i, chip_ref[0])),
                      pl.BlockSpec((tk, tn), lambda i, j, chip_ref: (0, j))],
            out_specs=pl.BlockSpec((tm, tn), lambda i, j, chip_ref: (i, j))),
        out_shape=jax.ShapeDtypeStruct((seq, d_model), F32),
        compiler_params=_params("parallel", "parallel"),
    )(chip, act, w_own)
    tile = pl.BlockSpec((tm, tn_rest), lambda i, j, k, chip_ref: (i, j))
    return pl.pallas_call(
        body_rest, name="mlp_out_rest",
        grid_spec=pltpu.PrefetchScalarGridSpec(
            num_scalar_prefetch=1, grid=(seq // tm, d_model // tn_rest, N_CHIPS - 1),
            in_specs=[pl.BlockSpec((tm, tk), lambda i, j, k, chip_ref: (i, slab(k, chip_ref))),
                      pl.BlockSpec((None, tk, tn_rest), lambda i, j, k, chip_ref: (slab(k, chip_ref), 0, j)),
                      tile, tile, pl.BlockSpec((1, tn_rest), lambda i, j, k, chip_ref: (0, j))],
            out_specs=[tile, tile],
            scratch_shapes=[pltpu.VMEM((tm, tn_rest), F32)]),
        out_shape=[jax.ShapeDtypeStruct((seq, d_model), F32), jax.ShapeDtypeStruct((seq, d_model), BF16)],
        compiler_params=_params("parallel", "parallel", "arbitrary"),
    )(chip, act, w_slabs, own, res, gate)


def _mlp_in(h, w_own, w_slabs, chip):
    return _column_sharded_matmul("mlp_in", h, w_own, w_slabs, chip, lambda acc: jnp.maximum(acc, 0.0), False)


def _grad_hidden(dmlp, w2, act, after=()):
    seq, d_model = dmlp.shape
    ff = w2.shape[0]
    tm, tn, tk = _mm_tiles(seq, ff, d_model)

    def epilogue(acc, extra_refs, out_refs):
        out_refs[0][...] = (acc * (2.0 * extra_refs[0][...].astype(F32))).astype(BF16)

    tile = pl.BlockSpec((tm, tn), lambda i, j, k: (i, j))
    return _matmul(
        "grad_hidden", dmlp, w2, (act,), grid=(seq // tm, ff // tn, d_model // tk), tiles=(tm, tn, tk), dims=NT,
        a_spec=pl.BlockSpec((tm, tk), lambda i, j, k: (i, k)),
        b_spec=pl.BlockSpec((tn, tk), lambda i, j, k: (j, k)),
        extra_specs=(tile,),
        out_shape=[jax.ShapeDtypeStruct((seq, ff), BF16)], out_specs=[tile],
        epilogue=epilogue, after=after)[0]


def _weight_grad(name, a, b, prologue=None):
    seq, m = a.shape
    n = b.shape[1]
    tm, tn, tk = _mm_tiles(m, n, seq)
    return _matmul(
        name, a, b, (), grid=(m // tm, n // tn, seq // tk), tiles=(tm, tn, tk), dims=TN,
        a_spec=pl.BlockSpec((tk, tm), lambda i, j, k: (k, i)),
        b_spec=pl.BlockSpec((tk, tn), lambda i, j, k: (k, j)),
        extra_specs=(),
        out_shape=[jax.ShapeDtypeStruct((m, n), BF16)],
        out_specs=[pl.BlockSpec((tm, tn), lambda i, j, k: (i, j))],
        epilogue=_store(BF16), prologue=prologue)[0]


def _weight_grad_slabs(name, a, b, slab_cols, after=()):
    seq, m = a.shape
    cols = b.shape[2] if slab_cols is None else slab_cols
    tm, tn, tk = _mm_tiles(m, cols, seq)
    nbj = cols // tn
    if slab_cols is None:
        b_spec = pl.BlockSpec((None, tk, tn), lambda i, j, k: (j // nbj, k, j % nbj))
    else:
        b_spec = pl.BlockSpec((tk, tn), lambda i, j, k: (k, j))
    return _matmul(
        name, a, b, (), grid=(m // tm, N_CHIPS * nbj, seq // tk), tiles=(tm, tn, tk), dims=TN,
        a_spec=pl.BlockSpec((tk, tm), lambda i, j, k: (k, i)),
        b_spec=b_spec, extra_specs=(),
        out_shape=[jax.ShapeDtypeStruct((N_CHIPS, m, cols), BF16)],
        out_specs=[pl.BlockSpec((None, tm, tn), lambda i, j, k: (j // nbj, i, j % nbj))],
        epilogue=_store(BF16), after=after)[0]


def _grad_input_slabs(name, dy, w_slabs, after=()):
    _, d_model, cols = w_slabs.shape
    seq = dy.shape[1] if dy.ndim == 3 else dy.shape[0]
    tm, tn, tk = _mm_tiles(seq, d_model, N_CHIPS * cols, k_block=cols)
    nbk = cols // tk
    if dy.ndim == 3:
        a_spec = pl.BlockSpec((None, tm, tk), lambda i, j, k: (k // nbk, i, k % nbk))
    else:
        a_spec = pl.BlockSpec((tm, tk), lambda i, j, k: (i, k))
    return _matmul(
        name, dy, w_slabs, (), grid=(seq // tm, d_model // tn, N_CHIPS * nbk), tiles=(tm, tn, tk), dims=NT,
        a_spec=a_spec,
        b_spec=pl.BlockSpec((None, tn, tk), lambda i, j, k: (k // nbk, j, k % nbk)),
        extra_specs=(),
        out_shape=[jax.ShapeDtypeStruct((seq, d_model), F32)],
        out_specs=[pl.BlockSpec((tm, tn), lambda i, j, k: (i, j))],
        epilogue=_store(F32), after=after)[0]


def _grad_input(name, dy, w):
    seq, n = dy.shape
    kdim = w.shape[0]
    tm, tn, tk = _mm_tiles(seq, kdim, n)
    return _matmul(
        name, dy, w, (), grid=(seq // tm, kdim // tn, n // tk), tiles=(tm, tn, tk), dims=NT,
        a_spec=pl.BlockSpec((tm, tk), lambda i, j, k: (i, k)),
        b_spec=pl.BlockSpec((tn, tk), lambda i, j, k: (j, k)),
        extra_specs=(),
        out_shape=[jax.ShapeDtypeStruct((seq, kdim), F32)],
        out_specs=[pl.BlockSpec((tm, tn), lambda i, j, k: (i, j))],
        epilogue=_store(F32))[0]


ROW_TILE = 128
ROW_TILE_FORWARD = 256


def _norm_modulate(name, xin, g, scale, shift):
    seq, d_model = xin.shape
    tr = _tile(seq, ROW_TILE_FORWARD)

    def body(x_ref, g_ref, sc_ref, sh_ref, h_ref):
        xv = x_ref[...]
        r = lax.rsqrt(jnp.mean(xv * xv, axis=-1, keepdims=True) + NORM_EPS)
        h_ref[...] = (((xv * r) * g_ref[...]) * (1.0 + sc_ref[...]) + sh_ref[...]).astype(BF16)

    row = pl.BlockSpec((tr, d_model), lambda i: (i, 0))
    vec = pl.BlockSpec((1, d_model), lambda i: (0, 0))
    return pl.pallas_call(
        body, name=name, grid=(seq // tr,),
        in_specs=[row, vec, vec, vec], out_specs=row,
        out_shape=jax.ShapeDtypeStruct((seq, d_model), BF16),
        compiler_params=_params("parallel"),
    )(xin, g, scale, shift)


def _loss_head(x2, target, final_g, mlp, gate2):
    seq, d_model = x2.shape
    tr = _tile(seq, ROW_TILE)

    def body(x_ref, t_ref, fg_ref, mlp_ref, gate_ref, dx_ref, dmlp_ref, gfg_ref, dgate_ref, sq_ref):
        @pl.when(pl.program_id(0) == 0)
        def _():
            gfg_ref[...] = jnp.zeros_like(gfg_ref)
            dgate_ref[...] = jnp.zeros_like(dgate_ref)
            sq_ref[...] = jnp.zeros_like(sq_ref)

        xv = x_ref[...]
        fg = fg_ref[...]
        r = lax.rsqrt(jnp.mean(xv * xv, axis=-1, keepdims=True) + NORM_EPS)
        n = xv * r
        err = n * fg - t_ref[...]
        sq_ref[...] += jnp.sum(err * err, axis=0, keepdims=True)
        dy = err * (1.0 / d_model)
        gfg_ref[...] += jnp.sum(dy * n, axis=0, keepdims=True)
        dn = dy * fg
        dx = r * (dn - n * jnp.mean(dn * n, axis=-1, keepdims=True))
        dx_ref[...] = dx
        dgate_ref[...] += jnp.sum(dx * mlp_ref[...].astype(F32), axis=0, keepdims=True)
        dmlp_ref[...] = (dx * gate_ref[...]).astype(BF16)

    row = pl.BlockSpec((tr, d_model), lambda i: (i, 0))
    vec = pl.BlockSpec((1, d_model), lambda i: (0, 0))
    vshape = jax.ShapeDtypeStruct((1, d_model), F32)
    return pl.pallas_call(
        body, name="loss_head", grid=(seq // tr,),
        in_specs=[row, row, vec, row, vec], out_specs=[row, row, vec, vec, vec],
        out_shape=[jax.ShapeDtypeStruct((seq, d_model), F32), jax.ShapeDtypeStruct((seq, d_model), BF16),
                   vshape, vshape, vshape],
        compiler_params=_params("arbitrary"),
    )(x2, target, final_g, mlp, gate2)


def _norm_modulate_backward(name, dh, xin, g, scale, dres, branch=None, gate=None, after=()):
    seq, d_model = xin.shape
    tr = _tile(seq, ROW_TILE)
    with_branch = branch is not None
    n_in = (7 if with_branch else 5) + len(after)

    def body(*refs):
        dh_ref, x_ref, g_ref, sc_ref, dres_ref = refs[:5]
        outs = refs[n_in:]
        dx_ref, dsc_ref, dsh_ref, dg_ref = outs[:4]

        @pl.when(pl.program_id(0) == 0)
        def _():
            for ref in outs[1:5] if with_branch else outs[1:4]:
                ref[...] = jnp.zeros_like(ref)

        xv = x_ref[...]
        gv = g_ref[...]
        dhv = dh_ref[...]
        r = lax.rsqrt(jnp.mean(xv * xv, axis=-1, keepdims=True) + NORM_EPS)
        xn = xv * r
        dsh_ref[...] += jnp.sum(dhv, axis=0, keepdims=True)
        dsc_ref[...] += jnp.sum(dhv * (xn * gv), axis=0, keepdims=True)
        t = dhv * (1.0 + sc_ref[...])
        dg_ref[...] += jnp.sum(t * xn, axis=0, keepdims=True)
        dxn = t * gv
        dx = dres_ref[...] + r * (dxn - xn * jnp.mean(dxn * xn, axis=-1, keepdims=True))
        dx_ref[...] = dx
        if with_branch:
            br_ref, gate_ref = refs[5:7]
            dgate_ref, dbr_ref = outs[4:6]
            dgate_ref[...] += jnp.sum(dx * br_ref[...].astype(F32), axis=0, keepdims=True)
            dbr_ref[...] = (dx * gate_ref[...]).astype(BF16)

    row = pl.BlockSpec((tr, d_model), lambda i: (i, 0))
    vec = pl.BlockSpec((1, d_model), lambda i: (0, 0))
    vshape = jax.ShapeDtypeStruct((1, d_model), F32)
    in_specs = [row, row, vec, vec, row]
    out_specs = [row, vec, vec, vec]
    out_shape = [jax.ShapeDtypeStruct((seq, d_model), F32), vshape, vshape, vshape]
    args = [dh, xin, g, scale, dres]
    if with_branch:
        in_specs += [row, vec]
        out_specs += [vec, row]
        out_shape += [vshape, jax.ShapeDtypeStruct((seq, d_model), BF16)]
        args += [branch, gate]
    in_specs += [ANY] * len(after)
    args += list(after)
    return pl.pallas_call(
        body, name=name, grid=(seq // tr,),
        in_specs=in_specs, out_specs=out_specs, out_shape=out_shape,
        compiler_params=_params("arbitrary"),
    )(*args)


def _shifted(v, k, t):
    seq = v.shape[0]
    if k == 0:
        return v
    moved = pltpu.roll(v, (-k) % seq, 0)
    return jnp.where((t + k >= 0) & (t + k < seq), moved, 0.0)


def _window_sum(v, offsets, t):
    acc = None
    for k in offsets:
        term = _shifted(v, k, t)
        acc = term if acc is None else acc + term
    return acc


def _window_count(seq, half):
    t = lax.broadcasted_iota(jnp.int32, (seq, 1), 0)
    return (jnp.minimum(t + half, seq) - jnp.maximum(t - half, 0)).astype(F32)


def _pool_forward(proj, group_dim):
    _, seq, cols = proj.shape
    tl = _tile(group_dim, 256)
    nbl = group_dim // tl
    n_groups = cols // group_dim

    def body(v_ref, o_ref):
        g = pl.program_id(0)
        for gi, window in enumerate(POOL_WINDOWS[:n_groups]):
            @pl.when(g == gi)
            def _(window=window):
                half = window // 2
                v = v_ref[...].astype(F32)
                t = lax.broadcasted_iota(jnp.int32, v.shape, 0)
                total = _window_sum(v, range(-half, half), t)
                o_ref[...] = (total / _window_count(seq, half) - v).astype(BF16)

    return pl.pallas_call(
        body, name="pool_forward", grid=(n_groups, nbl),
        in_specs=[pl.BlockSpec((None, seq, tl), lambda g, j: (0, 0, g * nbl + j))],
        out_specs=pl.BlockSpec((seq, tl), lambda g, j: (0, g * nbl + j)),
        out_shape=jax.ShapeDtypeStruct((seq, cols), BF16),
        compiler_params=_params("parallel", "parallel"),
    )(proj)


def _group_matrix(w_ref):
    return jnp.concatenate([w_ref[r] for r in range(N_CHIPS)], axis=0)


def _pool_mix_forward(pooled, w_pm, pool_scale, gnorm_g, d_model):
    seq, cols = pooled.shape
    _, n_groups, shard_rows, group_dim = w_pm.shape
    tm = _tile(seq, 512)

    def body(p_ref, w_ref, ps_ref, g_ref, o_ref):
        a = jnp.dot(p_ref[...], _group_matrix(w_ref), preferred_element_type=F32) * ps_ref[...]
        ra = lax.rsqrt(jnp.mean(a * a, axis=-1, keepdims=True) + NORM_EPS)
        o_ref[...] = ((a * ra) * g_ref[...]).astype(BF16)

    tile = pl.BlockSpec((tm, group_dim), lambda g, i: (i, g))
    vec = pl.BlockSpec((1, group_dim), lambda g, i: (0, g))
    return pl.pallas_call(
        body, name="pool_mix_forward", grid=(n_groups, seq // tm),
        in_specs=[tile, pl.BlockSpec((N_CHIPS, None, shard_rows, group_dim), lambda g, i: (0, g, 0, 0)), vec, vec],
        out_specs=tile,
        out_shape=jax.ShapeDtypeStruct((seq, d_model), BF16),
        compiler_params=_params("parallel", "parallel"),
    )(pooled, w_pm, pool_scale, gnorm_g)


def _conv_parts(b_ref, c_ref, u_ref, w_ref, bias_ref):
    bv = b_ref[...].astype(F32)
    cu = c_ref[...].astype(F32) * u_ref[...].astype(F32)
    t = lax.broadcasted_iota(jnp.int32, cu.shape, 0)
    prev, nxt = _shifted(cu, -1, t), _shifted(cu, 1, t)
    w = w_ref[...]
    conv = w[0:1] * prev + w[1:2] * cu + w[2:3] * nxt + bias_ref[...]
    return bv, cu, prev, nxt, conv, w, t


def _conv_forward(proj, conv_w, conv_b, gnorm_g, mixed):
    _, seq, cols = proj.shape
    tl = CONV_HEAD_DIM
    first = cols // tl

    def body(b_ref, c_ref, u_ref, w_ref, bias_ref, g_ref, mixed_ref, o_ref):
        bv, _, _, _, conv, _, _ = _conv_parts(b_ref, c_ref, u_ref, w_ref, bias_ref)
        bo = bv * conv
        rb = lax.rsqrt(jnp.mean(bo * bo, axis=-1, keepdims=True) + NORM_EPS)
        o_ref[...] = ((bo * rb) * g_ref[...]).astype(BF16)

    def slab(s):
        return pl.BlockSpec((None, seq, tl), lambda j, s=s: (s, 0, j))

    vec = pl.BlockSpec((1, tl), lambda j: (0, j))
    return pl.pallas_call(
        body, name="conv_forward", grid=(cols // tl,),
        in_specs=[slab(1), slab(2), slab(3), pl.BlockSpec((3, tl), lambda j: (0, j)), vec, vec, ANY],
        out_specs=pl.BlockSpec((seq, tl), lambda j: (0, first + j)),
        out_shape=jax.ShapeDtypeStruct(mixed.shape, mixed.dtype),
        input_output_aliases={6: 0},
        compiler_params=_params("parallel"),
    )(proj, proj, proj, conv_w, conv_b, gnorm_g, mixed)


def _pool_mix_backward(dmixed, pooled, w_pm, pool_scale, gnorm_g, after=()):
    seq, cols = pooled.shape
    _, n_groups, shard_rows, group_dim = w_pm.shape
    tm = _tile(seq, 512)

    def body(dm_ref, p_ref, w_ref, ps_ref, g_ref, *rest):
        dp_ref, dpm_ref, gg_ref, gps_ref = rest[-4:]

        @pl.when(pl.program_id(1) == 0)
        def _():
            gg_ref[...] = jnp.zeros_like(gg_ref)
            gps_ref[...] = jnp.zeros_like(gps_ref)

        w = _group_matrix(w_ref)
        ps = ps_ref[...]
        a_pre = jnp.dot(p_ref[...], w, preferred_element_type=F32)
        a = a_pre * ps
        ra = lax.rsqrt(jnp.mean(a * a, axis=-1, keepdims=True) + NORM_EPS)
        an = a * ra
        dm = dm_ref[...]
        gg_ref[...] += jnp.sum(dm * an, axis=0, keepdims=True)
        dan = dm * g_ref[...]
        da = ra * (dan - an * jnp.mean(dan * an, axis=-1, keepdims=True))
        gps_ref[...] += jnp.sum(da * a_pre, axis=0, keepdims=True)
        dpm = (da * ps).astype(BF16)
        dpm_ref[...] = dpm
        dp_ref[...] = lax.dot_general(dpm, w, NT, preferred_element_type=F32)

    tile = pl.BlockSpec((tm, group_dim), lambda g, i: (i, g))
    vec = pl.BlockSpec((1, group_dim), lambda g, i: (0, g))
    vshape = jax.ShapeDtypeStruct((1, cols), F32)
    return pl.pallas_call(
        body, name="pool_mix_backward", grid=(n_groups, seq // tm),
        in_specs=[tile, tile, pl.BlockSpec((N_CHIPS, None, shard_rows, group_dim), lambda g, i: (0, g, 0, 0)),
                  vec, vec] + [ANY] * len(after),
        out_specs=[tile, tile, vec, vec],
        out_shape=[jax.ShapeDtypeStruct((seq, cols), F32), jax.ShapeDtypeStruct((seq, cols), BF16), vshape, vshape],
        compiler_params=_params("parallel", "arbitrary"),
    )(dmixed, pooled, w_pm, pool_scale, gnorm_g, *after)


def _pool_mix_weight_grad(pooled, dpm, n_groups):
    seq, cols = pooled.shape
    group_dim = cols // n_groups
    shard_rows = group_dim // N_CHIPS
    tk = _tile(seq, 1024)
    gk = seq // tk

    def body(p_ref, d_ref, o_ref, acc_ref):
        k = pl.program_id(1)

        @pl.when(k == 0)
        def _():
            acc_ref[...] = jnp.zeros_like(acc_ref)

        acc_ref[...] += lax.dot_general(p_ref[...], d_ref[...], TN, preferred_element_type=F32)

        @pl.when(k == gk - 1)
        def _():
            for r in range(N_CHIPS):
                o_ref[r] = acc_ref[r * shard_rows:(r + 1) * shard_rows, :].astype(BF16)

    tile = pl.BlockSpec((tk, group_dim), lambda g, k: (k, g))
    return pl.pallas_call(
        body, name="pool_mix_weight_grad", grid=(n_groups, gk),
        in_specs=[tile, tile],
        out_specs=pl.BlockSpec((N_CHIPS, None, shard_rows, group_dim), lambda g, k: (0, g, 0, 0)),
        out_shape=jax.ShapeDtypeStruct((N_CHIPS, n_groups, shard_rows, group_dim), BF16),
        scratch_shapes=[pltpu.VMEM((group_dim, group_dim), F32)],
        compiler_params=_params("parallel", "arbitrary"),
    )(pooled, dpm)


def _mixers_backward(dpooled, dmixed, proj, conv_w, conv_b, gnorm_g, group_dim):
    _, seq, cols = proj.shape
    tl = CONV_HEAD_DIM
    first = cols // tl
    per_group = group_dim // tl
    n_groups = cols // group_dim

    def body(dp_ref, dm_ref, b_ref, c_ref, u_ref, w_ref, bias_ref, g_ref, o_ref, gg_ref, gb_ref, gw_ref):
        j = pl.program_id(0)
        for gi, window in enumerate(POOL_WINDOWS[:n_groups]):
            @pl.when(j // per_group == gi)
            def _(window=window):
                half = window // 2
                dp = dp_ref[...]
                t = lax.broadcasted_iota(jnp.int32, dp.shape, 0)
                dq = dp / _window_count(seq, half)
                o_ref[0] = (_window_sum(dq, range(-half + 1, half + 1), t) - dp).astype(BF16)

        bv, cu, prev, nxt, conv, w, t = _conv_parts(b_ref, c_ref, u_ref, w_ref, bias_ref)
        bo = bv * conv
        rb = lax.rsqrt(jnp.mean(bo * bo, axis=-1, keepdims=True) + NORM_EPS)
        bn = bo * rb
        dm = dm_ref[...]
        gg_ref[...] = jnp.sum(dm * bn, axis=0, keepdims=True)
        dbn = dm * g_ref[...]
        dbo = rb * (dbn - bn * jnp.mean(dbn * bn, axis=-1, keepdims=True))
        o_ref[1] = (dbo * conv).astype(BF16)
        dconv = dbo * bv
        gb_ref[...] = jnp.sum(dconv, axis=0, keepdims=True)
        gw_ref[0:1, :] = jnp.sum(dconv * prev, axis=0, keepdims=True)
        gw_ref[1:2, :] = jnp.sum(dconv * cu, axis=0, keepdims=True)
        gw_ref[2:3, :] = jnp.sum(dconv * nxt, axis=0, keepdims=True)
        dcu = w[0:1] * _shifted(dconv, 1, t) + w[1:2] * dconv + w[2:3] * _shifted(dconv, -1, t)
        o_ref[2] = (dcu * u_ref[...].astype(F32)).astype(BF16)
        o_ref[3] = (dcu * c_ref[...].astype(F32)).astype(BF16)

    def slab(s):
        return pl.BlockSpec((None, seq, tl), lambda j, s=s: (s, 0, j))

    vec = pl.BlockSpec((1, tl), lambda j: (0, j))
    rows3 = pl.BlockSpec((3, tl), lambda j: (0, j))
    vshape = jax.ShapeDtypeStruct((1, cols), F32)
    return pl.pallas_call(
        body, name="mixers_backward", grid=(cols // tl,),
        in_specs=[pl.BlockSpec((seq, tl), lambda j: (0, j)), pl.BlockSpec((seq, tl), lambda j: (0, first + j)),
                  slab(1), slab(2), slab(3), rows3, vec, vec],
        out_specs=[pl.BlockSpec((N_CHIPS, seq, tl), lambda j: (0, 0, j)), vec, vec, rows3],
        out_shape=[jax.ShapeDtypeStruct((N_CHIPS, seq, cols), BF16), vshape, vshape,
                   jax.ShapeDtypeStruct((3, cols), F32)],
        compiler_params=_params("parallel"),
    )(dpooled, dmixed, proj, proj, proj, conv_w, conv_b, gnorm_g)


class _GradReduction:
    def __init__(self, tag, grads, states, pair_id, scatter_id, position):
        self.tag, self.grads, self.states = tag, grads, states
        self.pair_id, self.scatter_id = pair_id, scatter_id
        self.chip_core, self.core, self.other_core = position

    def exchange(self):
        self.received = _exchange_halves(f"exchange_{self.tag}", self.grads, self.pair_id)

    def combine(self, after=()):
        self.parts = [_add_half(f"add_half_{self.tag}_{a}", g, r, self.core, after)
                      for a, (g, r) in enumerate(zip(self.grads, self.received))]

    def scatter(self):
        self.landed = _scatter_partials(f"scatter_{self.tag}", self.parts, self.scatter_id)

    def reduce(self, after=()):
        done = [_reduce_update(f"reduce_update_{self.tag}_{a}", p, l, *state, self.chip_core, after)
                for a, (p, l, state) in enumerate(zip(self.parts, self.landed, self.states))]
        self.reduced = [reduced for reduced, _ in done]
        self.mine = [updated for _, updated in done]

    def swap(self):
        self.swapped = _swap_reduced(f"swap_{self.tag}", self.reduced, self.pair_id)

    def update_other(self, after=()):
        self.results = [_adamw_half(f"adamw_other_{self.tag}_{a}", g, *state, self.other_core, done, after)
                        for a, (g, state, done) in enumerate(zip(self.swapped, self.states, self.mine))]

    def token(self, stage):
        first = getattr(self, stage)[0]
        return first if not isinstance(first, (list, tuple)) else first[0]


def kernel(x, c, w_ada, b_ada, norm1_g, w_in, pool_mix_w, pool_scale, conv_w, conv_b, gnorm_pool_g, gnorm_conv_g, w_out, norm2_g, w_mlp_in, w_mlp_out, final_g, loss_target, m_w_ada, m_b_ada, m_norm1_g, m_w_in, m_pool_mix_w, m_pool_scale, m_conv_w, m_conv_b, m_gnorm_pool_g, m_gnorm_conv_g, m_w_out, m_norm2_g, m_w_mlp_in, m_w_mlp_out, m_final_g, v_w_ada, v_b_ada, v_norm1_g, v_w_in, v_pool_mix_w, v_pool_scale, v_conv_w, v_conv_b, v_gnorm_pool_g, v_gnorm_conv_g, v_w_out, v_norm2_g, v_w_mlp_in, v_w_mlp_out, v_final_g):
    seq, d_model = x.shape[1], x.shape[2]
    cols = w_in.shape[2]
    n_groups, group_dim = pool_mix_w.shape[1], pool_mix_w.shape[3]
    shard_rows = pool_mix_w.shape[2]
    ff_cols = w_mlp_in.shape[2]
    ada_cols = w_ada.shape[2]
    conv_shard = conv_w.shape[2]
    assert pool_scale.shape[1] == cols and conv_b.shape[1] == cols and n_groups * group_dim == cols
    assert cols % CONV_HEAD_DIM == 0 and group_dim % CONV_HEAD_DIM == 0 and shard_rows * N_CHIPS == group_dim

    ix, iy, ic = _position()
    chip = 2 * ix + iy
    me = 4 * ix + 2 * iy + ic
    position = tuple(jnp.stack(v).astype(jnp.int32) for v in ([chip, ic], [ic], [1 - ic]))

    xs, target = x[0], loss_target[0]
    final_row = final_g.reshape(1, d_model)

    small = _gather_flat("gather_cond", jnp.concatenate([c[0], conv_w[0].reshape(-1)]))
    c_all = small[:, :d_model]
    conv_w_full = jnp.concatenate(
        [small[2 * j, d_model:].reshape(3, conv_shard) for j in range(N_CHIPS)], axis=1)
    b_cols = lax.dynamic_slice_in_dim(b_ada, chip * ada_cols, ada_cols, axis=1)
    mod_part = _ada_forward(c_all, w_ada[0], b_cols)
    mod_all = _gather_flat("gather_mod", mod_part.reshape(-1)).reshape(N_DEV, N_DEV, ada_cols)
    mod = jnp.concatenate(
        [lax.dynamic_slice_in_dim(mod_all[2 * j], me, 1, axis=0) for j in range(N_CHIPS)], axis=1)
    shift1, scale1, gate1, shift2, scale2, gate2 = [mod[:, i * d_model:(i + 1) * d_model] for i in range(N_MOD)]

    shards = [w_in[0].astype(BF16), pool_mix_w[0].reshape(n_groups * shard_rows, group_dim).astype(BF16),
              w_out[0].astype(BF16), w_mlp_in[0].astype(BF16), w_mlp_out[0].astype(BF16)]
    wg_in, wg_pm = _gather_weights("gather_w_in", shards[0:2], 1)
    (wg_out,) = _gather_weights("gather_w_out", shards[2:3], 2)
    (wg_1,) = _gather_weights("gather_w_mlp_in", shards[3:4], 3)
    (wg_2,) = _gather_weights("gather_w_mlp_out", shards[4:5], 4)
    wg_in, wg_pm, wg_out, wg_1, wg_2 = [
        _place_own(f"place_own_{a}", g, s, position[0])
        for a, (g, s) in enumerate(zip([wg_in, wg_pm, wg_out, wg_1, wg_2], shards))]
    wg_pm = wg_pm.reshape(N_CHIPS, n_groups, shard_rows, group_dim)
    wg_out = wg_out.reshape(d_model, d_model)
    wg_2 = wg_2.reshape(N_CHIPS * ff_cols, d_model)

    h1 = _norm_modulate("norm_modulate_1", xs, norm1_g, scale1, shift1)
    proj = _in_projection(h1, shards[0], wg_in, position[0])
    pooled = _pool_forward(proj, group_dim)
    mixed = _pool_mix_forward(pooled, wg_pm, pool_scale, gnorm_pool_g, d_model)
    mixed = _conv_forward(proj, conv_w_full, conv_b, gnorm_conv_g, mixed)
    x1, attn = _residual_projection("out_projection", mixed, wg_out, xs, gate1)
    h2 = _norm_modulate("norm_modulate_2", x1, norm2_g, scale2, shift2)
    act = _mlp_in(h2, shards[3], wg_1, position[0])
    x2, mlp = _mlp_out(act, shards[4], wg_2.reshape(N_CHIPS, ff_cols, d_model), x1, gate2, position[0])

    dx2, dmlp, g_final, dgate2, sq_err = _loss_head(x2, target, final_row, mlp, gate2)
    gw_2 = _weight_grad("grad_w_mlp_out", act, dmlp, prologue=_square)
    red_2 = _GradReduction("w_mlp_out", [gw_2.reshape(N_CHIPS, ff_cols, d_model)],
                           [(w_mlp_out[0], m_w_mlp_out[0], v_w_mlp_out[0])], 8, 12, position)
    red_2.exchange()
    dhid = _grad_hidden(dmlp, wg_2, act, after=[gw_2])
    red_2.combine(after=[dhid])
    red_2.scatter()
    gw_1 = _weight_grad_slabs("grad_w_mlp_in", h2, dhid, ff_cols, after=[red_2.token("parts")])
    red_1 = _GradReduction("w_mlp_in", [gw_1], [(w_mlp_in[0], m_w_mlp_in[0], v_w_mlp_in[0])], 7, 11, position)
    red_1.exchange()
    dh2 = _grad_input_slabs("grad_h2", dhid, wg_1, after=[gw_1])
    red_2.reduce(after=[dh2])
    red_2.swap()
    red_1.combine(after=[red_2.token("mine")])
    red_1.scatter()
    dx1, dscale2, dshift2, g_norm2, dgate1, dattn = _norm_modulate_backward(
        "norm_modulate_backward_2", dh2, x1, norm2_g, scale2, dx2, attn, gate1, after=[red_1.token("parts")])
    gw_out = _weight_grad("grad_w_out", mixed, dattn)
    red_out = _GradReduction("w_out", [gw_out.reshape(N_CHIPS, d_model // N_CHIPS, d_model)],
                             [(w_out[0], m_w_out[0], v_w_out[0])], 6, 10, position)
    red_out.exchange()
    dmixed = _grad_input("grad_mixed", dattn, wg_out)
    dpooled, dpm, g_gpool, g_pscale = _pool_mix_backward(dmixed, pooled, wg_pm, pool_scale, gnorm_pool_g)
    gw_pm = _pool_mix_weight_grad(pooled, dpm, n_groups)
    dproj, g_gconv, g_convb, g_convw = _mixers_backward(
        dpooled, dmixed, proj, conv_w_full, conv_b, gnorm_conv_g, group_dim)
    gw_in = _weight_grad_slabs("grad_w_in", h1, dproj, None)
    pm2d = (n_groups * shard_rows, group_dim)
    red_in = _GradReduction(
        "w_in", [gw_in, gw_pm.reshape((N_CHIPS,) + pm2d)],
        [(w_in[0], m_w_in[0], v_w_in[0]),
         (pool_mix_w[0].reshape(pm2d), m_pool_mix_w[0].reshape(pm2d), v_pool_mix_w[0].reshape(pm2d))],
        5, 9, position)
    red_in.exchange()
    red_1.reduce(after=[gw_in])
    red_1.swap()
    red_out.combine(after=[red_1.token("mine")])
    red_out.scatter()
    red_in.combine(after=[red_out.token("parts")])
    red_in.scatter()
    dh1 = _grad_input_slabs("grad_h1", dproj, wg_in, after=[red_in.token("parts")])
    red_2.update_other(after=[dh1])
    grad_x, dscale1, dshift1, g_norm1 = _norm_modulate_backward(
        "norm_modulate_backward_1", dh1, xs, norm1_g, scale1, dx1, after=[red_2.token("results")])
    red_out.reduce(after=[grad_x])
    red_out.swap()
    red_1.update_other(after=[red_out.token("mine")])

    mine = jnp.concatenate(
        [dshift1, dscale1, dgate1, dshift2, dscale2, dgate2, g_norm1, g_norm2, g_final, sq_err,
         g_pscale, g_convb, g_gpool, g_gconv, g_convw.reshape(1, 3 * cols)], axis=1)
    gathered = _gather_flat("gather_small", mine.reshape(-1))
    sums, loss = _reduce_small(gathered, d_model)
    n_rep = (N_MOD + 3) * d_model
    g_rep = jnp.concatenate([sums[:, :n_rep], sums[:, n_rep + d_model:n_rep + d_model + 4 * cols]], axis=1)
    n_small = g_rep.shape[1]

    def pack(b, n1, n2, fg, ps, cb, gp, gc):
        return jnp.concatenate([b, n1, n2, fg.reshape(1, d_model), ps, cb, gp, gc], axis=1).reshape(8, n_small // 8)

    d_rep, m_rep, v_rep = _adamw(
        "adamw_small", g_rep.reshape(8, n_small // 8),
        pack(b_ada, norm1_g, norm2_g, final_g, pool_scale, conv_b, gnorm_pool_g, gnorm_conv_g),
        pack(m_b_ada, m_norm1_g, m_norm2_g, m_final_g, m_pool_scale, m_conv_b, m_gnorm_pool_g, m_gnorm_conv_g),
        pack(v_b_ada, v_norm1_g, v_norm2_g, v_final_g, v_pool_scale, v_conv_b, v_gnorm_pool_g, v_gnorm_conv_g))

    def unpack(flat):
        flat = flat.reshape(1, n_small)
        sizes = [N_MOD * d_model, d_model, d_model, d_model, cols, cols, cols, cols]
        parts, at = [], 0
        for size in sizes:
            parts.append(flat[:, at:at + size])
            at += size
        parts[3] = parts[3].reshape(d_model)
        return parts

    g_convw_full = sums[:, n_rep + d_model + 4 * cols:].reshape(3, cols)
    g_convw_mine = lax.dynamic_slice_in_dim(g_convw_full, chip * conv_shard, conv_shard, axis=1)
    d_convw, m_convw, v_convw = _adamw("adamw_conv_w", g_convw_mine, conv_w[0], m_conv_w[0], v_conv_w[0])

    dmod_cols = lax.dynamic_slice_in_dim(gathered[:, :N_MOD * d_model], chip * ada_cols, ada_cols, axis=1)
    g_ada, d_ada, mn_ada, vn_ada = _ada_backward(c_all.T, dmod_cols, w_ada[0], m_w_ada[0], v_w_ada[0])

    red_in.reduce(after=[g_ada, red_1.token("results")])
    red_in.swap()
    red_out.update_other(after=[red_in.token("mine")])
    red_in.update_other(after=[red_out.token("results")])

    small_parts = [unpack(g_rep), unpack(d_rep), unpack(m_rep), unpack(v_rep)]
    ada_parts = [g_ada, d_ada, mn_ada, vn_ada]
    convw_parts = [g_convw_mine, d_convw, m_convw, v_convw]

    def ordered(k):
        b, n1, n2, fg, ps, cb, gp, gc = small_parts[k]
        return [ada_parts[k][None], b, n1, red_in.results[0][k][None],
                red_in.results[1][k].reshape(pool_mix_w.shape), ps, convw_parts[k][None], cb, gp, gc,
                red_out.results[0][k][None], n2, red_1.results[0][k][None], red_2.results[0][k][None], fg]

    return (loss[0, 0], grad_x[None], *ordered(0), *ordered(1), *ordered(2), *ordered(3))
```

```python
import jax
import jax.numpy as jnp
from jax import lax
from jax.experimental import pallas as pl
from jax.experimental.pallas import tpu as pltpu
from jax.experimental.pallas import tpu_sc as plsc

F32 = jnp.float32
BF16 = jnp.bfloat16
MESH = pl.DeviceIdType.MESH
ANY = pl.BlockSpec(memory_space=pl.ANY)

NORM_EPS = 1e-6
POOL_WINDOWS = (2, 4, 8, 16)
CONV_HEAD_DIM = 128
N_MOD = 6
N_CHIPS = 4
N_DEV = 8

ADAM_LR = 0.001
ADAM_B1 = 0.9
ADAM_B2 = 0.999
ADAM_EPS = 1e-08
ADAM_WD = 0.01
ADAM_STEP = 10
ADAM_BIAS1 = 1.0 - ADAM_B1 ** ADAM_STEP
ADAM_BIAS2 = 1.0 - ADAM_B2 ** ADAM_STEP

V7X_VMEM_BYTES = 64 * 1024 * 1024
VMEM_LIMIT_BYTES = V7X_VMEM_BYTES - 8 * 1024 * 1024

NN = (((1,), (0,)), ((), ()))
NT = (((1,), (1,)), ((), ()))
TN = (((0,), (0,)), ((), ()))


def _tile(n, pref):
    t = min(n, pref)
    while n % t:
        t //= 2
    return t


STREAM_BLOCK_BYTES = 2 * 1024 * 1024


def _stream_block(rows, cols, itemsize):
    tc = cols if 8 * cols * itemsize <= STREAM_BLOCK_BYTES else _tile(cols, 2048)
    fit = max(8, STREAM_BLOCK_BYTES // (tc * itemsize))
    return _tile(rows, 1 << (fit.bit_length() - 1)), tc


def _params(*sem):
    return pltpu.CompilerParams(dimension_semantics=sem, vmem_limit_bytes=VMEM_LIMIT_BYTES)


def _position():
    return lax.axis_index("x"), lax.axis_index("y"), lax.axis_index("c")


def _flip(ix, iy, ic, mask):
    return (1 - ix if mask & 4 else ix, 1 - iy if mask & 2 else iy, 1 - ic if mask & 1 else ic)


def _allgather8(name, blk):
    rows, cols = blk.shape

    def body(x_ref, out_ref, send_sems, recv_sems, local_sem):
        ix, iy, ic = _position()
        me = 4 * ix + 2 * iy + ic
        mine = pltpu.make_async_copy(x_ref, out_ref.at[me], local_sem)
        mine.start()
        sends = []
        for mask in range(1, N_DEV):
            cp = pltpu.make_async_remote_copy(
                src_ref=x_ref, dst_ref=out_ref.at[me],
                send_sem=send_sems.at[mask - 1], recv_sem=recv_sems.at[mask - 1],
                device_id=_flip(ix, iy, ic, mask), device_id_type=MESH)
            cp.start()
            sends.append(cp)
        for mask in range(1, N_DEV):
            px, py, pc = _flip(ix, iy, ic, mask)
            pltpu.make_async_remote_copy(
                src_ref=x_ref, dst_ref=out_ref.at[4 * px + 2 * py + pc],
                send_sem=send_sems.at[mask - 1], recv_sem=recv_sems.at[mask - 1],
                device_id=(px, py, pc), device_id_type=MESH).wait_recv()
        for cp in sends:
            cp.wait_send()
        mine.wait()

    return pl.pallas_call(
        body, name=name,
        out_shape=jax.ShapeDtypeStruct((N_DEV, rows, cols), F32),
        in_specs=[pl.BlockSpec(memory_space=pltpu.VMEM)],
        out_specs=pl.BlockSpec(memory_space=pltpu.VMEM),
        scratch_shapes=[pltpu.SemaphoreType.DMA((N_DEV - 1,)), pltpu.SemaphoreType.DMA((N_DEV - 1,)),
                        pltpu.SemaphoreType.DMA],
    )(blk)


def _gather_flat(name, vec):
    n = vec.shape[0]
    npad = -(-n // 1024) * 1024
    blk = jnp.pad(vec, (0, npad - n)).reshape(8, npad // 8)
    return _allgather8(name, blk).reshape(N_DEV, npad)[:, :n]


def _chip_relations(ix, iy):
    return [(1 - ix, iy), (ix, 1 - iy), (1 - ix, 1 - iy)]


def _gather_weights(name, shards, collective_id):
    n = len(shards)

    def body(*refs):
        src, out = refs[:n], refs[n:2 * n]
        send_sems, recv_sems = refs[2 * n:]
        ix, iy, ic = _position()
        chip, chip_x, chip_y, chip_d = 2 * ix + iy, 2 * (1 - ix) + iy, 2 * ix + 1 - iy, 2 * (1 - ix) + 1 - iy
        beside_x, beside_y, sibling = (1 - ix, iy, ic), (ix, 1 - iy, ic), (ix, iy, 1 - ic)

        _handshake([beside_x, beside_y, sibling])

        def rows(a, core, quarter=None):
            half = shards[a].shape[0] // 2
            if quarter is None:
                return pl.ds(core * half, half)
            return pl.ds(core * half + quarter * (half // 2), half // 2)

        def copy(a, k, src_ref, dst_ref, to):
            return pltpu.make_async_remote_copy(
                src_ref=src_ref, dst_ref=dst_ref, send_sem=send_sems.at[8 * a + k], recv_sem=recv_sems.at[8 * a + k],
                device_id=to, device_id_type=MESH)

        def relay(a, k, piece, to):
            return copy(a, k, out[a].at[piece], out[a].at[piece], to)

        started = []

        def start(cp):
            cp.start()
            started.append(cp)

        for a in range(n):
            mine = src[a].at[rows(a, ic)]
            start(copy(a, 0, mine, out[a].at[chip, rows(a, ic)], beside_x))
            start(copy(a, 1, mine, out[a].at[chip, rows(a, ic)], beside_y))
        for a in range(n):
            relay(a, 0, (chip_x, rows(a, ic)), beside_x).wait_recv()
            start(relay(a, 3, (chip_x, rows(a, ic, 1)), beside_y))
            start(relay(a, 4, (chip_x, rows(a, ic)), sibling))
            relay(a, 1, (chip_y, rows(a, ic)), beside_y).wait_recv()
            start(relay(a, 2, (chip_y, rows(a, ic, 0)), beside_x))
            start(relay(a, 5, (chip_y, rows(a, ic)), sibling))
        for a in range(n):
            relay(a, 2, (chip_d, rows(a, ic, 0)), beside_x).wait_recv()
            start(relay(a, 6, (chip_d, rows(a, ic, 0)), sibling))
            relay(a, 3, (chip_d, rows(a, ic, 1)), beside_y).wait_recv()
            start(relay(a, 7, (chip_d, rows(a, ic, 1)), sibling))
        for a in range(n):
            relay(a, 4, (chip_x, rows(a, 1 - ic)), sibling).wait_recv()
            relay(a, 5, (chip_y, rows(a, 1 - ic)), sibling).wait_recv()
            relay(a, 6, (chip_d, rows(a, 1 - ic, 0)), sibling).wait_recv()
            relay(a, 7, (chip_d, rows(a, 1 - ic, 1)), sibling).wait_recv()
        for cp in started:
            cp.wait_send()

    out_type = [jax.ShapeDtypeStruct((N_CHIPS,) + s.shape, s.dtype) for s in shards]
    return _sequencer_call(name, body, shards, out_type, [8 * n, 8 * n], collective_id)


def _place_own(name, gathered, shard, chip):
    rows, cols = shard.shape
    tr, tc = _stream_block(rows, cols, 2)

    def body(chip_ref, own_ref, gathered_ref, o_ref):
        o_ref[...] = own_ref[...]

    return pl.pallas_call(
        body, name=name,
        grid_spec=pltpu.PrefetchScalarGridSpec(
            num_scalar_prefetch=1, grid=(rows // tr, cols // tc),
            in_specs=[pl.BlockSpec((tr, tc), lambda i, j, chip_ref: (i, j)), ANY],
            out_specs=pl.BlockSpec((None, tr, tc), lambda i, j, chip_ref: (chip_ref[0], i, j))),
        out_shape=jax.ShapeDtypeStruct(gathered.shape, gathered.dtype),
        input_output_aliases={2: 0},
        compiler_params=_params("parallel", "parallel"),
    )(chip, shard, gathered)


def _sequencer_call(name, body, operands, out_type, sem_counts, collective_id):
    return pl.kernel(
        body, name=name, out_type=out_type,
        mesh=plsc.ScalarSubcoreMesh(axis_name="sequencer", num_cores=1),
        scratch_types=[pltpu.SemaphoreType.DMA((n,)) for n in sem_counts],
        compiler_params=pltpu.CompilerParams(collective_id=collective_id),
    )(*operands)


def _handshake(peers):
    barrier = pltpu.get_barrier_semaphore()
    for peer in peers:
        pl.semaphore_signal(barrier, inc=1, device_id=peer, device_id_type=MESH)
    pl.semaphore_wait(barrier, len(peers))


def _exchange_halves(name, grads, collective_id):
    n = len(grads)

    def body(*refs):
        src, out = refs[:n], refs[n:2 * n]
        send_sems, recv_sems = refs[2 * n:]
        ix, iy, ic = _position()
        sibling = (ix, iy, 1 - ic)
        _handshake([sibling])
        copies = []
        for a in range(n):
            half = grads[a].shape[1] // 2
            cp = pltpu.make_async_remote_copy(
                src_ref=src[a].at[pl.ds(0, N_CHIPS), pl.ds((1 - ic) * half, half)], dst_ref=out[a],
                send_sem=send_sems.at[a], recv_sem=recv_sems.at[a],
                device_id=sibling, device_id_type=MESH)
            cp.start()
            copies.append(cp)
        for cp in copies:
            cp.wait()

    out_type = [jax.ShapeDtypeStruct((N_CHIPS, g.shape[1] // 2, g.shape[2]), g.dtype) for g in grads]
    return _sequencer_call(name, body, grads, out_type, [n, n], collective_id)


def _scatter_partials(name, parts, collective_id):
    n = len(parts)

    def body(*refs):
        src, out = refs[:n], refs[n:2 * n]
        send_sems, recv_sems = refs[2 * n:]
        ix, iy, ic = _position()
        rels = _chip_relations(ix, iy)
        _handshake([(px, py, ic) for px, py in rels])
        copies = []
        for a in range(n):
            for r, (px, py) in enumerate(rels):
                cp = pltpu.make_async_remote_copy(
                    src_ref=src[a].at[2 * px + py], dst_ref=out[a].at[r],
                    send_sem=send_sems.at[3 * a + r], recv_sem=recv_sems.at[3 * a + r],
                    device_id=(px, py, ic), device_id_type=MESH)
                cp.start()
                copies.append(cp)
        for cp in copies:
            cp.wait()

    out_type = [jax.ShapeDtypeStruct((3,) + p.shape[1:], p.dtype) for p in parts]
    return _sequencer_call(name, body, parts, out_type, [3 * n, 3 * n], collective_id)


def _swap_reduced(name, reduced, collective_id):
    n = len(reduced)

    def body(*refs):
        src, out = refs[:n], refs[n:2 * n]
        send_sems, recv_sems = refs[2 * n:]
        ix, iy, ic = _position()
        sibling = (ix, iy, 1 - ic)
        _handshake([sibling])
        copies = []
        for a in range(n):
            cp = pltpu.make_async_remote_copy(
                src_ref=src[a], dst_ref=out[a], send_sem=send_sems.at[a], recv_sem=recv_sems.at[a],
                device_id=sibling, device_id_type=MESH)
            cp.start()
            copies.append(cp)
        for cp in copies:
            cp.wait()

    out_type = [jax.ShapeDtypeStruct(r.shape, r.dtype) for r in reduced]
    return _sequencer_call(name, body, reduced, out_type, [n, n], collective_id)


def _add_half(name, grad, recv, core, after=()):
    _, rows, cols = grad.shape
    half = rows // 2
    tr, tc = _stream_block(half, cols, 2)
    nbr = half // tr

    def body(core_ref, g_ref, r_ref, *rest):
        rest[-1][...] = g_ref[...] + r_ref[...]

    return pl.pallas_call(
        body, name=name,
        grid_spec=pltpu.PrefetchScalarGridSpec(
            num_scalar_prefetch=1, grid=(N_CHIPS, nbr, cols // tc),
            in_specs=[pl.BlockSpec((None, tr, tc), lambda s, i, j, core_ref: (s, core_ref[0] * nbr + i, j)),
                      pl.BlockSpec((None, tr, tc), lambda s, i, j, core_ref: (s, i, j))] + [ANY] * len(after),
            out_specs=pl.BlockSpec((None, tr, tc), lambda s, i, j, core_ref: (s, i, j))),
        out_shape=jax.ShapeDtypeStruct((N_CHIPS, half, cols), BF16),
        compiler_params=_params("parallel", "parallel", "parallel"),
    )(core, grad, recv, *after)


def _adamw_update(gv, wv, mv, vv):
    mn = ADAM_B1 * mv + (1.0 - ADAM_B1) * gv
    vn = ADAM_B2 * vv + (1.0 - ADAM_B2) * (gv * gv)
    delta = -ADAM_LR * ((mn / ADAM_BIAS1) / (jnp.sqrt(vn / ADAM_BIAS2) + ADAM_EPS) + ADAM_WD * wv)
    return delta, mn, vn


def _reduce_update(name, part, recv, w, m, v, chip_core, after=()):
    _, half, cols = part.shape
    tr, tc = _stream_block(half, cols, 4)
    nbr = half // tr

    def body(pos_ref, p_ref, r_ref, w_ref, m_ref, v_ref, *rest):
        red_ref, g_ref, d_ref, mo_ref, vo_ref = rest[-5:]
        gv = p_ref[...].astype(F32)
        for r in range(3):
            gv = gv + r_ref[r].astype(F32)
        red_ref[...] = gv
        g_ref[...] = gv
        d_ref[...], mo_ref[...], vo_ref[...] = _adamw_update(gv, w_ref[...], m_ref[...], v_ref[...])

    mine = pl.BlockSpec((tr, tc), lambda i, j, pos_ref: (pos_ref[1] * nbr + i, j))
    shape = jax.ShapeDtypeStruct((2 * half, cols), F32)
    reduced, *updated = pl.pallas_call(
        body, name=name,
        grid_spec=pltpu.PrefetchScalarGridSpec(
            num_scalar_prefetch=1, grid=(nbr, cols // tc),
            in_specs=[pl.BlockSpec((None, tr, tc), lambda i, j, pos_ref: (pos_ref[0], i, j)),
                      pl.BlockSpec((3, tr, tc), lambda i, j, pos_ref: (0, i, j)), mine, mine, mine]
            + [ANY] * len(after),
            out_specs=[pl.BlockSpec((tr, tc), lambda i, j, pos_ref: (i, j)), mine, mine, mine, mine]),
        out_shape=[jax.ShapeDtypeStruct((half, cols), F32), shape, shape, shape, shape],
        compiler_params=_params("parallel", "parallel"),
    )(chip_core, part, recv, w, m, v, *after)
    return reduced, updated


def _adamw_half(name, g_half, w, m, v, which, done, after=()):
    half, cols = g_half.shape
    tr, tc = _stream_block(half, cols, 4)
    nbr = half // tr

    def body(which_ref, g_ref, w_ref, m_ref, v_ref, *rest):
        go_ref, d_ref, mo_ref, vo_ref = rest[-4:]
        gv = g_ref[...]
        go_ref[...] = gv
        d_ref[...], mo_ref[...], vo_ref[...] = _adamw_update(gv, w_ref[...], m_ref[...], v_ref[...])

    mine = pl.BlockSpec((tr, tc), lambda i, j, which_ref: (which_ref[0] * nbr + i, j))
    shape = jax.ShapeDtypeStruct((2 * half, cols), F32)
    return pl.pallas_call(
        body, name=name,
        grid_spec=pltpu.PrefetchScalarGridSpec(
            num_scalar_prefetch=1, grid=(nbr, cols // tc),
            in_specs=([pl.BlockSpec((tr, tc), lambda i, j, which_ref: (i, j)), mine, mine, mine]
                      + [ANY] * (len(done) + len(after))),
            out_specs=[mine] * 4),
        out_shape=[shape] * 4,
        input_output_aliases={5 + k: k for k in range(len(done))},
        compiler_params=_params("parallel", "parallel"),
    )(which, g_half, w, m, v, *done, *after)


def _adamw(name, g, w, m, v):
    rows, cols = g.shape
    tr, tc = _stream_block(rows, cols, 4)

    def body(g_ref, w_ref, m_ref, v_ref, d_ref, mo_ref, vo_ref):
        d_ref[...], mo_ref[...], vo_ref[...] = _adamw_update(g_ref[...], w_ref[...], m_ref[...], v_ref[...])

    spec = pl.BlockSpec((tr, tc), lambda i, j: (i, j))
    shape = jax.ShapeDtypeStruct((rows, cols), F32)
    return pl.pallas_call(
        body, name=name, grid=(rows // tr, cols // tc),
        in_specs=[spec] * 4, out_specs=[spec] * 3, out_shape=[shape] * 3,
        compiler_params=_params("parallel", "parallel"),
    )(g, w, m, v)


def _reduce_small(gathered, d_model):
    n = gathered.shape[1]
    loss_at = (N_MOD + 3) * d_model

    def body(g_ref, s_ref, loss_ref):
        acc = g_ref[0:1, :]
        for d in range(1, N_DEV):
            acc = acc + g_ref[d:d + 1, :]
        s_ref[...] = acc
        lanes = acc[:, loss_at:loss_at + d_model]
        loss_ref[...] = jnp.broadcast_to((0.5 / d_model) * jnp.sum(lanes, axis=1, keepdims=True), loss_ref.shape)

    return pl.pallas_call(
        body, name="reduce_small",
        out_shape=[jax.ShapeDtypeStruct((1, n), F32), jax.ShapeDtypeStruct((1, 128), F32)],
        compiler_params=pltpu.CompilerParams(vmem_limit_bytes=VMEM_LIMIT_BYTES),
    )(gathered)


def _ada_forward(c_all, w_ada, b_cols):
    d_model, width = w_ada.shape
    tn = _tile(width, 512)

    def body(c_ref, w_ref, b_ref, o_ref):
        cv = c_ref[...]
        act = cv * jax.nn.sigmoid(cv)
        o_ref[...] = lax.dot_general(act, w_ref[...], NN, precision=lax.Precision.HIGHEST,
                                     preferred_element_type=F32) + b_ref[...]

    return pl.pallas_call(
        body, name="ada_forward", grid=(width // tn,),
        in_specs=[pl.BlockSpec((N_DEV, d_model), lambda j: (0, 0)),
                  pl.BlockSpec((d_model, tn), lambda j: (0, j)),
                  pl.BlockSpec((1, tn), lambda j: (0, j))],
        out_specs=pl.BlockSpec((N_DEV, tn), lambda j: (0, j)),
        out_shape=jax.ShapeDtypeStruct((N_DEV, width), F32),
        compiler_params=_params("parallel"),
    )(c_all, w_ada, b_cols)


def _ada_backward(c_all_t, dmod_cols, w, m, v):
    d_model, width = w.shape
    tr, tc = _stream_block(d_model, width, 4)

    def body(c_ref, dm_ref, w_ref, m_ref, v_ref, g_ref, d_ref, mo_ref, vo_ref):
        cv = c_ref[...]
        act = cv * jax.nn.sigmoid(cv)
        gv = lax.dot_general(act, dm_ref[...], NN, precision=lax.Precision.HIGHEST, preferred_element_type=F32)
        g_ref[...] = gv
        d_ref[...], mo_ref[...], vo_ref[...] = _adamw_update(gv, w_ref[...], m_ref[...], v_ref[...])

    spec = pl.BlockSpec((tr, tc), lambda i, j: (i, j))
    shape = jax.ShapeDtypeStruct((d_model, width), F32)
    return pl.pallas_call(
        body, name="ada_backward", grid=(d_model // tr, width // tc),
        in_specs=[pl.BlockSpec((tr, N_DEV), lambda i, j: (i, 0)),
                  pl.BlockSpec((N_DEV, tc), lambda i, j: (0, j)), spec, spec, spec],
        out_specs=[spec] * 4, out_shape=[shape] * 4,
        compiler_params=_params("parallel", "parallel"),
    )(c_all_t, dmod_cols, w, m, v)


def _matmul(name, a, b, extras, *, grid, tiles, dims, a_spec, b_spec, extra_specs, out_shape, out_specs,
            epilogue, prologue=None, after=()):
    tm, tn, _ = tiles
    gm, gn, gk = grid
    n_extra, n_out = len(extras), len(out_shape)
    first_out = 2 + n_extra + len(after)

    def product(a_ref, b_ref):
        av = a_ref[...]
        if prologue is not None:
            av = prologue(av)
        return lax.dot_general(av, b_ref[...], dims, preferred_element_type=F32)

    def body_single(*refs):
        epilogue(product(refs[0], refs[1]), refs[2:2 + n_extra], refs[first_out:first_out + n_out])

    def body(*refs):
        a_ref, b_ref = refs[0], refs[1]
        extra_refs = refs[2:2 + n_extra]
        out_refs = refs[first_out:first_out + n_out]
        acc_ref = refs[-1]
        k = pl.program_id(2)

        @pl.when(k == 0)
        def _():
            acc_ref[...] = jnp.zeros_like(acc_ref)

        acc_ref[...] += product(a_ref, b_ref)

        @pl.when(k == gk - 1)
        def _():
            epilogue(acc_ref[...], extra_refs, out_refs)

    single = gk == 1
    return pl.pallas_call(
        body_single if single else body, name=name, grid=(gm, gn, gk),
        in_specs=[a_spec, b_spec, *extra_specs] + [ANY] * len(after), out_specs=out_specs, out_shape=out_shape,
        scratch_shapes=[] if single else [pltpu.VMEM((tm, tn), F32)],
        compiler_params=_params("parallel", "parallel", "arbitrary"),
    )(a, b, *extras, *after)


def _store(dtype):
    def epilogue(acc, extra_refs, out_refs):
        out_refs[0][...] = acc.astype(dtype)
    return epilogue


def _residual_epilogue(acc, extra_refs, out_refs):
    res_ref, gate_ref = extra_refs
    out_refs[0][...] = res_ref[...] + gate_ref[...] * acc
    out_refs[1][...] = acc.astype(BF16)


def _square(av):
    af = av.astype(F32)
    return (af * af).astype(BF16)


MM_TILE_M = 1024
MM_TILE_N = 1024
MM_WHOLE_K = 4096
MM_TILE_K = 4096


def _mm_tiles(m, n, k, tn_pref=MM_TILE_N, k_block=None):
    tk = k if k <= MM_WHOLE_K else MM_TILE_K
    if k_block is not None:
        tk = min(tk, k_block)
    return _tile(m, MM_TILE_M), _tile(n, tn_pref), _tile(k, tk)


def _column_sharded_matmul(name, h, w_own, w_slabs, chip, finish, slab_major):
    seq, d_model = h.shape
    cols = w_own.shape[1]
    tm, tn, tk = _mm_tiles(seq, cols, d_model)
    assert tk == d_model
    nbj = cols // tn

    def body_own(chip_ref, a_ref, b_ref, o_ref):
        o_ref[...] = finish(jnp.dot(a_ref[...], b_ref[...], preferred_element_type=F32)).astype(BF16)

    def body_rest(chip_ref, a_ref, b_ref, own_ref, o_ref):
        o_ref[...] = finish(jnp.dot(a_ref[...], b_ref[...], preferred_element_type=F32)).astype(BF16)

    def slab(j, chip_ref):
        return (chip_ref[0] + 1 + j // nbj) % N_CHIPS

    if slab_major:
        out_shape = jax.ShapeDtypeStruct((N_CHIPS, seq, cols), BF16)
        out_block = (None, tm, tn)
        own_out = lambda i, j, chip_ref: (chip_ref[0], i, j)
        rest_out = lambda i, j, chip_ref: (slab(j, chip_ref), i, j % nbj)
    else:
        out_shape = jax.ShapeDtypeStruct((seq, N_CHIPS * cols), BF16)
        out_block = (tm, tn)
        own_out = lambda i, j, chip_ref: (i, chip_ref[0] * nbj + j)
        rest_out = lambda i, j, chip_ref: (i, slab(j, chip_ref) * nbj + j % nbj)
    rows = pl.BlockSpec((tm, tk), lambda i, j, chip_ref: (i, 0))
    own = pl.pallas_call(
        body_own, name=f"{name}_own",
        grid_spec=pltpu.PrefetchScalarGridSpec(
            num_scalar_prefetch=1, grid=(seq // tm, nbj),
            in_specs=[rows, pl.BlockSpec((tk, tn), lambda i, j, chip_ref: (0, j))],
            out_specs=pl.BlockSpec(out_block, own_out)),
        out_shape=out_shape, compiler_params=_params("parallel", "parallel"),
    )(chip, h, w_own)
    return pl.pallas_call(
        body_rest, name=f"{name}_rest",
        grid_spec=pltpu.PrefetchScalarGridSpec(
            num_scalar_prefetch=1, grid=(seq // tm, (N_CHIPS - 1) * nbj),
            in_specs=[rows, pl.BlockSpec((None, tk, tn), lambda i, j, chip_ref: (slab(j, chip_ref), 0, j % nbj)), ANY],
            out_specs=pl.BlockSpec(out_block, rest_out)),
        out_shape=out_shape, input_output_aliases={3: 0},
        compiler_params=_params("parallel", "parallel"),
    )(chip, h, w_slabs, own)


def _in_projection(h, w_own, w_slabs, chip):
    return _column_sharded_matmul("in_projection", h, w_own, w_slabs, chip, lambda acc: acc, True)


def _residual_projection(name, a, w, res, gate):
    seq, kdim = a.shape
    d_model = w.shape[1]
    tm, tn, tk = _mm_tiles(seq, d_model, kdim, tn_pref=MM_TILE_N // 2)
    tile = pl.BlockSpec((tm, tn), lambda i, j, k: (i, j))
    return _matmul(
        name, a, w, (res, gate), grid=(seq // tm, d_model // tn, kdim // tk), tiles=(tm, tn, tk), dims=NN,
        a_spec=pl.BlockSpec((tm, tk), lambda i, j, k: (i, k)),
        b_spec=pl.BlockSpec((tk, tn), lambda i, j, k: (k, j)),
        extra_specs=(tile, pl.BlockSpec((1, tn), lambda i, j, k: (0, j))),
        out_shape=[jax.ShapeDtypeStruct((seq, d_model), F32), jax.ShapeDtypeStruct((seq, d_model), BF16)],
        out_specs=[tile, tile],
        epilogue=_residual_epilogue)


def _mlp_out(act, w_own, w_slabs, res, gate, chip):
    seq = act.shape[0]
    rows, d_model = w_own.shape
    tm, tn, tk = _mm_tiles(seq, d_model, rows)
    assert tk == rows
    tn_rest = _tile(d_model, MM_TILE_N // 2)

    def body_own(chip_ref, a_ref, b_ref, o_ref):
        o_ref[...] = jnp.dot(_square(a_ref[...]), b_ref[...], preferred_element_type=F32)

    def body_rest(chip_ref, a_ref, b_ref, own_ref, res_ref, gate_ref, x_ref, branch_ref, acc_ref):
        k = pl.program_id(2)

        @pl.when(k == 0)
        def _():
            acc_ref[...] = own_ref[...]

        acc_ref[...] += jnp.dot(_square(a_ref[...]), b_ref[...], preferred_element_type=F32)

        @pl.when(k == N_CHIPS - 2)
        def _():
            _residual_epilogue(acc_ref[...], (res_ref, gate_ref), (x_ref, branch_ref))

    def slab(k, chip_ref):
        return (chip_ref[0] + 1 + k) % N_CHIPS

    own = pl.pallas_call(
        body_own, name="mlp_out_own",
        grid_spec=pltpu.PrefetchScalarGridSpec(
            num_scalar_prefetch=1, grid=(seq // tm, d_model // tn),
            in_specs=[pl.BlockSpec((tm, tk), lambda i, j, chip_ref: (i, chip_ref[0])),
                      pl.BlockSpec((tk, tn), lambda i, j, chip_ref: (0, j))],
            out_specs=pl.BlockSpec((tm, tn), lambda i, j, chip_ref: (i, j))),
        out_shape=jax.ShapeDtypeStruct((seq, d_model), F32),
        compiler_params=_params("parallel", "parallel"),
    )(chip, act, w_own)
    tile = pl.BlockSpec((tm, tn_rest), lambda i, j, k, chip_ref: (i, j))
    return pl.pallas_call(
        body_rest, name="mlp_out_rest",
        grid_spec=pltpu.PrefetchScalarGridSpec(
            num_scalar_prefetch=1, grid=(seq // tm, d_model // tn_rest, N_CHIPS - 1),
            in_specs=[pl.BlockSpec((tm, tk), lambda i, j, k, chip_ref: (i, slab(k, chip_ref))),
                      pl.BlockSpec((None, tk, tn_rest), lambda i, j, k, chip_ref: (slab(k, chip_ref), 0, j)),
                      tile, tile, pl.BlockSpec((1, tn_rest), lambda i, j, k, chip_ref: (0, j))],
            out_specs=[tile, tile],
            scratch_shapes=[pltpu.VMEM((tm, tn_rest), F32)]),
        out_shape=[jax.ShapeDtypeStruct((seq, d_model), F32), jax.ShapeDtypeStruct((seq, d_model), BF16)],
        compiler_params=_params("parallel", "parallel", "arbitrary"),
    )(chip, act, w_slabs, own, res, gate)


def _mlp_in(h, w_own, w_slabs, chip):
    return _column_sharded_matmul("mlp_in", h, w_own, w_slabs, chip, lambda acc: jnp.maximum(acc, 0.0), False)


def _grad_hidden(dmlp, w2, act, after=()):
    seq, d_model = dmlp.shape
    ff = w2.shape[0]
    tm, tn, tk = _mm_tiles(seq, ff, d_model)

    def epilogue(acc, extra_refs, out_refs):
        out_refs[0][...] = (acc * (2.0 * extra_refs[0][...].astype(F32))).astype(BF16)

    tile = pl.BlockSpec((tm, tn), lambda i, j, k: (i, j))
    return _matmul(
        "grad_hidden", dmlp, w2, (act,), grid=(seq // tm, ff // tn, d_model // tk), tiles=(tm, tn, tk), dims=NT,
        a_spec=pl.BlockSpec((tm, tk), lambda i, j, k: (i, k)),
        b_spec=pl.BlockSpec((tn, tk), lambda i, j, k: (j, k)),
        extra_specs=(tile,),
        out_shape=[jax.ShapeDtypeStruct((seq, ff), BF16)], out_specs=[tile],
        epilogue=epilogue, after=after)[0]


def _weight_grad(name, a, b, prologue=None):
    seq, m = a.shape
    n = b.shape[1]
    tm, tn, tk = _mm_tiles(m, n, seq)
    return _matmul(
        name, a, b, (), grid=(m // tm, n // tn, seq // tk), tiles=(tm, tn, tk), dims=TN,
        a_spec=pl.BlockSpec((tk, tm), lambda i, j, k: (k, i)),
        b_spec=pl.BlockSpec((tk, tn), lambda i, j, k: (k, j)),
        extra_specs=(),
        out_shape=[jax.ShapeDtypeStruct((m, n), BF16)],
        out_specs=[pl.BlockSpec((tm, tn), lambda i, j, k: (i, j))],
        epilogue=_store(BF16), prologue=prologue)[0]


def _weight_grad_slabs(name, a, b, slab_cols, after=()):
    seq, m = a.shape
    cols = b.shape[2] if slab_cols is None else slab_cols
    tm, tn, tk = _mm_tiles(m, cols, seq)
    nbj = cols // tn
    if slab_cols is None:
        b_spec = pl.BlockSpec((None, tk, tn), lambda i, j, k: (j // nbj, k, j % nbj))
    else:
        b_spec = pl.BlockSpec((tk, tn), lambda i, j, k: (k, j))
    return _matmul(
        name, a, b, (), grid=(m // tm, N_CHIPS * nbj, seq // tk), tiles=(tm, tn, tk), dims=TN,
        a_spec=pl.BlockSpec((tk, tm), lambda i, j, k: (k, i)),
        b_spec=b_spec, extra_specs=(),
        out_shape=[jax.ShapeDtypeStruct((N_CHIPS, m, cols), BF16)],
        out_specs=[pl.BlockSpec((None, tm, tn), lambda i, j, k: (j // nbj, i, j % nbj))],
        epilogue=_store(BF16), after=after)[0]


def _grad_input_slabs(name, dy, w_slabs, after=()):
    _, d_model, cols = w_slabs.shape
    seq = dy.shape[1] if dy.ndim == 3 else dy.shape[0]
    tm, tn, tk = _mm_tiles(seq, d_model, N_CHIPS * cols, k_block=cols)
    nbk = cols // tk
    if dy.ndim == 3:
        a_spec = pl.BlockSpec((None, tm, tk), lambda i, j, k: (k // nbk, i, k % nbk))
    else:
        a_spec = pl.BlockSpec((tm, tk), lambda i, j, k: (i, k))
    return _matmul(
        name, dy, w_slabs, (), grid=(seq // tm, d_model // tn, N_CHIPS * nbk), tiles=(tm, tn, tk), dims=NT,
        a_spec=a_spec,
        b_spec=pl.BlockSpec((None, tn, tk), lambda i, j, k: (k // nbk, j, k % nbk)),
        extra_specs=(),
        out_shape=[jax.ShapeDtypeStruct((seq, d_model), BF16)],
        out_specs=[pl.BlockSpec((tm, tn), lambda i, j, k: (i, j))],
        epilogue=_store(BF16), after=after)[0]


def _grad_input(name, dy, w):
    seq, n = dy.shape
    kdim = w.shape[0]
    tm, tn, tk = _mm_tiles(seq, kdim, n)
    return _matmul(
        name, dy, w, (), grid=(seq // tm, kdim // tn, n // tk), tiles=(tm, tn, tk), dims=NT,
        a_spec=pl.BlockSpec((tm, tk), lambda i, j, k: (i, k)),
        b_spec=pl.BlockSpec((tn, tk), lambda i, j, k: (j, k)),
        extra_specs=(),
        out_shape=[jax.ShapeDtypeStruct((seq, kdim), BF16)],
        out_specs=[pl.BlockSpec((tm, tn), lambda i, j, k: (i, j))],
        epilogue=_store(BF16))[0]


ROW_TILE = 128
ROW_TILE_FORWARD = 256


def _norm_modulate(name, xin, g, scale, shift):
    seq, d_model = xin.shape
    tr = _tile(seq, ROW_TILE_FORWARD)

    def body(x_ref, g_ref, sc_ref, sh_ref, h_ref):
        xv = x_ref[...]
        r = lax.rsqrt(jnp.mean(xv * xv, axis=-1, keepdims=True) + NORM_EPS)
        h_ref[...] = (((xv * r) * g_ref[...]) * (1.0 + sc_ref[...]) + sh_ref[...]).astype(BF16)

    row = pl.BlockSpec((tr, d_model), lambda i: (i, 0))
    vec = pl.BlockSpec((1, d_model), lambda i: (0, 0))
    return pl.pallas_call(
        body, name=name, grid=(seq // tr,),
        in_specs=[row, vec, vec, vec], out_specs=row,
        out_shape=jax.ShapeDtypeStruct((seq, d_model), BF16),
        compiler_params=_params("parallel"),
    )(xin, g, scale, shift)


def _loss_head(x2, target, final_g, mlp, gate2):
    seq, d_model = x2.shape
    tr = _tile(seq, ROW_TILE)

    def body(x_ref, t_ref, fg_ref, mlp_ref, gate_ref, dx_ref, dmlp_ref, gfg_ref, dgate_ref, sq_ref):
        @pl.when(pl.program_id(0) == 0)
        def _():
            gfg_ref[...] = jnp.zeros_like(gfg_ref)
            dgate_ref[...] = jnp.zeros_like(dgate_ref)
            sq_ref[...] = jnp.zeros_like(sq_ref)

        xv = x_ref[...]
        fg = fg_ref[...]
        r = lax.rsqrt(jnp.mean(xv * xv, axis=-1, keepdims=True) + NORM_EPS)
        n = xv * r
        err = n * fg - t_ref[...]
        sq_ref[...] += jnp.sum(err * err, axis=0, keepdims=True)
        dy = err * (1.0 / d_model)
        gfg_ref[...] += jnp.sum(dy * n, axis=0, keepdims=True)
        dn = dy * fg
        dx = r * (dn - n * jnp.mean(dn * n, axis=-1, keepdims=True))
        dx_ref[...] = dx
        dgate_ref[...] += jnp.sum(dx * mlp_ref[...].astype(F32), axis=0, keepdims=True)
        dmlp_ref[...] = (dx * gate_ref[...]).astype(BF16)

    row = pl.BlockSpec((tr, d_model), lambda i: (i, 0))
    vec = pl.BlockSpec((1, d_model), lambda i: (0, 0))
    vshape = jax.ShapeDtypeStruct((1, d_model), F32)
    return pl.pallas_call(
        body, name="loss_head", grid=(seq // tr,),
        in_specs=[row, row, vec, row, vec], out_specs=[row, row, vec, vec, vec],
        out_shape=[jax.ShapeDtypeStruct((seq, d_model), F32), jax.ShapeDtypeStruct((seq, d_model), BF16),
                   vshape, vshape, vshape],
        compiler_params=_params("arbitrary"),
    )(x2, target, final_g, mlp, gate2)


def _norm_modulate_backward(name, dh, xin, g, scale, dres, branch=None, gate=None, after=()):
    seq, d_model = xin.shape
    tr = _tile(seq, ROW_TILE)
    with_branch = branch is not None
    n_in = (7 if with_branch else 5) + len(after)

    def body(*refs):
        dh_ref, x_ref, g_ref, sc_ref, dres_ref = refs[:5]
        outs = refs[n_in:]
        dx_ref, dsc_ref, dsh_ref, dg_ref = outs[:4]

        @pl.when(pl.program_id(0) == 0)
        def _():
            for ref in outs[1:5] if with_branch else outs[1:4]:
                ref[...] = jnp.zeros_like(ref)

        xv = x_ref[...]
        gv = g_ref[...]
        dhv = dh_ref[...].astype(F32)
        r = lax.rsqrt(jnp.mean(xv * xv, axis=-1, keepdims=True) + NORM_EPS)
        xn = xv * r
        dsh_ref[...] += jnp.sum(dhv, axis=0, keepdims=True)
        dsc_ref[...] += jnp.sum(dhv * (xn * gv), axis=0, keepdims=True)
        t = dhv * (1.0 + sc_ref[...])
        dg_ref[...] += jnp.sum(t * xn, axis=0, keepdims=True)
        dxn = t * gv
        dx = dres_ref[...] + r * (dxn - xn * jnp.mean(dxn * xn, axis=-1, keepdims=True))
        dx_ref[...] = dx
        if with_branch:
            br_ref, gate_ref = refs[5:7]
            dgate_ref, dbr_ref = outs[4:6]
            dgate_ref[...] += jnp.sum(dx * br_ref[...].astype(F32), axis=0, keepdims=True)
            dbr_ref[...] = (dx * gate_ref[...]).astype(BF16)

    row = pl.BlockSpec((tr, d_model), lambda i: (i, 0))
    vec = pl.BlockSpec((1, d_model), lambda i: (0, 0))
    vshape = jax.ShapeDtypeStruct((1, d_model), F32)
    in_specs = [row, row, vec, vec, row]
    out_specs = [row, vec, vec, vec]
    out_shape = [jax.ShapeDtypeStruct((seq, d_model), F32), vshape, vshape, vshape]
    args = [dh, xin, g, scale, dres]
    if with_branch:
        in_specs += [row, vec]
        out_specs += [vec, row]
        out_shape += [vshape, jax.ShapeDtypeStruct((seq, d_model), BF16)]
        args += [branch, gate]
    in_specs += [ANY] * len(after)
    args += list(after)
    return pl.pallas_call(
        body, name=name, grid=(seq // tr,),
        in_specs=in_specs, out_specs=out_specs, out_shape=out_shape,
        compiler_params=_params("arbitrary"),
    )(*args)


def _shifted(v, k, t):
    seq = v.shape[0]
    if k == 0:
        return v
    moved = pltpu.roll(v, (-k) % seq, 0)
    return jnp.where((t + k >= 0) & (t + k < seq), moved, 0.0)


def _window_sum(v, offsets, t):
    acc = None
    for k in offsets:
        term = _shifted(v, k, t)
        acc = term if acc is None else acc + term
    return acc


def _window_count(seq, half):
    t = lax.broadcasted_iota(jnp.int32, (seq, 1), 0)
    return (jnp.minimum(t + half, seq) - jnp.maximum(t - half, 0)).astype(F32)


def _pool_forward(proj, group_dim):
    _, seq, cols = proj.shape
    tl = _tile(group_dim, 256)
    nbl = group_dim // tl
    n_groups = cols // group_dim

    def body(v_ref, o_ref):
        g = pl.program_id(0)
        for gi, window in enumerate(POOL_WINDOWS[:n_groups]):
            @pl.when(g == gi)
            def _(window=window):
                half = window // 2
                v = v_ref[...].astype(F32)
                t = lax.broadcasted_iota(jnp.int32, v.shape, 0)
                total = _window_sum(v, range(-half, half), t)
                o_ref[...] = (total / _window_count(seq, half) - v).astype(BF16)

    return pl.pallas_call(
        body, name="pool_forward", grid=(n_groups, nbl),
        in_specs=[pl.BlockSpec((None, seq, tl), lambda g, j: (0, 0, g * nbl + j))],
        out_specs=pl.BlockSpec((seq, tl), lambda g, j: (0, g * nbl + j)),
        out_shape=jax.ShapeDtypeStruct((seq, cols), BF16),
        compiler_params=_params("parallel", "parallel"),
    )(proj)


def _group_matrix(w_ref):
    return jnp.concatenate([w_ref[r] for r in range(N_CHIPS)], axis=0)


def _pool_mix_forward(pooled, w_pm, pool_scale, gnorm_g, d_model):
    seq, cols = pooled.shape
    _, n_groups, shard_rows, group_dim = w_pm.shape
    tm = _tile(seq, 512)

    def body(p_ref, w_ref, ps_ref, g_ref, o_ref):
        a = jnp.dot(p_ref[...], _group_matrix(w_ref), preferred_element_type=F32) * ps_ref[...]
        ra = lax.rsqrt(jnp.mean(a * a, axis=-1, keepdims=True) + NORM_EPS)
        o_ref[...] = ((a * ra) * g_ref[...]).astype(BF16)

    tile = pl.BlockSpec((tm, group_dim), lambda g, i: (i, g))
    vec = pl.BlockSpec((1, group_dim), lambda g, i: (0, g))
    return pl.pallas_call(
        body, name="pool_mix_forward", grid=(n_groups, seq // tm),
        in_specs=[tile, pl.BlockSpec((N_CHIPS, None, shard_rows, group_dim), lambda g, i: (0, g, 0, 0)), vec, vec],
        out_specs=tile,
        out_shape=jax.ShapeDtypeStruct((seq, d_model), BF16),
        compiler_params=_params("parallel", "parallel"),
    )(pooled, w_pm, pool_scale, gnorm_g)


def _conv_parts(b_ref, c_ref, u_ref, w_ref, bias_ref):
    bv = b_ref[...].astype(F32)
    cu = c_ref[...].astype(F32) * u_ref[...].astype(F32)
    t = lax.broadcasted_iota(jnp.int32, cu.shape, 0)
    prev, nxt = _shifted(cu, -1, t), _shifted(cu, 1, t)
    w = w_ref[...]
    conv = w[0:1] * prev + w[1:2] * cu + w[2:3] * nxt + bias_ref[...]
    return bv, cu, prev, nxt, conv, w, t


def _conv_forward(proj, conv_w, conv_b, gnorm_g, mixed):
    _, seq, cols = proj.shape
    tl = CONV_HEAD_DIM
    first = cols // tl

    def body(b_ref, c_ref, u_ref, w_ref, bias_ref, g_ref, mixed_ref, o_ref):
        bv, _, _, _, conv, _, _ = _conv_parts(b_ref, c_ref, u_ref, w_ref, bias_ref)
        bo = bv * conv
        rb = lax.rsqrt(jnp.mean(bo * bo, axis=-1, keepdims=True) + NORM_EPS)
        o_ref[...] = ((bo * rb) * g_ref[...]).astype(BF16)

    def slab(s):
        return pl.BlockSpec((None, seq, tl), lambda j, s=s: (s, 0, j))

    vec = pl.BlockSpec((1, tl), lambda j: (0, j))
    return pl.pallas_call(
        body, name="conv_forward", grid=(cols // tl,),
        in_specs=[slab(1), slab(2), slab(3), pl.BlockSpec((3, tl), lambda j: (0, j)), vec, vec, ANY],
        out_specs=pl.BlockSpec((seq, tl), lambda j: (0, first + j)),
        out_shape=jax.ShapeDtypeStruct(mixed.shape, mixed.dtype),
        input_output_aliases={6: 0},
        compiler_params=_params("parallel"),
    )(proj, proj, proj, conv_w, conv_b, gnorm_g, mixed)


def _pool_mix_backward(dmixed, pooled, w_pm, pool_scale, gnorm_g, after=()):
    seq, cols = pooled.shape
    _, n_groups, shard_rows, group_dim = w_pm.shape
    tm = _tile(seq, 512)

    def body(dm_ref, p_ref, w_ref, ps_ref, g_ref, *rest):
        dp_ref, dpm_ref, gg_ref, gps_ref = rest[-4:]

        @pl.when(pl.program_id(1) == 0)
        def _():
            gg_ref[...] = jnp.zeros_like(gg_ref)
            gps_ref[...] = jnp.zeros_like(gps_ref)

        w = _group_matrix(w_ref)
        ps = ps_ref[...]
        a_pre = jnp.dot(p_ref[...], w, preferred_element_type=F32)
        a = a_pre * ps
        ra = lax.rsqrt(jnp.mean(a * a, axis=-1, keepdims=True) + NORM_EPS)
        an = a * ra
        dm = dm_ref[...].astype(F32)
        gg_ref[...] += jnp.sum(dm * an, axis=0, keepdims=True)
        dan = dm * g_ref[...]
        da = ra * (dan - an * jnp.mean(dan * an, axis=-1, keepdims=True))
        gps_ref[...] += jnp.sum(da * a_pre, axis=0, keepdims=True)
        dpm = (da * ps).astype(BF16)
        dpm_ref[...] = dpm
        dp_ref[...] = lax.dot_general(dpm, w, NT, preferred_element_type=F32)

    tile = pl.BlockSpec((tm, group_dim), lambda g, i: (i, g))
    vec = pl.BlockSpec((1, group_dim), lambda g, i: (0, g))
    vshape = jax.ShapeDtypeStruct((1, cols), F32)
    return pl.pallas_call(
        body, name="pool_mix_backward", grid=(n_groups, seq // tm),
        in_specs=[tile, tile, pl.BlockSpec((N_CHIPS, None, shard_rows, group_dim), lambda g, i: (0, g, 0, 0)),
                  vec, vec] + [ANY] * len(after),
        out_specs=[tile, tile, vec, vec],
        out_shape=[jax.ShapeDtypeStruct((seq, cols), F32), jax.ShapeDtypeStruct((seq, cols), BF16), vshape, vshape],
        compiler_params=_params("parallel", "arbitrary"),
    )(dmixed, pooled, w_pm, pool_scale, gnorm_g, *after)


def _pool_mix_weight_grad(pooled, dpm, n_groups):
    seq, cols = pooled.shape
    group_dim = cols // n_groups
    shard_rows = group_dim // N_CHIPS
    tk = _tile(seq, 1024)
    gk = seq // tk

    def body(p_ref, d_ref, o_ref, acc_ref):
        k = pl.program_id(1)

        @pl.when(k == 0)
        def _():
            acc_ref[...] = jnp.zeros_like(acc_ref)

        acc_ref[...] += lax.dot_general(p_ref[...], d_ref[...], TN, preferred_element_type=F32)

        @pl.when(k == gk - 1)
        def _():
            for r in range(N_CHIPS):
                o_ref[r] = acc_ref[r * shard_rows:(r + 1) * shard_rows, :].astype(BF16)

    tile = pl.BlockSpec((tk, group_dim), lambda g, k: (k, g))
    return pl.pallas_call(
        body, name="pool_mix_weight_grad", grid=(n_groups, gk),
        in_specs=[tile, tile],
        out_specs=pl.BlockSpec((N_CHIPS, None, shard_rows, group_dim), lambda g, k: (0, g, 0, 0)),
        out_shape=jax.ShapeDtypeStruct((N_CHIPS, n_groups, shard_rows, group_dim), BF16),
        scratch_shapes=[pltpu.VMEM((group_dim, group_dim), F32)],
        compiler_params=_params("parallel", "arbitrary"),
    )(pooled, dpm)


def _mixers_backward(dpooled, dmixed, proj, conv_w, conv_b, gnorm_g, group_dim):
    _, seq, cols = proj.shape
    tl = CONV_HEAD_DIM
    first = cols // tl
    per_group = group_dim // tl
    n_groups = cols // group_dim

    def body(dp_ref, dm_ref, b_ref, c_ref, u_ref, w_ref, bias_ref, g_ref, o_ref, gg_ref, gb_ref, gw_ref):
        j = pl.program_id(0)
        for gi, window in enumerate(POOL_WINDOWS[:n_groups]):
            @pl.when(j // per_group == gi)
            def _(window=window):
                half = window // 2
                dp = dp_ref[...]
                t = lax.broadcasted_iota(jnp.int32, dp.shape, 0)
                dq = dp / _window_count(seq, half)
                o_ref[0] = (_window_sum(dq, range(-half + 1, half + 1), t) - dp).astype(BF16)

        bv, cu, prev, nxt, conv, w, t = _conv_parts(b_ref, c_ref, u_ref, w_ref, bias_ref)
        bo = bv * conv
        rb = lax.rsqrt(jnp.mean(bo * bo, axis=-1, keepdims=True) + NORM_EPS)
        bn = bo * rb
        dm = dm_ref[...].astype(F32)
        gg_ref[...] = jnp.sum(dm * bn, axis=0, keepdims=True)
        dbn = dm * g_ref[...]
        dbo = rb * (dbn - bn * jnp.mean(dbn * bn, axis=-1, keepdims=True))
        o_ref[1] = (dbo * conv).astype(BF16)
        dconv = dbo * bv
        gb_ref[...] = jnp.sum(dconv, axis=0, keepdims=True)
        gw_ref[0:1, :] = jnp.sum(dconv * prev, axis=0, keepdims=True)
        gw_ref[1:2, :] = jnp.sum(dconv * cu, axis=0, keepdims=True)
        gw_ref[2:3, :] = jnp.sum(dconv * nxt, axis=0, keepdims=True)
        dcu = w[0:1] * _shifted(dconv, 1, t) + w[1:2] * dconv + w[2:3] * _shifted(dconv, -1, t)
        o_ref[2] = (dcu * u_ref[...].astype(F32)).astype(BF16)
        o_ref[3] = (dcu * c_ref[...].astype(F32)).astype(BF16)

    def slab(s):
        return pl.BlockSpec((None, seq, tl), lambda j, s=s: (s, 0, j))

    vec = pl.BlockSpec((1, tl), lambda j: (0, j))
    rows3 = pl.BlockSpec((3, tl), lambda j: (0, j))
    vshape = jax.ShapeDtypeStruct((1, cols), F32)
    return pl.pallas_call(
        body, name="mixers_backward", grid=(cols // tl,),
        in_specs=[pl.BlockSpec((seq, tl), lambda j: (0, j)), pl.BlockSpec((seq, tl), lambda j: (0, first + j)),
                  slab(1), slab(2), slab(3), rows3, vec, vec],
        out_specs=[pl.BlockSpec((N_CHIPS, seq, tl), lambda j: (0, 0, j)), vec, vec, rows3],
        out_shape=[jax.ShapeDtypeStruct((N_CHIPS, seq, cols), BF16), vshape, vshape,
                   jax.ShapeDtypeStruct((3, cols), F32)],
        compiler_params=_params("parallel"),
    )(dpooled, dmixed, proj, proj, proj, conv_w, conv_b, gnorm_g)


class _GradReduction:
    def __init__(self, tag, grads, states, pair_id, scatter_id, position):
        self.tag, self.grads, self.states = tag, grads, states
        self.pair_id, self.scatter_id = pair_id, scatter_id
        self.chip_core, self.core, self.other_core = position

    def exchange(self):
        self.received = _exchange_halves(f"exchange_{self.tag}", self.grads, self.pair_id)

    def combine(self, after=()):
        self.parts = [_add_half(f"add_half_{self.tag}_{a}", g, r, self.core, after)
                      for a, (g, r) in enumerate(zip(self.grads, self.received))]

    def scatter(self):
        self.landed = _scatter_partials(f"scatter_{self.tag}", self.parts, self.scatter_id)

    def reduce(self, after=()):
        done = [_reduce_update(f"reduce_update_{self.tag}_{a}", p, l, *state, self.chip_core, after)
                for a, (p, l, state) in enumerate(zip(self.parts, self.landed, self.states))]
        self.reduced = [reduced for reduced, _ in done]
        self.mine = [updated for _, updated in done]

    def swap(self):
        self.swapped = _swap_reduced(f"swap_{self.tag}", self.reduced, self.pair_id)

    def update_other(self, after=()):
        self.results = [_adamw_half(f"adamw_other_{self.tag}_{a}", g, *state, self.other_core, done, after)
                        for a, (g, state, done) in enumerate(zip(self.swapped, self.states, self.mine))]

    def token(self, stage):
        first = getattr(self, stage)[0]
        return first if not isinstance(first, (list, tuple)) else first[0]


def kernel(x, c, w_ada, b_ada, norm1_g, w_in, pool_mix_w, pool_scale, conv_w, conv_b, gnorm_pool_g, gnorm_conv_g, w_out, norm2_g, w_mlp_in, w_mlp_out, final_g, loss_target, m_w_ada, m_b_ada, m_norm1_g, m_w_in, m_pool_mix_w, m_pool_scale, m_conv_w, m_conv_b, m_gnorm_pool_g, m_gnorm_conv_g, m_w_out, m_norm2_g, m_w_mlp_in, m_w_mlp_out, m_final_g, v_w_ada, v_b_ada, v_norm1_g, v_w_in, v_pool_mix_w, v_pool_scale, v_conv_w, v_conv_b, v_gnorm_pool_g, v_gnorm_conv_g, v_w_out, v_norm2_g, v_w_mlp_in, v_w_mlp_out, v_final_g):
    seq, d_model = x.shape[1], x.shape[2]
    cols = w_in.shape[2]
    n_groups, group_dim = pool_mix_w.shape[1], pool_mix_w.shape[3]
    shard_rows = pool_mix_w.shape[2]
    ff_cols = w_mlp_in.shape[2]
    ada_cols = w_ada.shape[2]
    conv_shard = conv_w.shape[2]
    assert pool_scale.shape[1] == cols and conv_b.shape[1] == cols and n_groups * group_dim == cols
    assert cols % CONV_HEAD_DIM == 0 and group_dim % CONV_HEAD_DIM == 0 and shard_rows * N_CHIPS == group_dim

    ix, iy, ic = _position()
    chip = 2 * ix + iy
    me = 4 * ix + 2 * iy + ic
    position = tuple(jnp.stack(v).astype(jnp.int32) for v in ([chip, ic], [ic], [1 - ic]))

    xs, target = x[0], loss_target[0]
    final_row = final_g.reshape(1, d_model)

    small = _gather_flat("gather_cond", jnp.concatenate([c[0], conv_w[0].reshape(-1)]))
    c_all = small[:, :d_model]
    conv_w_full = jnp.concatenate(
        [small[2 * j, d_model:].reshape(3, conv_shard) for j in range(N_CHIPS)], axis=1)
    b_cols = lax.dynamic_slice_in_dim(b_ada, chip * ada_cols, ada_cols, axis=1)
    mod_part = _ada_forward(c_all, w_ada[0], b_cols)
    mod_all = _gather_flat("gather_mod", mod_part.reshape(-1)).reshape(N_DEV, N_DEV, ada_cols)
    mod = jnp.concatenate(
        [lax.dynamic_slice_in_dim(mod_all[2 * j], me, 1, axis=0) for j in range(N_CHIPS)], axis=1)
    shift1, scale1, gate1, shift2, scale2, gate2 = [mod[:, i * d_model:(i + 1) * d_model] for i in range(N_MOD)]

    shards = [w_in[0].astype(BF16), pool_mix_w[0].reshape(n_groups * shard_rows, group_dim).astype(BF16),
              w_out[0].astype(BF16), w_mlp_in[0].astype(BF16), w_mlp_out[0].astype(BF16)]
    wg_in, wg_pm = _gather_weights("gather_w_in", shards[0:2], 1)
    (wg_out,) = _gather_weights("gather_w_out", shards[2:3], 2)
    (wg_1,) = _gather_weights("gather_w_mlp_in", shards[3:4], 3)
    (wg_2,) = _gather_weights("gather_w_mlp_out", shards[4:5], 4)
    wg_in, wg_pm, wg_out, wg_1, wg_2 = [
        _place_own(f"place_own_{a}", g, s, position[0])
        for a, (g, s) in enumerate(zip([wg_in, wg_pm, wg_out, wg_1, wg_2], shards))]
    wg_pm = wg_pm.reshape(N_CHIPS, n_groups, shard_rows, group_dim)
    wg_out = wg_out.reshape(d_model, d_model)
    wg_2 = wg_2.reshape(N_CHIPS * ff_cols, d_model)

    h1 = _norm_modulate("norm_modulate_1", xs, norm1_g, scale1, shift1)
    proj = _in_projection(h1, shards[0], wg_in, position[0])
    pooled = _pool_forward(proj, group_dim)
    mixed = _pool_mix_forward(pooled, wg_pm, pool_scale, gnorm_pool_g, d_model)
    mixed = _conv_forward(proj, conv_w_full, conv_b, gnorm_conv_g, mixed)
    x1, attn = _residual_projection("out_projection", mixed, wg_out, xs, gate1)
    h2 = _norm_modulate("norm_modulate_2", x1, norm2_g, scale2, shift2)
    act = _mlp_in(h2, shards[3], wg_1, position[0])
    x2, mlp = _mlp_out(act, shards[4], wg_2.reshape(N_CHIPS, ff_cols, d_model), x1, gate2, position[0])

    dx2, dmlp, g_final, dgate2, sq_err = _loss_head(x2, target, final_row, mlp, gate2)
    gw_2 = _weight_grad("grad_w_mlp_out", act, dmlp, prologue=_square)
    red_2 = _GradReduction("w_mlp_out", [gw_2.reshape(N_CHIPS, ff_cols, d_model)],
                           [(w_mlp_out[0], m_w_mlp_out[0], v_w_mlp_out[0])], 8, 12, position)
    red_2.exchange()
    dhid = _grad_hidden(dmlp, wg_2, act, after=[gw_2])
    red_2.combine(after=[dhid])
    red_2.scatter()
    gw_1 = _weight_grad_slabs("grad_w_mlp_in", h2, dhid, ff_cols, after=[red_2.token("parts")])
    red_1 = _GradReduction("w_mlp_in", [gw_1], [(w_mlp_in[0], m_w_mlp_in[0], v_w_mlp_in[0])], 7, 11, position)
    red_1.exchange()
    dh2 = _grad_input_slabs("grad_h2", dhid, wg_1, after=[gw_1])
    red_2.reduce(after=[dh2])
    red_2.swap()
    red_1.combine(after=[red_2.token("mine")])
    red_1.scatter()
    dx1, dscale2, dshift2, g_norm2, dgate1, dattn = _norm_modulate_backward(
        "norm_modulate_backward_2", dh2, x1, norm2_g, scale2, dx2, attn, gate1, after=[red_1.token("parts")])
    gw_out = _weight_grad("grad_w_out", mixed, dattn)
    red_out = _GradReduction("w_out", [gw_out.reshape(N_CHIPS, d_model // N_CHIPS, d_model)],
                             [(w_out[0], m_w_out[0], v_w_out[0])], 6, 10, position)
    red_out.exchange()
    dmixed = _grad_input("grad_mixed", dattn, wg_out)
    dpooled, dpm, g_gpool, g_pscale = _pool_mix_backward(dmixed, pooled, wg_pm, pool_scale, gnorm_pool_g)
    gw_pm = _pool_mix_weight_grad(pooled, dpm, n_groups)
    dproj, g_gconv, g_convb, g_convw = _mixers_backward(
        dpooled, dmixed, proj, conv_w_full, conv_b, gnorm_conv_g, group_dim)
    gw_in = _weight_grad_slabs("grad_w_in", h1, dproj, None)
    pm2d = (n_groups * shard_rows, group_dim)
    red_in = _GradReduction(
        "w_in", [gw_in, gw_pm.reshape((N_CHIPS,) + pm2d)],
        [(w_in[0], m_w_in[0], v_w_in[0]),
         (pool_mix_w[0].reshape(pm2d), m_pool_mix_w[0].reshape(pm2d), v_pool_mix_w[0].reshape(pm2d))],
        5, 9, position)
    red_in.exchange()
    red_1.reduce(after=[gw_in])
    red_1.swap()
    red_out.combine(after=[red_1.token("mine")])
    red_out.scatter()
    red_in.combine(after=[red_out.token("parts")])
    red_in.scatter()
    dh1 = _grad_input_slabs("grad_h1", dproj, wg_in, after=[red_in.token("parts")])
    red_2.update_other(after=[dh1])
    grad_x, dscale1, dshift1, g_norm1 = _norm_modulate_backward(
        "norm_modulate_backward_1", dh1, xs, norm1_g, scale1, dx1, after=[red_2.token("results")])
    red_out.reduce(after=[grad_x])
    red_out.swap()
    red_1.update_other(after=[red_out.token("mine")])

    mine = jnp.concatenate(
        [dshift1, dscale1, dgate1, dshift2, dscale2, dgate2, g_norm1, g_norm2, g_final, sq_err,
         g_pscale, g_convb, g_gpool, g_gconv, g_convw.reshape(1, 3 * cols)], axis=1)
    gathered = _gather_flat("gather_small", mine.reshape(-1))
    sums, loss = _reduce_small(gathered, d_model)
    n_rep = (N_MOD + 3) * d_model
    g_rep = jnp.concatenate([sums[:, :n_rep], sums[:, n_rep + d_model:n_rep + d_model + 4 * cols]], axis=1)
    n_small = g_rep.shape[1]

    def pack(b, n1, n2, fg, ps, cb, gp, gc):
        return jnp.concatenate([b, n1, n2, fg.reshape(1, d_model), ps, cb, gp, gc], axis=1).reshape(8, n_small // 8)

    d_rep, m_rep, v_rep = _adamw(
        "adamw_small", g_rep.reshape(8, n_small // 8),
        pack(b_ada, norm1_g, norm2_g, final_g, pool_scale, conv_b, gnorm_pool_g, gnorm_conv_g),
        pack(m_b_ada, m_norm1_g, m_norm2_g, m_final_g, m_pool_scale, m_conv_b, m_gnorm_pool_g, m_gnorm_conv_g),
        pack(v_b_ada, v_norm1_g, v_norm2_g, v_final_g, v_pool_scale, v_conv_b, v_gnorm_pool_g, v_gnorm_conv_g))

    def unpack(flat):
        flat = flat.reshape(1, n_small)
        sizes = [N_MOD * d_model, d_model, d_model, d_model, cols, cols, cols, cols]
        parts, at = [], 0
        for size in sizes:
            parts.append(flat[:, at:at + size])
            at += size
        parts[3] = parts[3].reshape(d_model)
        return parts

    g_convw_full = sums[:, n_rep + d_model + 4 * cols:].reshape(3, cols)
    g_convw_mine = lax.dynamic_slice_in_dim(g_convw_full, chip * conv_shard, conv_shard, axis=1)
    d_convw, m_convw, v_convw = _adamw("adamw_conv_w", g_convw_mine, conv_w[0], m_conv_w[0], v_conv_w[0])

    dmod_cols = lax.dynamic_slice_in_dim(gathered[:, :N_MOD * d_model], chip * ada_cols, ada_cols, axis=1)
    g_ada, d_ada, mn_ada, vn_ada = _ada_backward(c_all.T, dmod_cols, w_ada[0], m_w_ada[0], v_w_ada[0])

    red_in.reduce(after=[g_ada, red_1.token("results")])
    red_in.swap()
    red_out.update_other(after=[red_in.token("mine")])
    red_in.update_other(after=[red_out.token("results")])

    small_parts = [unpack(g_rep), unpack(d_rep), unpack(m_rep), unpack(v_rep)]
    ada_parts = [g_ada, d_ada, mn_ada, vn_ada]
    convw_parts = [g_convw_mine, d_convw, m_convw, v_convw]

    def ordered(k):
        b, n1, n2, fg, ps, cb, gp, gc = small_parts[k]
        return [ada_parts[k][None], b, n1, red_in.results[0][k][None],
                red_in.results[1][k].reshape(pool_mix_w.shape), ps, convw_parts[k][None], cb, gp, gc,
                red_out.results[0][k][None], n2, red_1.results[0][k][None], red_2.results[0][k][None], fg]

    return (loss[0, 0], grad_x[None], *ordered(0), *ordered(1), *ordered(2), *ordered(3))
```
